```python
import math
import jax
import jax.numpy as jnp
from jax import lax
import numpy as np

D_MODEL = 1024
BATCH = 4
SEQ = 4096
DEPTH = 1
DEC_BATCH = 32
DEC_SEQ = 1
PAST_LEN = 16384
PAGE_SIZE = 128

H_A = 4
DK_A = 128
DV_A = 128
W_QK_A = H_A * DK_A
W_V_A = H_A * DV_A
C_CONV = 2 * W_QK_A + W_V_A
CONV_W = 4
CHUNK = 64
H_B = 8
HD_B = 64
W_B = H_B * HD_B
MOBA_BLOCK = 256
MOBA_TOPK = 3
Q_BLOCK = 64
ROPE_THETA = 500000.0
ROT_DIM = HD_B // 4
D_FF = 4 * D_MODEL
EPS = 1e-6
SPLIT_SIZES = (C_CONV, W_V_A, H_A, H_A, W_B, W_B, W_B, D_MODEL, D_MODEL)
N_IN = C_CONV + W_V_A + 2 * H_A + 3 * W_B + 2 * D_MODEL
F32 = jnp.float32

kernel_name = 'hybrid_gated_delta_moba_step'


def rmsnorm(x, w):
    xf = x.astype(F32)
    y = xf * lax.rsqrt(jnp.mean(xf * xf, axis=-1, keepdims=True) + EPS) * w.astype(F32)
    return y.astype(x.dtype)


def l2norm(x):
    xf = x.astype(F32)
    return xf * lax.rsqrt(jnp.sum(xf * xf, axis=-1, keepdims=True) + EPS)


def partial_rope(x, pos):
    half = ROT_DIM // 2
    inv_freq = jnp.exp(jnp.arange(half, dtype=F32) * (-2.0 * math.log(ROPE_THETA) / ROT_DIM))
    ang = pos.astype(F32)[:, None] * inv_freq[None, :]
    cos = jnp.cos(ang)[None, :, None, :]
    sin = jnp.sin(ang)[None, :, None, :]
    xf = x.astype(F32)
    x1, x2 = xf[..., :half], xf[..., half:ROT_DIM]
    out = jnp.concatenate([x1 * cos - x2 * sin, x2 * cos + x1 * sin, xf[..., ROT_DIM:]], axis=-1)
    return out.astype(x.dtype)


def in_proj(x, norm_w, w_in):
    h = rmsnorm(x, norm_w) @ w_in
    offsets = np.cumsum(SPLIT_SIZES)[:-1].tolist()
    return jnp.split(h, offsets, axis=-1)


def causal_conv_silu(x, buf, w):
    length = x.shape[1]
    xp = jnp.concatenate([buf.astype(x.dtype), x], axis=1)
    y = xp[:, 0:length] * w[0]
    for i in range(1, CONV_W):
        y = y + xp[:, i:i + length] * w[i]
    return jax.nn.silu(y), xp[:, -(CONV_W - 1):]


def delta_prep(qkv_pre, conv_buf, b_logit, a_logit, conv_w, a_log, dt_bias):
    bsz, length, _ = qkv_pre.shape
    qkv, new_buf = causal_conv_silu(qkv_pre, conv_buf, conv_w)
    q = l2norm(qkv[..., :W_QK_A].reshape(bsz, length, H_A, DK_A)) * (DK_A ** -0.5)
    k = l2norm(qkv[..., W_QK_A:2 * W_QK_A].reshape(bsz, length, H_A, DK_A))
    v = qkv[..., 2 * W_QK_A:].reshape(bsz, length, H_A, DV_A).astype(F32)
    beta = jax.nn.sigmoid(b_logit.astype(F32))
    g = -jnp.exp(a_log.astype(F32)) * jax.nn.softplus(a_logit.astype(F32) + dt_bias.astype(F32))
    return q, k, v, g, beta, new_buf


def gated_delta_chunked(q, k, v, g, beta, s0):
    b, s, h, _ = q.shape
    dv = v.shape[-1]
    nc = s // CHUNK

    def chunks(t):
        t = t.reshape((b, nc, CHUNK, h) + t.shape[3:])
        return jnp.moveaxis(t, 3, 1)

    qc, kc, vc, bc = chunks(q), chunks(k), chunks(v), chunks(beta)
    gc = jnp.cumsum(chunks(g), axis=-1)
    causal = jnp.tril(jnp.ones((CHUNK, CHUNK), bool))
    strict = jnp.tril(jnp.ones((CHUNK, CHUNK), bool), -1)
    decay = jnp.exp(jnp.where(causal, gc[..., :, None] - gc[..., None, :], -jnp.inf))
    kb = kc * bc[..., None]
    a = jnp.where(strict, jnp.einsum('bhnid,bhnjd->bhnij', kb, kc) * decay, 0.0)
    eye = jnp.eye(CHUNK, dtype=a.dtype)
    t_inv = lax.linalg.triangular_solve(a + eye, jnp.broadcast_to(eye, a.shape),
                                        left_side=True, lower=True, unit_diagonal=True)
    u = t_inv @ (vc * bc[..., None])
    w = t_inv @ (kb * jnp.exp(gc)[..., None])
    qk = jnp.where(causal, jnp.einsum('bhnid,bhnjd->bhnij', qc, kc) * decay, 0.0)

    def step(state, xs):
        qi, ki, ui, wi, gi, qki = xs
        v_new = ui - wi @ state
        o = (qi * jnp.exp(gi)[..., None]) @ state + qki @ v_new
        g_last = gi[..., -1:]
        state = state * jnp.exp(g_last)[..., None] + jnp.einsum(
            'bhcd,bhce->bhde', ki * jnp.exp(g_last - gi)[..., None], v_new)
        return state, o

    xs = tuple(jnp.moveaxis(t, 2, 0) for t in (qc, kc, u, w, gc, qk))
    s_fin, o = lax.scan(step, s0, xs)
    o = jnp.moveaxis(jnp.moveaxis(o, 0, 2), 1, 3).reshape(b, s, h, dv)
    return o, s_fin


def gated_delta_recurrent(q, k, v, g, beta, s0):
    def step(state, xs):
        qt, kt, vt, gt, bt = xs
        state = state * jnp.exp(gt)[..., None, None]
        kv = jnp.einsum('bhd,bhde->bhe', kt, state)
        state = state + kt[..., :, None] * ((vt - kv) * bt[..., None])[..., None, :]
        return state, jnp.einsum('bhd,bhde->bhe', qt, state)

    xs = tuple(jnp.moveaxis(t, 1, 0) for t in (q, k, v, g, beta))
    s_fin, o = lax.scan(step, s0, xs)
    return jnp.moveaxis(o, 0, 1), s_fin


def delta_out(o, z, norm_w, w_proj):
    bsz, length = o.shape[:2]
    on = o * lax.rsqrt(jnp.mean(o * o, axis=-1, keepdims=True) + EPS) * norm_w.astype(F32)
    on = on * jax.nn.silu(z.reshape(bsz, length, H_A, DV_A).astype(F32))
    return on.reshape(bsz, length, W_V_A).astype(z.dtype) @ w_proj


def moba_combine(q, q_pos, k_own, v_own, own_pos, ks=None, vs=None, sel_valid=None):
    scale = HD_B ** -0.5
    lo = jnp.einsum('bhqd,bhld->bhql', q, k_own, preferred_element_type=F32) * scale
    lo = jnp.where(own_pos[None, :] <= q_pos[:, None], lo, -jnp.inf)
    if ks is None:
        p = jax.nn.softmax(lo, axis=-1)
        return jnp.einsum('bhql,bhld->bhqd', p.astype(v_own.dtype), v_own, preferred_element_type=F32)
    ls = jnp.einsum('bhqd,bhqkld->bhqkl', q, ks, preferred_element_type=F32) * scale
    if sel_valid is not None:
        ls = jnp.where(sel_valid[..., None], ls, -jnp.inf)
    b, h, nq, nk, bs = ls.shape
    p = jax.nn.softmax(jnp.concatenate([ls.reshape(b, h, nq, nk * bs), lo], axis=-1), axis=-1)
    p_sel = p[..., :nk * bs].reshape(b, h, nq, nk, bs)
    p_own = p[..., nk * bs:]
    return (jnp.einsum('bhqkl,bhqkld->bhqd', p_sel.astype(vs.dtype), vs, preferred_element_type=F32)
            + jnp.einsum('bhql,bhld->bhqd', p_own.astype(v_own.dtype), v_own, preferred_element_type=F32))


def moba_prompt(q, k, v):
    b, s = q.shape[:2]
    nb = -(-s // MOBA_BLOCK)
    pad = nb * MOBA_BLOCK - s

    def blocks(t):
        t = jnp.pad(t, ((0, 0), (0, pad), (0, 0), (0, 0)))
        return t.reshape(b, nb, MOBA_BLOCK, H_B, HD_B).transpose(0, 3, 1, 2, 4)

    kb, vb = blocks(k), blocks(v)
    k_mean = jnp.mean(kb.astype(F32), axis=3)
    kk = min(MOBA_TOPK, nb - 1)
    nq = s // Q_BLOCK
    qc = q.reshape(b, nq, Q_BLOCK, H_B, HD_B).transpose(1, 0, 3, 2, 4)
    b_i = jnp.arange(b)[:, None, None, None]
    h_i = jnp.arange(H_B)[None, :, None, None]

    def one(args):
        ci, qb = args
        start = ci * Q_BLOCK
        blk = start // MOBA_BLOCK
        q_pos = start + jnp.arange(Q_BLOCK)
        k_own = lax.dynamic_index_in_dim(kb, blk, axis=2, keepdims=False)
        v_own = lax.dynamic_index_in_dim(vb, blk, axis=2, keepdims=False)
        own_pos = blk * MOBA_BLOCK + jnp.arange(MOBA_BLOCK)
        if kk == 0:
            return moba_combine(qb, q_pos, k_own, v_own, own_pos)
        scores = jnp.einsum('bhqd,bhnd->bhqn', qb, k_mean, preferred_element_type=F32)
        scores = jnp.where(jnp.arange(nb) < blk, scores, -jnp.inf)
        _, top_i = lax.top_k(scores, kk)
        valid = jnp.arange(kk) < blk
        ks = kb[b_i, h_i, top_i]
        vs = vb[b_i, h_i, top_i]
        return moba_combine(qb, q_pos, k_own, v_own, own_pos, ks, vs, valid)

    o = lax.map(one, (jnp.arange(nq), qc))
    return o.transpose(1, 0, 3, 2, 4).reshape(b, s, W_B)


def moba_sample(q, k_new, v_new, cache_k, cache_v, page_table, layer):
    nbat, length = q.shape[:2]
    ppb = MOBA_BLOCK // PAGE_SIZE
    n_full = PAST_LEN // MOBA_BLOCK
    n_rem = (PAST_LEN - n_full * MOBA_BLOCK) // PAGE_SIZE
    q_pos = PAST_LEN + jnp.arange(length)
    qh = q.transpose(0, 2, 1, 3)
    own_pages = page_table[:, n_full * ppb:n_full * ppb + n_rem]

    def own_rows(pool, new):
        past = pool[layer, own_pages].reshape(nbat, n_rem * PAGE_SIZE, H_B, HD_B).astype(new.dtype)
        return jnp.concatenate([past, new], axis=1).transpose(0, 2, 1, 3)

    k_own = own_rows(cache_k, k_new)
    v_own = own_rows(cache_v, v_new)
    own_pos = n_full * MOBA_BLOCK + jnp.arange(n_rem * PAGE_SIZE + length)
    kk = min(MOBA_TOPK, n_full)
    if kk == 0:
        o = moba_combine(qh, q_pos, k_own, v_own, own_pos)
    else:
        k_past = cache_k[layer, page_table[:, :n_full * ppb]].reshape(nbat, n_full, MOBA_BLOCK, H_B, HD_B)
        k_mean = jnp.mean(k_past.astype(F32), axis=2).transpose(0, 2, 1, 3)
        scores = jnp.einsum('bhqd,bhnd->bhqn', qh, k_mean, preferred_element_type=F32)
        _, top_i = lax.top_k(scores, kk)
        b_i = jnp.arange(nbat)[:, None, None, None]
        h_i = jnp.arange(H_B)[None, :, None, None]
        k_sel = k_past[b_i, top_i, :, h_i]
        phys = page_table[b_i[..., None], top_i[..., None] * ppb + jnp.arange(ppb)]
        v_sel = cache_v[layer, phys, :, h_i[..., None]].reshape(nbat, H_B, length, kk, MOBA_BLOCK, HD_B)
        o = moba_combine(qh, q_pos, k_own, v_own, own_pos, k_sel, v_sel)
    return o.transpose(0, 2, 1, 3).reshape(nbat, length, W_B)


def merge_and_mlp(x, y_a, y_b, gate_a, gate_b, w_out, norm2_w, w_up, w_down):
    mixed = jax.nn.sigmoid(gate_a) * y_a + jax.nn.sigmoid(gate_b) * y_b
    h = x + mixed @ w_out
    u = jnp.square(jax.nn.relu(rmsnorm(h, norm2_w) @ w_up))
    return h + u @ w_down


def setup_inputs(seed: int = 0) -> dict:
    key = jax.random.key(seed)
    keys = jax.random.split(key, 20)
    n_pages = PAST_LEN // PAGE_SIZE
    n_pool = (DEC_BATCH * n_pages * 5) // 4

    def normal(k, shape, scale):
        return jax.random.normal(k, shape, F32) * scale

    def gain(k, shape):
        return 1.0 + normal(k, shape, 0.02)

    perm = jax.random.permutation(keys[4], n_pool)
    page_table = perm[:DEC_BATCH * n_pages].reshape(DEC_BATCH, n_pages).astype(jnp.int32)
    dt = jnp.exp(jax.random.uniform(keys[9], (DEPTH, H_A), F32, math.log(1e-3), math.log(1e-1)))
    return {
        'x_prompt': normal(keys[0], (BATCH, SEQ, D_MODEL), 1.0),
        'x_sample': normal(keys[1], (DEC_BATCH, DEC_SEQ, D_MODEL), 1.0),
        'cache_k': normal(keys[2], (DEPTH, n_pool, PAGE_SIZE, H_B, HD_B), 1.0),
        'cache_v': normal(keys[3], (DEPTH, n_pool, PAGE_SIZE, H_B, HD_B), 1.0),
        'page_table': page_table,
        'state_delta': normal(keys[5], (DEPTH, DEC_BATCH, H_A, DK_A, DV_A), 0.05),
        'state_conv': normal(keys[6], (DEPTH, DEC_BATCH, CONV_W - 1, C_CONV), 1.0),
        'norm1_w': gain(keys[7], (DEPTH, D_MODEL)),
        'w_in': normal(keys[8], (DEPTH, D_MODEL, N_IN), D_MODEL ** -0.5),
        'conv_w': normal(keys[10], (DEPTH, CONV_W, C_CONV), CONV_W ** -0.5),
        'a_log': jnp.log(jax.random.uniform(keys[11], (DEPTH, H_A), F32, 1.0, 16.0)),
        'dt_bias': dt + jnp.log(-jnp.expm1(-dt)),
        'delta_norm_w': gain(keys[12], (DEPTH, DV_A)),
        'w_proj_a': normal(keys[13], (DEPTH, W_V_A, D_MODEL), W_V_A ** -0.5),
        'w_proj_b': normal(keys[14], (DEPTH, W_B, D_MODEL), W_B ** -0.5),
        'w_out': normal(keys[15], (DEPTH, D_MODEL, D_MODEL), D_MODEL ** -0.5),
        'norm2_w': gain(keys[16], (DEPTH, D_MODEL)),
        'w_up': normal(keys[17], (DEPTH, D_MODEL, D_FF), D_MODEL ** -0.5),
        'w_down': normal(keys[18], (DEPTH, D_FF, D_MODEL), D_FF ** -0.5),
        'norm_f_w': gain(keys[19], (D_MODEL,)),
    }


def reference(x_prompt, x_sample, cache_k, cache_v, page_table, state_delta, state_conv,
              norm1_w, w_in, conv_w, a_log, dt_bias, delta_norm_w, w_proj_a, w_proj_b,
              w_out, norm2_w, w_up, w_down, norm_f_w):
    bp, sp = x_prompt.shape[:2]
    bs, ss = x_sample.shape[:2]
    pos_p = jnp.arange(sp, dtype=jnp.int32)
    pos_s = PAST_LEN + jnp.arange(ss, dtype=jnp.int32)
    hp, hs = x_prompt, x_sample
    kp_l, vp_l, dp_l, cp_l = [], [], [], []
    ks_l, vs_l, ds_l, cs_l = [], [], [], []
    for l in range(DEPTH):
        qkv, z, b_lg, a_lg, qb, kb, vb, ga, gb = in_proj(hp, norm1_w[l], w_in[l])
        q, k, v, g, beta, conv_new = delta_prep(qkv, jnp.zeros((bp, CONV_W - 1, C_CONV), hp.dtype),
                                                b_lg, a_lg, conv_w[l], a_log[l], dt_bias[l])
        o_a, s_new = gated_delta_chunked(q, k, v, g, beta, jnp.zeros((bp, H_A, DK_A, DV_A), F32))
        y_a = delta_out(o_a, z, delta_norm_w[l], w_proj_a[l])
        qr = partial_rope(qb.reshape(bp, sp, H_B, HD_B), pos_p)
        kr = partial_rope(kb.reshape(bp, sp, H_B, HD_B), pos_p)
        vr = vb.reshape(bp, sp, H_B, HD_B)
        y_b = moba_prompt(qr, kr, vr).astype(hp.dtype) @ w_proj_b[l]
        hp = merge_and_mlp(hp, y_a, y_b, ga, gb, w_out[l], norm2_w[l], w_up[l], w_down[l])
        kp_l.append(kr.astype(cache_k.dtype))
        vp_l.append(vr.astype(cache_v.dtype))
        dp_l.append(s_new.astype(state_delta.dtype))
        cp_l.append(conv_new.astype(state_conv.dtype))
        qkv, z, b_lg, a_lg, qb, kb, vb, ga, gb = in_proj(hs, norm1_w[l], w_in[l])
        q, k, v, g, beta, conv_new = delta_prep(qkv, state_conv[l], b_lg, a_lg,
                                                conv_w[l], a_log[l], dt_bias[l])
        o_a, s_new = gated_delta_recurrent(q, k, v, g, beta, state_delta[l].astype(F32))
        y_a = delta_out(o_a, z, delta_norm_w[l], w_proj_a[l])
        qr = partial_rope(qb.reshape(bs, ss, H_B, HD_B), pos_s)
        kr = partial_rope(kb.reshape(bs, ss, H_B, HD_B), pos_s)
        vr = vb.reshape(bs, ss, H_B, HD_B)
        y_b = moba_sample(qr, kr, vr, cache_k, cache_v, page_table, l).astype(hs.dtype) @ w_proj_b[l]
        hs = merge_and_mlp(hs, y_a, y_b, ga, gb, w_out[l], norm2_w[l], w_up[l], w_down[l])
        ks_l.append(kr.astype(cache_k.dtype))
        vs_l.append(vr.astype(cache_v.dtype))
        ds_l.append(s_new.astype(state_delta.dtype))
        cs_l.append(conv_new.astype(state_conv.dtype))
    y_prompt = rmsnorm(hp, norm_f_w)
    y_sample = rmsnorm(hs, norm_f_w)
    k_prompt = jnp.stack(kp_l)
    v_prompt = jnp.stack(vp_l)
    state_delta_prompt = jnp.stack(dp_l)
    state_conv_prompt = jnp.stack(cp_l)
    k_sample = jnp.stack(ks_l)
    v_sample = jnp.stack(vs_l)
    state_delta_sample = jnp.stack(ds_l)
    state_conv_sample = jnp.stack(cs_l)
    return (y_prompt, y_sample, k_prompt, v_prompt, state_delta_prompt, state_conv_prompt,
            k_sample, v_sample, state_delta_sample, state_conv_sample)
```

```python
import functools
import math

import jax
import jax.numpy as jnp
from jax import lax
from jax.experimental import pallas as pl
from jax.experimental.pallas import tpu as pltpu

F32 = jnp.float32
BF16 = jnp.bfloat16
HIGHEST = lax.Precision.HIGHEST

H_A = 4
DK_A = 128
DV_A = 128
W_QK_A = H_A * DK_A
W_V_A = H_A * DV_A
C_CONV = 2 * W_QK_A + W_V_A
CONV_W = 4
CHUNK = 64
H_B = 8
HD_B = 64
W_B = H_B * HD_B
MOBA_BLOCK = 256
MOBA_TOPK = 3
ROPE_THETA = 500000.0
ROT_DIM = HD_B // 4
EPS = 1e-6

LANES = 128
SUBLANES = 8
HEADS_PER_LANE_TILE = LANES // HD_B
N_PAIR = H_B // HEADS_PER_LANE_TILE
LOGIT_PAD = LANES
VMEM_LIMIT = 56 * 1024 * 1024
NEG_INF = float("-inf")


def _dot(a, b, dims, precision=None):
    return lax.dot_general(a, b, (dims, ((), ())), precision=precision,
                           preferred_element_type=F32)


def _mm(a, b, precision=None):
    return _dot(a, b, ((1,), (0,)), precision)


def _mm_nt(a, b, precision=None):
    return _dot(a, b, ((1,), (1,)), precision)


def _mm_tn(a, b, precision=None):
    return _dot(a, b, ((0,), (0,)), precision)


def _bf(x):
    return x.astype(BF16)


def _sigmoid(x):
    return 1.0 / (1.0 + jnp.exp(-x))


def _silu(x):
    return x * _sigmoid(x)


def _softplus(x):
    return jnp.maximum(x, 0.0) + jnp.log(1.0 + jnp.exp(-jnp.abs(x)))


def _rms(x, w):
    return x * lax.rsqrt(jnp.mean(x * x, axis=-1, keepdims=True) + EPS) * w


def _iota(shape, dim):
    return lax.broadcasted_iota(jnp.int32, shape, dim)


def _const_spec(shape):
    nd = len(shape)
    return pl.BlockSpec(shape, lambda *_: (0,) * nd, pipeline_mode=pl.Buffered(1))


def _block_rank(sc, n_blocks):
    lane = _iota(sc.shape, 1)
    rank = jnp.zeros(sc.shape, F32)
    for j in range(n_blocks):
        cj = sc[:, j:j + 1]
        first_on_tie = jnp.where(lane > j, 1.0, 0.0)
        rank = rank + jnp.where(cj > sc, 1.0, jnp.where(cj == sc, first_on_tie, 0.0))
    return rank


def _in_proj_kernel(x_ref, nw_ref, w_ref, cos_ref, slo_ref, shi_ref, *rest,
                    d_model, tiles_per_seq, kv_transposed):
    if kv_transposed:
        (wkv_t_ref, cos_t_ref, sin_t_ref,
         qkv_ref, z_ref, qb_ref, kb_ref, vb_ref, ga_ref, gb_ref, lg_ref, conv_ref) = rest
    else:
        qkv_ref, z_ref, qb_ref, kb_ref, vb_ref, ga_ref, gb_ref, lg_ref = rest
    xb = _bf(_rms(x_ref[...], nw_ref[...]))
    tm = xb.shape[0]
    half = ROT_DIM // 2

    def cols(c0, n):
        return _mm(xb, w_ref[:, c0:c0 + n])

    def rope_store(dst_ref, c0):
        cos, slo, shi = cos_ref[...], slo_ref[...], shi_ref[...]
        for c in range(W_B // LANES):
            y = cols(c0 + c * LANES, LANES)
            up = pltpu.roll(y, LANES - half, 1)
            dn = pltpu.roll(y, half, 1)
            dst_ref[:, c * LANES:(c + 1) * LANES] = y * cos + up * slo + dn * shi

    step = 512
    c0 = 0
    for c in range(C_CONV // step):
        qkv_ref[:, c * step:(c + 1) * step] = cols(c0, step)
        c0 += step
    z_ref[...] = cols(c0, W_V_A)
    c0 += W_V_A
    rope_store(qb_ref, c0)
    c0 += W_B
    if kv_transposed:
        cos_t, sin_t = cos_t_ref[...], sin_t_ref[...]
        k_t = _mm_nt(wkv_t_ref[0:W_B, :], xb)
        for h in range(H_B):
            r = h * HD_B
            x1, x2 = k_t[r:r + half], k_t[r + half:r + ROT_DIM]
            kb_ref[0, r:r + half, :] = x1 * cos_t - x2 * sin_t
            kb_ref[0, r + half:r + ROT_DIM, :] = x2 * cos_t + x1 * sin_t
            kb_ref[0, r + ROT_DIM:r + HD_B, :] = k_t[r + ROT_DIM:r + HD_B]
        vb_ref[0] = _mm_nt(wkv_t_ref[W_B:2 * W_B, :], xb)
    else:
        rope_store(kb_ref, c0)
        vb_ref[...] = cols(c0 + W_B, W_B)
    c0 += 2 * W_B
    for g_ref in (ga_ref, gb_ref):
        for c in range(d_model // step):
            g_ref[:, c * step:(c + 1) * step] = cols(c0, step)
            c0 += step
    lg_ref[...] = cols(c0, LOGIT_PAD)

    if kv_transposed:
        @pl.when(pl.program_id(0) % tiles_per_seq == tiles_per_seq - 1)
        def _():
            conv_ref[0] = qkv_ref[tm - (CONV_W - 1):tm, :]


def _in_proj(x2d, norm_w, w_r, tabs, *, tm, rows_per_seq, wkv_t=None, tabs_t=None):
    rows, d_model = x2d.shape
    n_tiles = rows // tm
    tiles_per_seq = rows_per_seq // tm
    n_seq = rows // rows_per_seq
    kv_transposed = wkv_t is not None

    def row_spec(n):
        return pl.BlockSpec((tm, n), lambda i: (i, 0))

    def sds(*shape):
        return jax.ShapeDtypeStruct(shape, F32)

    kv_t_spec = pl.BlockSpec((1, W_B, tm), lambda i: (i // tiles_per_seq, 0, i % tiles_per_seq))
    kv_shape = sds(n_seq, W_B, rows_per_seq) if kv_transposed else sds(rows, W_B)
    kv_spec = kv_t_spec if kv_transposed else row_spec(W_B)
    tab_spec = pl.BlockSpec((tm, LANES), lambda i: (i % tiles_per_seq, 0))
    in_specs = [row_spec(d_model), _const_spec((1, d_model)), _const_spec(w_r.shape),
                tab_spec, tab_spec, tab_spec]
    args = [x2d, norm_w.reshape(1, d_model), w_r, *tabs]
    out_shape = [sds(rows, C_CONV), sds(rows, W_V_A), sds(rows, W_B), kv_shape, kv_shape,
                 sds(rows, d_model), sds(rows, d_model), sds(rows, LOGIT_PAD)]
    out_specs = [row_spec(C_CONV), row_spec(W_V_A), row_spec(W_B), kv_spec, kv_spec,
                 row_spec(d_model), row_spec(d_model), row_spec(LOGIT_PAD)]
    if kv_transposed:
        tab_t_spec = pl.BlockSpec((ROT_DIM // 2, tm), lambda i: (0, i % tiles_per_seq))
        in_specs += [_const_spec(wkv_t.shape), tab_t_spec, tab_t_spec]
        args += [wkv_t, *tabs_t]
        out_shape.append(sds(n_seq, CONV_W - 1, C_CONV))
        out_specs.append(pl.BlockSpec((1, CONV_W - 1, C_CONV),
                                      lambda i: (i // tiles_per_seq, 0, 0)))
    return pl.pallas_call(
        functools.partial(_in_proj_kernel, d_model=d_model, tiles_per_seq=tiles_per_seq,
                          kv_transposed=kv_transposed),
        grid=(n_tiles,),
        in_specs=in_specs,
        out_specs=out_specs,
        out_shape=out_shape,
        compiler_params=pltpu.CompilerParams(dimension_semantics=("arbitrary",),
                                             vmem_limit_bytes=VMEM_LIMIT),
    )(*args)


def _rope_angles(pos):
    half = ROT_DIM // 2
    inv_freq = jnp.exp(jnp.arange(half, dtype=F32) * (-2.0 * math.log(ROPE_THETA) / ROT_DIM))
    ang = pos.astype(F32)[:, None] * inv_freq[None, :]
    return jnp.cos(ang), jnp.sin(ang)


def _rope_tables_t(pos):
    cos, sin = _rope_angles(pos)
    return cos.T, sin.T


def _rope_tables(pos):
    half = ROT_DIM // 2
    cos, sin = _rope_angles(pos)
    n = pos.shape[0]
    ones = jnp.ones((n, HD_B - ROT_DIM), F32)
    zeros_h = jnp.zeros((n, half), F32)
    zeros_r = jnp.zeros((n, HD_B - ROT_DIM), F32)
    cos_h = jnp.concatenate([cos, cos, ones], axis=1)
    slo_h = jnp.concatenate([-sin, zeros_h, zeros_r], axis=1)
    shi_h = jnp.concatenate([zeros_h, sin, zeros_r], axis=1)
    tile = lambda t: jnp.tile(t, (1, HEADS_PER_LANE_TILE))
    return tile(cos_h), tile(slo_h), tile(shi_h)


def _lane_pick(x, lane):
    return jnp.sum(jnp.where(_iota(x.shape, 1) == lane, x, 0.0), axis=-1, keepdims=True)


def _delta_prompt_kernel(alog_ref, dtb_ref, q_ref, k_ref, v_ref, z_ref, lg_ref,
                         cwq_ref, cwk_ref, cwv_ref, nw_ref,
                         o_ref, s_ref,
                         u_s, w_s, qg_s, kd_s, qk_s, gc_s, *, seq):
    h = pl.program_id(1)
    n_chunks = seq // CHUNK
    c_shape = (CHUNK, CHUNK)
    row = _iota(c_shape, 0)
    col = _iota(c_shape, 1)
    causal = col <= row
    strict = col < row
    tril_f = jnp.where(causal, 1.0, 0.0)
    eye_f = jnp.where(col == row, 1.0, 0.0)
    ones_cl = jnp.ones((CHUNK, LANES), F32)
    lane0 = jnp.where(_iota((CHUNK, LANES), 1) == 0, 1.0, 0.0)
    neg_a = -jnp.exp(jnp.full((1, 1), alog_ref[h], F32))
    dtb = dtb_ref[h]

    def conv_silu(x_ref, cw_ref, c, r0):
        cur = x_ref[0, pl.ds(r0, CHUNK), :]
        p0 = pl.multiple_of(jnp.maximum(r0 - SUBLANES, 0), SUBLANES)
        prev = x_ref[0, pl.ds(p0, SUBLANES), :]
        prev = jnp.where(c > 0, prev, 0.0)
        ext = jnp.concatenate([prev, cur], axis=0)
        w = cw_ref[...]
        y = cur * w[CONV_W - 1:CONV_W]
        for i in range(CONV_W - 1):
            lo = SUBLANES - (CONV_W - 1) + i
            y = y + ext[lo:lo + CHUNK] * w[i:i + 1]
        return _silu(y)

    def l2n(x):
        return x * lax.rsqrt(jnp.sum(x * x, axis=-1, keepdims=True) + EPS)

    def prep(c, carry):
        r0 = pl.multiple_of(c * CHUNK, CHUNK)
        q = l2n(conv_silu(q_ref, cwq_ref, c, r0)) * (DK_A ** -0.5)
        k = l2n(conv_silu(k_ref, cwk_ref, c, r0))
        v = conv_silu(v_ref, cwv_ref, c, r0)
        lg = lg_ref[0, pl.ds(r0, CHUNK), :]
        beta = _sigmoid(_lane_pick(lg, h))
        g = neg_a * _softplus(_lane_pick(lg, H_A + h) + dtb)
        gc = _mm(tril_f, jnp.broadcast_to(g, (CHUNK, LANES)), HIGHEST)
        gc_row = _mm_nt(ones_cl, gc * lane0, HIGHEST)
        decay = jnp.exp(jnp.where(causal, gc[:, :CHUNK] - gc_row, NEG_INF))
        kb = k * beta
        a = jnp.where(strict, _mm_nt(_bf(kb), _bf(k)) * decay, 0.0)
        t_inv = eye_f - a
        a_pow = a
        for _ in range(int(math.log2(CHUNK)) - 1):
            a_pow = _mm(a_pow, a_pow, HIGHEST)
            t_inv = t_inv + _mm(t_inv, a_pow, HIGHEST)
        e_gc = jnp.exp(gc)
        u_s[pl.ds(r0, CHUNK), :] = _mm(_bf(t_inv), _bf(v * beta))
        w_s[pl.ds(r0, CHUNK), :] = _mm(_bf(t_inv), _bf(kb * e_gc))
        qk_s[pl.ds(r0, CHUNK), :] = jnp.where(causal, _mm_nt(_bf(q), _bf(k)) * decay, 0.0)
        qg_s[pl.ds(r0, CHUNK), :] = q * e_gc
        kd_s[pl.ds(r0, CHUNK), :] = k * jnp.exp(gc[CHUNK - 1:CHUNK, :] - gc)
        gc_s[pl.ds(r0, CHUNK), :] = gc
        return carry

    lax.fori_loop(0, n_chunks, prep, 0)

    nw = nw_ref[...]

    def scan(c, state):
        r0 = pl.multiple_of(c * CHUNK, CHUNK)
        sb = _bf(state)
        v_new = u_s[pl.ds(r0, CHUNK), :] - _mm(_bf(w_s[pl.ds(r0, CHUNK), :]), sb)
        vb = _bf(v_new)
        o = _mm(_bf(qg_s[pl.ds(r0, CHUNK), :]), sb) + _mm(_bf(qk_s[pl.ds(r0, CHUNK), :]), vb)
        g_last = gc_s[pl.ds(r0 + CHUNK - 1, 1), :]
        state = state * jnp.exp(g_last) + _mm_tn(_bf(kd_s[pl.ds(r0, CHUNK), :]), vb)
        gate = _silu(z_ref[0, pl.ds(r0, CHUNK), :])
        o_ref[0, pl.ds(r0, CHUNK), :] = _bf(_rms(o, nw) * gate)
        return state

    s_ref[0, 0] = lax.fori_loop(0, n_chunks, scan, jnp.zeros((DK_A, DV_A), F32))


def _delta_prompt(qkv, z, lg, conv_w, a_log, dt_bias, norm_w):
    bsz, seq, _ = qkv.shape
    hq = W_QK_A // LANES

    def seq_spec(off):
        return pl.BlockSpec((1, seq, LANES), lambda b, h, *_: (b, 0, off + h))

    def cw_spec(off):
        return pl.BlockSpec((CONV_W, LANES), lambda b, h, *_: (0, off + h))

    smem = pl.BlockSpec(memory_space=pltpu.SMEM)
    return pl.pallas_call(
        functools.partial(_delta_prompt_kernel, seq=seq),
        grid=(bsz, H_A),
        in_specs=[smem, smem, seq_spec(0), seq_spec(hq), seq_spec(2 * hq), seq_spec(0),
                  pl.BlockSpec((1, seq, LOGIT_PAD), lambda b, h: (b, 0, 0)),
                  cw_spec(0), cw_spec(hq), cw_spec(2 * hq),
                  pl.BlockSpec((1, DV_A), lambda b, h: (0, 0))],
        out_specs=[seq_spec(0),
                   pl.BlockSpec((1, 1, DK_A, DV_A), lambda b, h: (b, h, 0, 0))],
        out_shape=[jax.ShapeDtypeStruct((bsz, seq, W_V_A), BF16),
                   jax.ShapeDtypeStruct((bsz, H_A, DK_A, DV_A), F32)],
        scratch_shapes=[pltpu.VMEM((seq, LANES), F32) for _ in range(4)]
        + [pltpu.VMEM((seq, CHUNK), F32), pltpu.VMEM((seq, LANES), F32)],
        compiler_params=pltpu.CompilerParams(dimension_semantics=("arbitrary", "arbitrary"),
                                             vmem_limit_bytes=VMEM_LIMIT),
    )(a_log, dt_bias, qkv, qkv, qkv, z, lg, conv_w, conv_w, conv_w, norm_w.reshape(1, DV_A))


def _moba_prompt_kernel(q_ref, kt_ref, vt_ref, o_ref, kb_s, vb_s, km_s, *, n_blocks):
    qi = pl.program_id(2)
    bs = MOBA_BLOCK

    @pl.when(qi == 0)
    def _():
        lane = _iota((LANES, LANES), 1)
        km = jnp.zeros((LANES, LANES), F32)
        for j in range(n_blocks):
            kj = kt_ref[0, :, j * bs:(j + 1) * bs]
            kb_s[j] = _bf(kj)
            vb_s[j] = _bf(vt_ref[0, :, j * bs:(j + 1) * bs])
            km = jnp.where(lane == j, jnp.mean(kj, axis=-1, keepdims=True), km)
        km_s[...] = km

    q = q_ref[0]
    lane_q = _iota(q.shape, 1)
    km = km_s[...]
    blk_lane = _iota((bs, LANES), 1)
    k_own = kb_s[qi]
    v_own = vb_s[qi]
    own_mask = _iota((bs, bs), 1) <= _iota((bs, bs), 0)
    outs = []
    for hh in range(HEADS_PER_LANE_TILE):
        in_head = (lane_q >= hh * HD_B) & (lane_q < (hh + 1) * HD_B)
        qh = jnp.where(in_head, q, 0.0)
        sc = _mm(qh, km, HIGHEST)
        sc = jnp.where(blk_lane < qi, sc, NEG_INF)
        rank = _block_rank(sc, n_blocks)
        chosen = jnp.where(blk_lane < qi, jnp.where(rank < MOBA_TOPK, 1.0, 0.0), 0.0)

        qs = _bf(qh * (HD_B ** -0.5))
        s = jnp.where(own_mask, _mm(qs, k_own), NEG_INF)
        m = jnp.max(s, axis=-1, keepdims=True)
        p = jnp.exp(s - m)
        l = jnp.sum(p, axis=-1, keepdims=True)
        acc = _mm_nt(_bf(p), v_own)

        def past_block(j, carry):
            m, l, acc = carry
            pick = jnp.max(jnp.where(blk_lane == j, chosen, 0.0), axis=-1, keepdims=True)
            s = jnp.where(pick > 0.0, _mm(qs, kb_s[j]), NEG_INF)
            m_new = jnp.maximum(m, jnp.max(s, axis=-1, keepdims=True))
            alpha = jnp.exp(m - m_new)
            p = jnp.exp(s - m_new)
            l = alpha * l + jnp.sum(p, axis=-1, keepdims=True)
            acc = alpha * acc + _mm_nt(_bf(p), vb_s[j])
            return m_new, l, acc

        m, l, acc = lax.fori_loop(0, qi, past_block, (m, l, acc))
        outs.append(acc / l)
    o = outs[0]
    for hh in range(1, HEADS_PER_LANE_TILE):
        o = jnp.where(lane_q >= hh * HD_B, outs[hh], o)
    o_ref[0] = _bf(o)


def _moba_prompt(qr, kt, vt):
    bsz, seq, _ = qr.shape
    n_blocks = seq // MOBA_BLOCK
    assert seq % MOBA_BLOCK == 0 and n_blocks <= LANES
    q_spec = pl.BlockSpec((1, MOBA_BLOCK, LANES), lambda b, p, i: (b, i, p))
    kv_spec = pl.BlockSpec((1, LANES, seq), lambda b, p, i: (b, p, 0))
    return pl.pallas_call(
        functools.partial(_moba_prompt_kernel, n_blocks=n_blocks),
        grid=(bsz, N_PAIR, n_blocks),
        in_specs=[q_spec, kv_spec, kv_spec],
        out_specs=q_spec,
        out_shape=jax.ShapeDtypeStruct((bsz, seq, W_B), BF16),
        scratch_shapes=[pltpu.VMEM((n_blocks, LANES, MOBA_BLOCK), BF16),
                        pltpu.VMEM((n_blocks, LANES, MOBA_BLOCK), BF16),
                        pltpu.VMEM((LANES, LANES), F32)],
        compiler_params=pltpu.CompilerParams(
            dimension_semantics=("arbitrary", "arbitrary", "arbitrary"),
            vmem_limit_bytes=VMEM_LIMIT),
    )(qr, kt, vt)


def _merge_mlp_kernel(x_ref, oa_ref, ob_ref, ga_ref, gb_ref, wpa_ref, wpb_ref, wo_ref, n2_ref,
                      wup_ref, wdn_ref, nf_ref, y_ref, *, ff_step):
    y_a = _mm(oa_ref[...], wpa_ref[...])
    y_b = _mm(ob_ref[...], wpb_ref[...])
    mixed = _sigmoid(ga_ref[...]) * y_a + _sigmoid(gb_ref[...]) * y_b
    h = x_ref[...] + _mm(_bf(mixed), wo_ref[...])
    hn = _bf(_rms(h, n2_ref[...]))
    out = h
    d_ff = wup_ref.shape[1]
    for c in range(d_ff // ff_step):
        u = jnp.maximum(_mm(hn, wup_ref[:, c * ff_step:(c + 1) * ff_step]), 0.0)
        out = out + _mm(_bf(u * u), wdn_ref[c * ff_step:(c + 1) * ff_step, :])
    y_ref[...] = _rms(out, nf_ref[...])


def _merge_mlp(x2d, oa, ob, ga, gb, wpa, wpb, wo, n2, wup, wdn, nf, *, tm):
    rows, d_model = x2d.shape
    d_ff = wup.shape[1]

    def row_spec(n):
        return pl.BlockSpec((tm, n), lambda i: (i, 0))

    return pl.pallas_call(
        functools.partial(_merge_mlp_kernel, ff_step=min(d_ff, 1024)),
        grid=(rows // tm,),
        in_specs=[row_spec(d_model), row_spec(W_V_A), row_spec(W_B), row_spec(d_model),
                  row_spec(d_model),
                  _const_spec(wpa.shape), _const_spec(wpb.shape), _const_spec(wo.shape),
                  _const_spec((1, d_model)), _const_spec(wup.shape), _const_spec(wdn.shape),
                  _const_spec((1, d_model))],
        out_specs=row_spec(d_model),
        out_shape=jax.ShapeDtypeStruct((rows, d_model), F32),
        compiler_params=pltpu.CompilerParams(dimension_semantics=("arbitrary",),
                                             vmem_limit_bytes=VMEM_LIMIT),
    )(x2d, oa, ob, ga, gb, wpa, wpb, wo, n2.reshape(1, d_model), wup, wdn,
      nf.reshape(1, d_model))


def _delta_step_kernel(alog_ref, dtb_ref, x_ref, z_ref, lg_ref, sc_ref, cw_ref, sd_ref, nw_ref,
                       o_ref, snew_ref, cnew_ref):
    x = x_ref[0]
    hist = sc_ref[0]
    w = cw_ref[...]
    y = x * w[CONV_W - 1:CONV_W]
    for i in range(CONV_W - 1):
        y = y + hist[i:i + 1] * w[i:i + 1]
    y = _silu(y)
    cnew_ref[0, 0:CONV_W - 2, :] = hist[1:CONV_W - 1]
    cnew_ref[0, CONV_W - 2:CONV_W - 1, :] = x
    lg = lg_ref[0]
    zz = z_ref[0]
    nw = nw_ref[...]
    sq = (DK_A, DK_A)
    eye = _iota(sq, 0) == _iota(sq, 1)

    def l2n(t):
        return t * lax.rsqrt(jnp.sum(t * t, axis=-1, keepdims=True) + EPS)

    for h in range(H_A):
        q = l2n(y[:, h * DK_A:(h + 1) * DK_A]) * (DK_A ** -0.5)
        k = l2n(y[:, W_QK_A + h * DK_A:W_QK_A + (h + 1) * DK_A])
        v = y[:, 2 * W_QK_A + h * DV_A:2 * W_QK_A + (h + 1) * DV_A]
        beta = _sigmoid(lg[:, h:h + 1])
        neg_a = -jnp.exp(jnp.full((1, 1), alog_ref[h], F32))
        g = neg_a * _softplus(lg[:, H_A + h:H_A + h + 1] + dtb_ref[h])
        state = sd_ref[0, h] * jnp.exp(g)
        kv = _mm(jnp.broadcast_to(k, (SUBLANES, DK_A)), state, HIGHEST)[0:1]
        dv = (v - kv) * beta
        k_diag = jnp.where(eye, jnp.broadcast_to(k, sq), 0.0)
        state = state + _mm(k_diag, jnp.broadcast_to(dv, (DK_A, DV_A)), HIGHEST)
        snew_ref[0, h] = state
        o = _mm(jnp.broadcast_to(q, (SUBLANES, DK_A)), state, HIGHEST)[0:1]
        gate = _silu(zz[:, h * DV_A:(h + 1) * DV_A])
        o_ref[0, :, h * DV_A:(h + 1) * DV_A] = _bf(_rms(o, nw) * gate)


def _delta_step(qkv, z, lg, state_conv, conv_w, state_delta, a_log, dt_bias, norm_w):
    nb = qkv.shape[0]
    smem = pl.BlockSpec(memory_space=pltpu.SMEM)

    def per_b(shape):
        nd = len(shape)
        return pl.BlockSpec((1,) + shape, lambda b: (b,) + (0,) * nd)

    return pl.pallas_call(
        _delta_step_kernel,
        grid=(nb,),
        in_specs=[smem, smem, per_b((1, C_CONV)), per_b((1, W_V_A)), per_b((1, LOGIT_PAD)),
                  per_b((CONV_W - 1, C_CONV)), _const_spec((CONV_W, C_CONV)),
                  per_b((H_A, DK_A, DV_A)), _const_spec((1, DV_A))],
        out_specs=[per_b((1, W_V_A)), per_b((H_A, DK_A, DV_A)), per_b((CONV_W - 1, C_CONV))],
        out_shape=[jax.ShapeDtypeStruct((nb, 1, W_V_A), BF16),
                   jax.ShapeDtypeStruct((nb, H_A, DK_A, DV_A), F32),
                   jax.ShapeDtypeStruct((nb, CONV_W - 1, C_CONV), F32)],
        compiler_params=pltpu.CompilerParams(dimension_semantics=("arbitrary",),
                                             vmem_limit_bytes=VMEM_LIMIT),
    )(a_log, dt_bias, qkv.reshape(nb, 1, C_CONV), z.reshape(nb, 1, W_V_A),
      lg.reshape(nb, 1, LOGIT_PAD), state_conv, conv_w, state_delta, norm_w.reshape(1, DV_A))


KMEAN_SLOTS = 16


def _kmean_topk_kernel(pt_ref, qt_ref, cache_ref, idx_ref, buf, sems, km_s,
                       *, n_seq, n_full, ppb, page_size, kk):
    b = pl.program_id(0)
    pages_per_seq = n_full * ppb
    total = n_seq * pages_per_seq

    def copy(p, slot):
        return pltpu.make_async_copy(cache_ref.at[pt_ref[p]], buf.at[slot], sems.at[slot])

    @pl.when(b == 0)
    def _():
        for s in range(KMEAN_SLOTS):
            copy(s, s).start()

    base = b * pages_per_seq
    lane = _iota((HD_B, LANES), 1)
    km_s[...] = jnp.zeros_like(km_s)

    def block(n, carry):
        p0 = base + n * ppb
        slot0 = p0 % KMEAN_SLOTS
        for i in range(ppb):
            copy(p0 + i, slot0 + i).wait()
        for h in range(H_B):
            x = buf[slot0, h]
            for i in range(1, ppb):
                x = x + buf[slot0 + i, h]
            mean = jnp.sum(x, axis=-1, keepdims=True) * (1.0 / (ppb * page_size))
            rows = slice(h * HD_B, (h + 1) * HD_B)
            km_s[rows, :] = jnp.where(lane == n, mean, km_s[rows, :])
        for i in range(ppb):
            @pl.when(p0 + i + KMEAN_SLOTS < total)
            def _():
                copy(p0 + i + KMEAN_SLOTS, slot0 + i).start()
        return carry

    lax.fori_loop(0, n_full, block, 0)

    qt = qt_ref[...]
    q_col = jnp.sum(jnp.where(_iota(qt.shape, 1) == b, qt, 0.0), axis=-1, keepdims=True)
    prod = km_s[...] * q_col
    sc = jnp.concatenate(
        [jnp.sum(prod[h * HD_B:(h + 1) * HD_B], axis=0, keepdims=True) for h in range(H_B)],
        axis=0)
    blk = _iota(sc.shape, 1)
    sc = jnp.where(blk < n_full, sc, NEG_INF)
    rank = _block_rank(sc, n_full)
    blk_f = blk.astype(F32)
    out = jnp.zeros(sc.shape, F32)
    for r in range(kk):
        pick = jnp.sum(jnp.where(rank == float(r), blk_f, 0.0), axis=-1, keepdims=True)
        out = jnp.where(blk == r, pick, out)
    idx_ref[0] = out.astype(jnp.int32)


def _kmean_topk(page_ids, q_t, cache_t, *, n_full, ppb, kk):
    _, n_seq = q_t.shape
    _, _, _, page_size = cache_t.shape
    assert n_full <= LANES and KMEAN_SLOTS % ppb == 0 and n_seq * n_full * ppb >= KMEAN_SLOTS
    return pl.pallas_call(
        functools.partial(_kmean_topk_kernel, n_seq=n_seq, n_full=n_full, ppb=ppb,
                          page_size=page_size, kk=kk),
        grid_spec=pltpu.PrefetchScalarGridSpec(
            num_scalar_prefetch=1,
            grid=(n_seq,),
            in_specs=[_const_spec(q_t.shape), pl.BlockSpec(memory_space=pl.ANY)],
            out_specs=pl.BlockSpec((1, H_B, LANES), lambda b, *_: (b, 0, 0)),
            scratch_shapes=[pltpu.VMEM((KMEAN_SLOTS, H_B, HD_B, page_size), F32),
                            pltpu.SemaphoreType.DMA((KMEAN_SLOTS,)),
                            pltpu.VMEM((W_B, LANES), F32)]),
        out_shape=jax.ShapeDtypeStruct((n_seq, H_B, LANES), jnp.int32),
        compiler_params=pltpu.CompilerParams(dimension_semantics=("arbitrary",),
                                             vmem_limit_bytes=VMEM_LIMIT),
    )(page_ids, q_t, cache_t)


def _moba_sample_kernel(top_ref, pt_ref, q_ref, kn_ref, vn_ref, ck_ref, cv_ref, o_ref,
                        kbuf, vbuf, sems, *, kk, ppb, n_pages, page_size):
    b = pl.program_id(0)
    pages_per_head = kk * ppb

    def copies(h, i):
        blk = top_ref[(b * H_B + h) * kk + i // ppb]
        page = pt_ref[b * n_pages + blk * ppb + i % ppb]
        keys = pl.ds(i * page_size, page_size)
        return (pltpu.make_async_copy(ck_ref.at[page, h], kbuf.at[h, :, keys], sems.at[0, h]),
                pltpu.make_async_copy(cv_ref.at[page, h], vbuf.at[h, :, keys], sems.at[1, h]))

    for h in range(H_B):
        for i in range(pages_per_head):
            for c in copies(h, i):
                c.start()

    q = q_ref[0]
    kn = kn_ref[0]
    vn = vn_ref[0]
    for h in range(H_B):
        for i in range(pages_per_head):
            for c in copies(h, i):
                c.wait()
        sl = slice(h * HD_B, (h + 1) * HD_B)
        qh = q[:, sl] * (HD_B ** -0.5)
        s = _mm(_bf(jnp.broadcast_to(qh, (SUBLANES, HD_B))), _bf(kbuf[h]))[0:1]
        s_own = jnp.sum(qh * kn[:, sl], axis=-1, keepdims=True)
        m = jnp.maximum(jnp.max(s, axis=-1, keepdims=True), s_own)
        p = jnp.exp(s - m)
        p_own = jnp.exp(s_own - m)
        l = jnp.sum(p, axis=-1, keepdims=True) + p_own
        pv = _mm_nt(_bf(jnp.broadcast_to(p, (SUBLANES, p.shape[1]))), _bf(vbuf[h]))[0:1]
        o_ref[0, :, sl] = _bf((pv + p_own * vn[:, sl]) / l)


def _moba_sample(top_flat, pt_flat, q3, kn3, vn3, cache_kt, cache_vt, *, kk, ppb):
    nb = q3.shape[0]
    _, _, _, page_size = cache_kt.shape
    keys = kk * ppb * page_size
    row = pl.BlockSpec((1, 1, W_B), lambda b, *_: (b, 0, 0))
    any_spec = pl.BlockSpec(memory_space=pl.ANY)
    return pl.pallas_call(
        functools.partial(_moba_sample_kernel, kk=kk, ppb=ppb,
                          n_pages=pt_flat.shape[0] // nb, page_size=page_size),
        grid_spec=pltpu.PrefetchScalarGridSpec(
            num_scalar_prefetch=2,
            grid=(nb,),
            in_specs=[row, row, row, any_spec, any_spec],
            out_specs=row,
            scratch_shapes=[pltpu.VMEM((H_B, HD_B, keys), F32),
                            pltpu.VMEM((H_B, HD_B, keys), F32),
                            pltpu.SemaphoreType.DMA((2, H_B))]),
        out_shape=jax.ShapeDtypeStruct((nb, 1, W_B), BF16),
        compiler_params=pltpu.CompilerParams(dimension_semantics=("arbitrary",),
                                             vmem_limit_bytes=VMEM_LIMIT),
    )(top_flat, pt_flat, q3, kn3, vn3, cache_kt, cache_vt)


def _rearranged_w_in(w_in, d_model):
    o_z = C_CONV
    o_lg = o_z + W_V_A
    o_b = o_lg + 2 * H_A
    o_g = o_b + 3 * W_B
    pad = jnp.zeros((d_model, LOGIT_PAD - 2 * H_A), w_in.dtype)
    return _bf(jnp.concatenate(
        [w_in[:, :o_lg], w_in[:, o_b:o_g + 2 * d_model], w_in[:, o_lg:o_b], pad], axis=1))


def kernel(x_prompt, x_sample, cache_k, cache_v, page_table, state_delta, state_conv, norm1_w,
           w_in, conv_w, a_log, dt_bias, delta_norm_w, w_proj_a, w_proj_b, w_out, norm2_w, w_up,
           w_down, norm_f_w):
    depth = w_in.shape[0]
    assert depth == 1, "single-layer trunk"
    bp, sp, d_model = x_prompt.shape
    bs, ss, _ = x_sample.shape
    assert ss == 1, "one new token per sample sequence"
    _, n_pool, page_size, _, _ = cache_k.shape
    n_pages = page_table.shape[1]
    past_len = n_pages * page_size
    ppb = MOBA_BLOCK // page_size
    n_full = past_len // MOBA_BLOCK
    assert n_full * MOBA_BLOCK == past_len, "the sample token starts a fresh MoBA block"
    kk = min(MOBA_TOPK, n_full)
    assert kk >= 1

    w_r = _rearranged_w_in(w_in[0], d_model)
    o_kb = C_CONV + W_V_A + 2 * H_A + W_B
    wkv_t = _bf(w_in[0][:, o_kb:o_kb + 2 * W_B].T)
    wpa, wpb, wo = _bf(w_proj_a[0]), _bf(w_proj_b[0]), _bf(w_out[0])
    wup, wdn = _bf(w_up[0]), _bf(w_down[0])

    tm = 256
    pos_p = jnp.arange(sp, dtype=jnp.int32)
    (qkv, z, qr, kt, vt, ga, gb, lg, conv_p) = _in_proj(
        x_prompt.reshape(bp * sp, d_model), norm1_w[0], w_r, _rope_tables(pos_p),
        tm=tm, rows_per_seq=sp, wkv_t=wkv_t, tabs_t=_rope_tables_t(pos_p))
    o_a, s_p = _delta_prompt(qkv.reshape(bp, sp, C_CONV), z.reshape(bp, sp, W_V_A),
                             lg.reshape(bp, sp, LOGIT_PAD), conv_w[0], a_log[0], dt_bias[0],
                             delta_norm_w[0])
    o_b = _moba_prompt(qr.reshape(bp, sp, W_B), kt, vt)
    y_p = _merge_mlp(x_prompt.reshape(bp * sp, d_model), o_a.reshape(bp * sp, W_V_A),
                     o_b.reshape(bp * sp, W_B), ga, gb, wpa, wpb, wo, norm2_w[0], wup, wdn,
                     norm_f_w, tm=tm)

    pos_s = jnp.full((bs,), past_len, dtype=jnp.int32)
    (qkv_s, z_s, qr_s, kr_s, vr_s, ga_s, gb_s, lg_s) = _in_proj(
        x_sample.reshape(bs, d_model), norm1_w[0], w_r, _rope_tables(pos_s),
        tm=bs, rows_per_seq=bs)
    o_a_s, s_s, conv_s = _delta_step(qkv_s, z_s, lg_s, state_conv[0], conv_w[0], state_delta[0],
                                     a_log[0], dt_bias[0], delta_norm_w[0])
    cache_kt = cache_k.reshape(n_pool, page_size, H_B, HD_B).transpose(0, 2, 3, 1)
    cache_vt = cache_v.reshape(n_pool, page_size, H_B, HD_B).transpose(0, 2, 3, 1)
    top = _kmean_topk(page_table[:, :n_full * ppb].reshape(-1), qr_s.T, cache_kt,
                      n_full=n_full, ppb=ppb, kk=kk)
    top_flat = top[:, :, :kk].reshape(-1)
    o_b_s = _moba_sample(top_flat, page_table.reshape(-1), qr_s.reshape(bs, 1, W_B),
                         kr_s.reshape(bs, 1, W_B), vr_s.reshape(bs, 1, W_B), cache_kt, cache_vt,
                         kk=kk, ppb=ppb)
    y_s = _merge_mlp(x_sample.reshape(bs, d_model), o_a_s.reshape(bs, W_V_A),
                     o_b_s.reshape(bs, W_B), ga_s, gb_s, wpa, wpb, wo, norm2_w[0], wup, wdn,
                     norm_f_w, tm=bs)

    def kv_out(t):
        return t.reshape(1, bp, H_B, HD_B, sp).transpose(0, 1, 4, 2, 3)

    return (y_p.reshape(bp, sp, d_model), y_s.reshape(bs, ss, d_model),
            kv_out(kt), kv_out(vt),
            s_p.reshape(1, bp, H_A, DK_A, DV_A), conv_p.reshape(1, bp, CONV_W - 1, C_CONV),
            kr_s.reshape(1, bs, ss, H_B, HD_B), vr_s.reshape(1, bs, ss, H_B, HD_B),
            s_s.reshape(1, bs, H_A, DK_A, DV_A), conv_s.reshape(1, bs, CONV_W - 1, C_CONV))
```

```python
import functools
import math

import jax
import jax.numpy as jnp
from jax import lax
from jax.experimental import pallas as pl
from jax.experimental.pallas import tpu as pltpu

F32 = jnp.float32
BF16 = jnp.bfloat16
HIGHEST = lax.Precision.HIGHEST

H_A = 4
DK_A = 128
DV_A = 128
W_QK_A = H_A * DK_A
W_V_A = H_A * DV_A
C_CONV = 2 * W_QK_A + W_V_A
CONV_W = 4
H_B = 8
HD_B = 64
W_B = H_B * HD_B
MOBA_BLOCK = 256
MOBA_TOPK = 3
ROPE_THETA = 500000.0
ROT_DIM = HD_B // 4
EPS = 1e-6

LANES = 128
SUBLANES = 8
HEADS_PER_LANE_TILE = LANES // HD_B
N_PAIR = H_B // HEADS_PER_LANE_TILE
LOGIT_PAD = LANES
DELTA_CHUNK = LANES
DELTA_TILE = 4 * DELTA_CHUNK
VMEM_LIMIT = 56 * 1024 * 1024
NEG_INF = float("-inf")


def _dot(a, b, dims, precision=None):
    return lax.dot_general(a, b, (dims, ((), ())), precision=precision,
                           preferred_element_type=F32)


def _mm(a, b, precision=None):
    return _dot(a, b, ((1,), (0,)), precision)


def _mm_nt(a, b, precision=None):
    return _dot(a, b, ((1,), (1,)), precision)


def _bf(x):
    return x.astype(BF16)


def _sigmoid(x):
    return 1.0 / (1.0 + jnp.exp(-x))


def _silu(x):
    return x * _sigmoid(x)


def _softplus(x):
    return jnp.maximum(x, 0.0) + jnp.log(1.0 + jnp.exp(-jnp.abs(x)))


def _rms(x, w):
    return x * lax.rsqrt(jnp.mean(x * x, axis=-1, keepdims=True) + EPS) * w


def _iota(shape, dim):
    return lax.broadcasted_iota(jnp.int32, shape, dim)


def _const_spec(shape):
    nd = len(shape)
    return pl.BlockSpec(shape, lambda *_: (0,) * nd, pipeline_mode=pl.Buffered(1))


def _block_rank(sc, n_blocks, axis):
    blk = _iota(sc.shape, axis)
    rank = jnp.zeros(sc.shape, F32)
    for j in range(n_blocks):
        cj = sc[j:j + 1, :] if axis == 0 else sc[:, j:j + 1]
        first_on_tie = jnp.where(blk > j, 1.0, 0.0)
        rank = rank + jnp.where(cj > sc, 1.0, jnp.where(cj == sc, first_on_tie, 0.0))
    return rank


def _split_bf16(x):
    hi = _bf(x)
    return hi, _bf(x - hi.astype(F32))


def _mm_3pass(a, b):
    a_hi, a_lo = _split_bf16(a)
    b_hi, b_lo = _split_bf16(b)
    return _mm(jnp.concatenate([a_hi, a_lo, a_hi], axis=1),
               jnp.concatenate([b_hi, b_hi, b_lo], axis=0))


def _in_proj_kernel(x_ref, nw_ref, w_ref, cos_ref, slo_ref, shi_ref, *rest,
                    d_model, tiles_per_seq, kv_transposed):
    if kv_transposed:
        (wqkv_t_ref, cos_t_ref, sin_t_ref,
         qkv_ref, z_ref, qb_ref, kb_ref, vb_ref, ga_ref, gb_ref, lg_ref, conv_ref) = rest
    else:
        qkv_ref, z_ref, qb_ref, kb_ref, vb_ref, ga_ref, gb_ref, lg_ref = rest
    xb = _bf(_rms(x_ref[...], nw_ref[...]))
    tm = xb.shape[0]
    half = ROT_DIM // 2

    def cols(c0, n):
        return _mm(xb, w_ref[:, c0:c0 + n])

    def rope_store(dst_ref, c0):
        cos, slo, shi = cos_ref[...], slo_ref[...], shi_ref[...]
        for c in range(W_B // LANES):
            y = cols(c0 + c * LANES, LANES)
            up = pltpu.roll(y, LANES - half, 1)
            dn = pltpu.roll(y, half, 1)
            dst_ref[:, c * LANES:(c + 1) * LANES] = y * cos + up * slo + dn * shi

    def rope_store_t(dst_ref, r0):
        cos_t, sin_t = cos_t_ref[...], sin_t_ref[...]
        y_t = _mm_nt(wqkv_t_ref[r0:r0 + W_B, :], xb)
        for h in range(H_B):
            r = h * HD_B
            x1, x2 = y_t[r:r + half], y_t[r + half:r + ROT_DIM]
            dst_ref[0, r:r + half, :] = x1 * cos_t - x2 * sin_t
            dst_ref[0, r + half:r + ROT_DIM, :] = x2 * cos_t + x1 * sin_t
            dst_ref[0, r + ROT_DIM:r + HD_B, :] = y_t[r + ROT_DIM:r + HD_B]

    step = 512
    c0 = 0
    for c in range(C_CONV // step):
        qkv_ref[:, c * step:(c + 1) * step] = cols(c0, step)
        c0 += step
    z_ref[...] = cols(c0, W_V_A)
    c0 += W_V_A
    if kv_transposed:
        rope_store_t(qb_ref, 0)
        rope_store_t(kb_ref, W_B)
        vb_ref[0] = _mm_nt(wqkv_t_ref[2 * W_B:3 * W_B, :], xb)
    else:
        rope_store(qb_ref, c0)
        rope_store(kb_ref, c0 + W_B)
        vb_ref[...] = cols(c0 + 2 * W_B, W_B)
    c0 += 3 * W_B
    for g_ref in (ga_ref, gb_ref):
        for c in range(d_model // step):
            g_ref[:, c * step:(c + 1) * step] = cols(c0, step)
            c0 += step
    lg_ref[...] = cols(c0, LOGIT_PAD)

    if kv_transposed:
        @pl.when(pl.program_id(0) % tiles_per_seq == tiles_per_seq - 1)
        def _():
            conv_ref[0] = qkv_ref[tm - (CONV_W - 1):tm, :]


def _in_proj(x2d, norm_w, w_r, tabs, *, tm, rows_per_seq, wqkv_t=None, tabs_t=None):
    rows, d_model = x2d.shape
    n_tiles = rows // tm
    tiles_per_seq = rows_per_seq // tm
    n_seq = rows // rows_per_seq
    kv_transposed = wqkv_t is not None

    def row_spec(n):
        return pl.BlockSpec((tm, n), lambda i: (i, 0))

    def sds(*shape):
        return jax.ShapeDtypeStruct(shape, F32)

    kv_t_spec = pl.BlockSpec((1, W_B, tm), lambda i: (i // tiles_per_seq, 0, i % tiles_per_seq))
    kv_shape = sds(n_seq, W_B, rows_per_seq) if kv_transposed else sds(rows, W_B)
    kv_spec = kv_t_spec if kv_transposed else row_spec(W_B)
    tab_spec = pl.BlockSpec((tm, LANES), lambda i: (i % tiles_per_seq, 0))
    in_specs = [row_spec(d_model), _const_spec((1, d_model)), _const_spec(w_r.shape),
                tab_spec, tab_spec, tab_spec]
    args = [x2d, norm_w.reshape(1, d_model), w_r, *tabs]
    out_shape = [sds(rows, C_CONV), sds(rows, W_V_A), kv_shape, kv_shape, kv_shape,
                 sds(rows, d_model), sds(rows, d_model), sds(rows, LOGIT_PAD)]
    out_specs = [row_spec(C_CONV), row_spec(W_V_A), kv_spec, kv_spec, kv_spec,
                 row_spec(d_model), row_spec(d_model), row_spec(LOGIT_PAD)]
    if kv_transposed:
        tab_t_spec = pl.BlockSpec((ROT_DIM // 2, tm), lambda i: (0, i % tiles_per_seq))
        in_specs += [_const_spec(wqkv_t.shape), tab_t_spec, tab_t_spec]
        args += [wqkv_t, *tabs_t]
        out_shape.append(sds(n_seq, CONV_W - 1, C_CONV))
        out_specs.append(pl.BlockSpec((1, CONV_W - 1, C_CONV),
                                      lambda i: (i // tiles_per_seq, 0, 0)))
    return pl.pallas_call(
        functools.partial(_in_proj_kernel, d_model=d_model, tiles_per_seq=tiles_per_seq,
                          kv_transposed=kv_transposed),
        grid=(n_tiles,),
        in_specs=in_specs,
        out_specs=out_specs,
        out_shape=out_shape,
        compiler_params=pltpu.CompilerParams(dimension_semantics=("arbitrary",),
                                             vmem_limit_bytes=VMEM_LIMIT),
    )(*args)


def _rope_angles(pos):
    half = ROT_DIM // 2
    inv_freq = jnp.exp(jnp.arange(half, dtype=F32) * (-2.0 * math.log(ROPE_THETA) / ROT_DIM))
    ang = pos.astype(F32)[:, None] * inv_freq[None, :]
    return jnp.cos(ang), jnp.sin(ang)


def _rope_tables_t(pos):
    cos, sin = _rope_angles(pos)
    return cos.T, sin.T


def _rope_tables(pos):
    half = ROT_DIM // 2
    cos, sin = _rope_angles(pos)
    n = pos.shape[0]
    ones = jnp.ones((n, HD_B - ROT_DIM), F32)
    zeros_h = jnp.zeros((n, half), F32)
    zeros_r = jnp.zeros((n, HD_B - ROT_DIM), F32)
    cos_h = jnp.concatenate([cos, cos, ones], axis=1)
    slo_h = jnp.concatenate([-sin, zeros_h, zeros_r], axis=1)
    shi_h = jnp.concatenate([zeros_h, sin, zeros_r], axis=1)
    tile = lambda t: jnp.tile(t, (1, HEADS_PER_LANE_TILE))
    return tile(cos_h), tile(slo_h), tile(shi_h)


def _delta_prompt_kernel(alog_ref, dtb_ref, x_ref, z_ref, lg_ref, cw_ref, nw_ref,
                         o_ref, s_ref,
                         hist_s, state_s, u_s, w_s, qg_s, qk_s, kdt_s, gl_s):
    t = pl.program_id(1)
    cs = DELTA_CHUNK
    n_chunks = x_ref.shape[1] // cs
    heads = range(H_A)

    @pl.when(t == 0)
    def _():
        hist_s[...] = jnp.zeros_like(hist_s)
        state_s[...] = jnp.zeros_like(state_s)

    sq = (cs, cs)
    row = _iota(sq, 0)
    col = _iota(sq, 1)
    causal = col <= row
    strict = col < row
    tril_b = _bf(jnp.where(causal, 1.0, 0.0))
    eye_f = jnp.where(col == row, 1.0, 0.0)
    neg_a = [-jnp.exp(jnp.full((1, 1), alog_ref[h], F32)) for h in heads]
    dtb = [dtb_ref[h] for h in heads]
    cw = cw_ref[...]

    def conv_silu(c, r0, c0):
        lanes = slice(c0, c0 + LANES)
        cur = x_ref[0, pl.ds(r0, cs), lanes]
        p0 = pl.multiple_of(jnp.maximum(r0 - SUBLANES, 0), SUBLANES)
        prev = jnp.where(c > 0, x_ref[0, pl.ds(p0, SUBLANES), lanes], hist_s[:, lanes])
        ext = jnp.concatenate([prev, cur], axis=0)
        w = cw[:, lanes]
        y = cur * w[CONV_W - 1:CONV_W]
        for i in range(CONV_W - 1):
            lo = SUBLANES - (CONV_W - 1) + i
            y = y + ext[lo:lo + cs] * w[i:i + 1]
        return _silu(y)

    def l2n(x):
        return x * lax.rsqrt(jnp.sum(x * x, axis=-1, keepdims=True) + EPS)

    def prep(c, carry):
        r0 = pl.multiple_of(c * cs, cs)
        rows = pl.ds(r0, cs)
        q = [l2n(conv_silu(c, r0, h * DK_A)) * (DK_A ** -0.5) for h in heads]
        k = [l2n(conv_silu(c, r0, W_QK_A + h * DK_A)) for h in heads]
        v = [conv_silu(c, r0, 2 * W_QK_A + h * DV_A) for h in heads]
        lg = lg_ref[0, rows, :]
        beta = [_sigmoid(lg[:, h:h + 1]) for h in heads]
        g = [neg_a[h] * _softplus(lg[:, H_A + h:H_A + h + 1] + dtb[h]) for h in heads]
        pieces = []
        for h in heads:
            rem = jnp.broadcast_to(g[h], (cs, LANES))
            for _ in range(3):
                piece = _bf(rem)
                pieces.append(piece)
                rem = rem - piece.astype(F32)
        csum = _mm(tril_b, jnp.concatenate(pieces, axis=1))
        gc = [csum[:, 3 * h * LANES:(3 * h + 1) * LANES]
              + csum[:, (3 * h + 1) * LANES:(3 * h + 2) * LANES]
              + csum[:, (3 * h + 2) * LANES:(3 * h + 3) * LANES] for h in heads]
        decay = [jnp.exp(jnp.where(causal, gc[h] - gc[h].T, NEG_INF)) for h in heads]
        kb = [k[h] * beta[h] for h in heads]
        a = [jnp.where(strict, _mm_nt(_bf(kb[h]), _bf(k[h])) * decay[h], 0.0) for h in heads]
        t_inv = [eye_f - a[h] for h in heads]
        a_pow = [_mm_3pass(a[h], a[h]) for h in heads]
        n_levels = int(math.log2(cs)) - 1
        for lvl in range(n_levels):
            if lvl < n_levels - 1:
                prod = [_mm_3pass(a_pow[h], jnp.concatenate([a_pow[h], t_inv[h]], axis=1))
                        for h in heads]
                a_pow = [prod[h][:, :cs] for h in heads]
                t_inv = [t_inv[h] + prod[h][:, cs:] for h in heads]
            else:
                t_inv = [t_inv[h] + _mm_3pass(a_pow[h], t_inv[h]) for h in heads]
        e_gc = [jnp.exp(gc[h]) for h in heads]
        uw = [_mm(_bf(t_inv[h]),
                  jnp.concatenate([_bf(v[h] * beta[h]), _bf(kb[h] * e_gc[h])], axis=1))
              for h in heads]
        qk = [jnp.where(causal, _mm_nt(_bf(q[h]), _bf(k[h])) * decay[h], 0.0) for h in heads]
        for h in heads:
            g_last = gc[h][cs - 1:cs, :]
            u_s[h, rows, :] = uw[h][:, :DV_A]
            w_s[h, rows, :] = _bf(uw[h][:, DV_A:])
            qk_s[h, rows, :] = _bf(qk[h])
            qg_s[h, rows, :] = _bf(q[h] * e_gc[h])
            kdt_s[h, c] = _bf((k[h] * jnp.exp(g_last - gc[h])).T)
            gl_s[h, c] = jnp.broadcast_to(jnp.exp(g_last), (SUBLANES, LANES))
        return carry

    lax.fori_loop(0, n_chunks, prep, 0)
    hist_s[...] = x_ref[0, n_chunks * cs - SUBLANES:n_chunks * cs, :]

    nw = nw_ref[...]

    def scan(c, carry):
        rows = pl.ds(pl.multiple_of(c * cs, cs), cs)
        state = [state_s[h] for h in heads]
        sb = [_bf(state[h]) for h in heads]
        ws = [_mm(jnp.concatenate([w_s[h, rows, :], qg_s[h, rows, :]], axis=0), sb[h])
              for h in heads]
        vb = [_bf(u_s[h, rows, :] - ws[h][:cs]) for h in heads]
        o = [ws[h][cs:] + _mm(qk_s[h, rows, :], vb[h]) for h in heads]
        new = [state[h] * gl_s[h, c][0:1] + _mm(kdt_s[h, c], vb[h]) for h in heads]
        for h in heads:
            state_s[h] = new[h]
            lanes = slice(h * DV_A, (h + 1) * DV_A)
            o_ref[0, rows, lanes] = _bf(_rms(o[h], nw) * _silu(z_ref[0, rows, lanes]))
        return carry

    lax.fori_loop(0, n_chunks, scan, 0)

    @pl.when(t == pl.num_programs(1) - 1)
    def _():
        s_ref[0] = state_s[...]


def _delta_prompt(qkv, z, lg, conv_w, a_log, dt_bias, norm_w):
    bsz, seq, _ = qkv.shape
    ts = min(DELTA_TILE, seq)
    n_chunks = ts // DELTA_CHUNK
    assert seq % ts == 0 and ts % DELTA_CHUNK == 0 and DK_A == LANES and DV_A == LANES

    def tile_spec(n):
        return pl.BlockSpec((1, ts, n), lambda b, t: (b, t, 0))

    def per_head(shape, dtype):
        return pltpu.VMEM((H_A,) + shape, dtype)

    smem = pl.BlockSpec(memory_space=pltpu.SMEM)
    return pl.pallas_call(
        _delta_prompt_kernel,
        grid=(bsz, seq // ts),
        in_specs=[smem, smem, tile_spec(C_CONV), tile_spec(W_V_A), tile_spec(LOGIT_PAD),
                  _const_spec((CONV_W, C_CONV)), _const_spec((1, DV_A))],
        out_specs=[tile_spec(W_V_A),
                   pl.BlockSpec((1, H_A, DK_A, DV_A), lambda b, t: (b, 0, 0, 0))],
        out_shape=[jax.ShapeDtypeStruct((bsz, seq, W_V_A), BF16),
                   jax.ShapeDtypeStruct((bsz, H_A, DK_A, DV_A), F32)],
        scratch_shapes=[pltpu.VMEM((SUBLANES, C_CONV), F32),
                        per_head((DK_A, DV_A), F32),
                        per_head((ts, DV_A), F32),
                        per_head((ts, DK_A), BF16),
                        per_head((ts, DK_A), BF16),
                        per_head((ts, DELTA_CHUNK), BF16),
                        per_head((n_chunks, DK_A, DELTA_CHUNK), BF16),
                        per_head((n_chunks, SUBLANES, LANES), F32)],
        compiler_params=pltpu.CompilerParams(dimension_semantics=("arbitrary", "arbitrary"),
                                             vmem_limit_bytes=VMEM_LIMIT),
    )(a_log, dt_bias, qkv, z, lg, conv_w, norm_w.reshape(1, DV_A))


def _moba_prompt_kernel(qt_ref, kt_ref, vt_ref, o_ref, kn_s, vt_s, km_s, ch_s, *, n_blocks):
    qi = pl.program_id(2)
    bs = MOBA_BLOCK
    heads = range(HEADS_PER_LANE_TILE)

    @pl.when(qi == 0)
    def _():
        km_s[...] = jnp.zeros_like(km_s)
        for j in range(n_blocks):
            kj = kt_ref[0, :, j * bs:(j + 1) * bs].T
            kn_s[j] = _bf(kj)
            km_s[j:j + 1, :] = jnp.mean(kj, axis=0, keepdims=True)
            vt_s[j] = _bf(vt_ref[0, :, j * bs:(j + 1) * bs])

    qt = qt_ref[0]
    d_row = _iota(qt.shape, 0)
    km = km_s[...]
    blk = _iota((km.shape[0], bs), 0)
    key_le_query = _iota((bs, bs), 0) <= _iota((bs, bs), 1)
    d_rows = [slice(hh * HD_B, (hh + 1) * HD_B) for hh in heads]
    qh = [jnp.where((d_row >= hh * HD_B) & (d_row < (hh + 1) * HD_B), qt, 0.0) for hh in heads]
    sc = [jnp.where(blk < qi, _mm(km, qh[hh], HIGHEST), NEG_INF) for hh in heads]
    rank = [_block_rank(sc[hh], n_blocks, 0) for hh in heads]
    for hh in heads:
        ch_s[hh] = jnp.where(blk < qi, jnp.where(rank[hh] < MOBA_TOPK, 1.0, 0.0), 0.0)

    qs = [_bf(qh[hh] * (HD_B ** -0.5)) for hh in heads]
    s = [jnp.where(key_le_query, _mm(kn_s[qi], qs[hh]), NEG_INF) for hh in heads]
    m = [jnp.max(s[hh], axis=0, keepdims=True) for hh in heads]
    p = [jnp.exp(s[hh] - m[hh]) for hh in heads]
    l = [jnp.sum(p[hh], axis=0, keepdims=True) for hh in heads]
    acc = [_mm(vt_s[qi, d_rows[hh], :], _bf(p[hh])) for hh in heads]

    def two_past_blocks(jj, carry):
        m, l, acc = carry
        js = (2 * jj, 2 * jj + 1)
        kn = [kn_s[j] for j in js]
        s = [[jnp.where(ch_s[hh, pl.ds(j, 1), :] > 0.0, _mm(kn[i], qs[hh]), NEG_INF)
              for i, j in enumerate(js)] for hh in heads]
        m_new = [jnp.maximum(m[hh], jnp.maximum(jnp.max(s[hh][0], axis=0, keepdims=True),
                                                jnp.max(s[hh][1], axis=0, keepdims=True)))
                 for hh in heads]
        alpha = [jnp.exp(m[hh] - m_new[hh]) for hh in heads]
        p = [[jnp.exp(s[hh][i] - m_new[hh]) for i in range(2)] for hh in heads]
        l = [alpha[hh] * l[hh] + jnp.sum(p[hh][0], axis=0, keepdims=True)
             + jnp.sum(p[hh][1], axis=0, keepdims=True) for hh in heads]
        acc = [alpha[hh] * acc[hh] + _mm(vt_s[js[0], d_rows[hh], :], _bf(p[hh][0]))
               + _mm(vt_s[js[1], d_rows[hh], :], _bf(p[hh][1])) for hh in heads]
        return m_new, l, acc

    m, l, acc = lax.fori_loop(0, (qi + 1) // 2, two_past_blocks, (m, l, acc))
    o_t = jnp.concatenate([acc[hh] / l[hh] for hh in heads], axis=0)
    o_ref[0] = _bf(o_t.T)


def _moba_prompt(qt, kt, vt):
    bsz, _, seq = qt.shape
    n_blocks = seq // MOBA_BLOCK
    assert seq % MOBA_BLOCK == 0
    blocks_pad = -(-n_blocks // SUBLANES) * SUBLANES
    q_spec = pl.BlockSpec((1, LANES, MOBA_BLOCK), lambda b, p, i: (b, p, i))
    kv_spec = pl.BlockSpec((1, LANES, seq), lambda b, p, i: (b, p, 0))
    return pl.pallas_call(
        functools.partial(_moba_prompt_kernel, n_blocks=n_blocks),
        grid=(bsz, N_PAIR, n_blocks),
        in_specs=[q_spec, kv_spec, kv_spec],
        out_specs=pl.BlockSpec((1, MOBA_BLOCK, LANES), lambda b, p, i: (b, i, p)),
        out_shape=jax.ShapeDtypeStruct((bsz, seq, W_B), BF16),
        scratch_shapes=[pltpu.VMEM((n_blocks, MOBA_BLOCK, LANES), BF16),
                        pltpu.VMEM((n_blocks, LANES, MOBA_BLOCK), BF16),
                        pltpu.VMEM((blocks_pad, LANES), F32),
                        pltpu.VMEM((HEADS_PER_LANE_TILE, blocks_pad, MOBA_BLOCK), F32)],
        compiler_params=pltpu.CompilerParams(
            dimension_semantics=("arbitrary", "arbitrary", "arbitrary"),
            vmem_limit_bytes=VMEM_LIMIT),
    )(qt, kt, vt)


def _merge_mlp_kernel(x_ref, oa_ref, ob_ref, ga_ref, gb_ref, wpa_ref, wpb_ref, wo_ref, n2_ref,
                      wup_ref, wdn_ref, nf_ref, y_ref, *, ff_step):
    y_a = _mm(oa_ref[...], wpa_ref[...])
    y_b = _mm(ob_ref[...], wpb_ref[...])
    mixed = _sigmoid(ga_ref[...]) * y_a + _sigmoid(gb_ref[...]) * y_b
    h = x_ref[...] + _mm(_bf(mixed), wo_ref[...])
    hn = _bf(_rms(h, n2_ref[...]))
    out = h
    d_ff = wup_ref.shape[1]
    for c in range(d_ff // ff_step):
        u = jnp.maximum(_mm(hn, wup_ref[:, c * ff_step:(c + 1) * ff_step]), 0.0)
        out = out + _mm(_bf(u * u), wdn_ref[c * ff_step:(c + 1) * ff_step, :])
    y_ref[...] = _rms(out, nf_ref[...])


def _merge_mlp(x2d, oa, ob, ga, gb, wpa, wpb, wo, n2, wup, wdn, nf, *, tm):
    rows, d_model = x2d.shape
    d_ff = wup.shape[1]

    def row_spec(n):
        return pl.BlockSpec((tm, n), lambda i: (i, 0))

    return pl.pallas_call(
        functools.partial(_merge_mlp_kernel, ff_step=min(d_ff, 1024)),
        grid=(rows // tm,),
        in_specs=[row_spec(d_model), row_spec(W_V_A), row_spec(W_B), row_spec(d_model),
                  row_spec(d_model),
                  _const_spec(wpa.shape), _const_spec(wpb.shape), _const_spec(wo.shape),
                  _const_spec((1, d_model)), _const_spec(wup.shape), _const_spec(wdn.shape),
                  _const_spec((1, d_model))],
        out_specs=row_spec(d_model),
        out_shape=jax.ShapeDtypeStruct((rows, d_model), F32),
        compiler_params=pltpu.CompilerParams(dimension_semantics=("arbitrary",),
                                             vmem_limit_bytes=VMEM_LIMIT),
    )(x2d, oa, ob, ga, gb, wpa, wpb, wo, n2.reshape(1, d_model), wup, wdn,
      nf.reshape(1, d_model))


def _delta_step_kernel(alog_ref, dtb_ref, x_ref, z_ref, lg_ref, sc_ref, cw_ref, sd_ref, nw_ref,
                       o_ref, snew_ref, cnew_ref):
    x = x_ref[0]
    hist = sc_ref[0]
    w = cw_ref[...]
    y = x * w[CONV_W - 1:CONV_W]
    for i in range(CONV_W - 1):
        y = y + hist[i:i + 1] * w[i:i + 1]
    y = _silu(y)
    cnew_ref[0, 0:CONV_W - 2, :] = hist[1:CONV_W - 1]
    cnew_ref[0, CONV_W - 2:CONV_W - 1, :] = x
    lg = lg_ref[0]
    zz = z_ref[0]
    nw = nw_ref[...]
    sq = (DK_A, DK_A)
    eye = _iota(sq, 0) == _iota(sq, 1)

    def l2n(t):
        return t * lax.rsqrt(jnp.sum(t * t, axis=-1, keepdims=True) + EPS)

    for h in range(H_A):
        q = l2n(y[:, h * DK_A:(h + 1) * DK_A]) * (DK_A ** -0.5)
        k = l2n(y[:, W_QK_A + h * DK_A:W_QK_A + (h + 1) * DK_A])
        v = y[:, 2 * W_QK_A + h * DV_A:2 * W_QK_A + (h + 1) * DV_A]
        beta = _sigmoid(lg[:, h:h + 1])
        neg_a = -jnp.exp(jnp.full((1, 1), alog_ref[h], F32))
        g = neg_a * _softplus(lg[:, H_A + h:H_A + h + 1] + dtb_ref[h])
        state = sd_ref[0, h] * jnp.exp(g)
        kv = _mm(jnp.broadcast_to(k, (SUBLANES, DK_A)), state, HIGHEST)[0:1]
        dv = (v - kv) * beta
        k_diag = jnp.where(eye, jnp.broadcast_to(k, sq), 0.0)
        state = state + _mm(k_diag, jnp.broadcast_to(dv, (DK_A, DV_A)), HIGHEST)
        snew_ref[0, h] = state
        o = _mm(jnp.broadcast_to(q, (SUBLANES, DK_A)), state, HIGHEST)[0:1]
        gate = _silu(zz[:, h * DV_A:(h + 1) * DV_A])
        o_ref[0, :, h * DV_A:(h + 1) * DV_A] = _bf(_rms(o, nw) * gate)


def _delta_step(qkv, z, lg, state_conv, conv_w, state_delta, a_log, dt_bias, norm_w):
    nb = qkv.shape[0]
    smem = pl.BlockSpec(memory_space=pltpu.SMEM)

    def per_b(shape):
        nd = len(shape)
        return pl.BlockSpec((1,) + shape, lambda b: (b,) + (0,) * nd)

    return pl.pallas_call(
        _delta_step_kernel,
        grid=(nb,),
        in_specs=[smem, smem, per_b((1, C_CONV)), per_b((1, W_V_A)), per_b((1, LOGIT_PAD)),
                  per_b((CONV_W - 1, C_CONV)), _const_spec((CONV_W, C_CONV)),
                  per_b((H_A, DK_A, DV_A)), _const_spec((1, DV_A))],
        out_specs=[per_b((1, W_V_A)), per_b((H_A, DK_A, DV_A)), per_b((CONV_W - 1, C_CONV))],
        out_shape=[jax.ShapeDtypeStruct((nb, 1, W_V_A), BF16),
                   jax.ShapeDtypeStruct((nb, H_A, DK_A, DV_A), F32),
                   jax.ShapeDtypeStruct((nb, CONV_W - 1, C_CONV), F32)],
        compiler_params=pltpu.CompilerParams(dimension_semantics=("arbitrary",),
                                             vmem_limit_bytes=VMEM_LIMIT),
    )(a_log, dt_bias, qkv.reshape(nb, 1, C_CONV), z.reshape(nb, 1, W_V_A),
      lg.reshape(nb, 1, LOGIT_PAD), state_conv, conv_w, state_delta, norm_w.reshape(1, DV_A))


KMEAN_SLOTS = 16


def _kmean_topk_kernel(pt_ref, qt_ref, cache_ref, idx_ref, buf, sems, km_s,
                       *, n_seq, n_full, ppb, page_size, kk):
    b = pl.program_id(0)
    pages_per_seq = n_full * ppb
    total = n_seq * pages_per_seq

    def copy(p, slot):
        return pltpu.make_async_copy(cache_ref.at[pt_ref[p]], buf.at[slot], sems.at[slot])

    @pl.when(b == 0)
    def _():
        for s in range(KMEAN_SLOTS):
            copy(s, s).start()

    base = b * pages_per_seq
    lane = _iota((HD_B, LANES), 1)
    km_s[...] = jnp.zeros_like(km_s)

    def block(n, carry):
        p0 = base + n * ppb
        slot0 = p0 % KMEAN_SLOTS
        for i in range(ppb):
            copy(p0 + i, slot0 + i).wait()
        for h in range(H_B):
            x = buf[slot0, h]
            for i in range(1, ppb):
                x = x + buf[slot0 + i, h]
            mean = jnp.sum(x, axis=-1, keepdims=True) * (1.0 / (ppb * page_size))
            rows = slice(h * HD_B, (h + 1) * HD_B)
            km_s[rows, :] = jnp.where(lane == n, mean, km_s[rows, :])
        for i in range(ppb):
            @pl.when(p0 + i + KMEAN_SLOTS < total)
            def _():
                copy(p0 + i + KMEAN_SLOTS, slot0 + i).start()
        return carry

    lax.fori_loop(0, n_full, block, 0)

    qt = qt_ref[...]
    q_col = jnp.sum(jnp.where(_iota(qt.shape, 1) == b, qt, 0.0), axis=-1, keepdims=True)
    prod = km_s[...] * q_col
    sc = jnp.concatenate(
        [jnp.sum(prod[h * HD_B:(h + 1) * HD_B], axis=0, keepdims=True) for h in range(H_B)],
        axis=0)
    blk = _iota(sc.shape, 1)
    sc = jnp.where(blk < n_full, sc, NEG_INF)
    rank = _block_rank(sc, n_full, 1)
    blk_f = blk.astype(F32)
    out = jnp.zeros(sc.shape, F32)
    for r in range(kk):
        pick = jnp.sum(jnp.where(rank == float(r), blk_f, 0.0), axis=-1, keepdims=True)
        out = jnp.where(blk == r, pick, out)
    idx_ref[0] = out.astype(jnp.int32)


def _kmean_topk(page_ids, q_t, cache_t, *, n_full, ppb, kk):
    _, n_seq = q_t.shape
    _, _, _, page_size = cache_t.shape
    assert n_full <= LANES and KMEAN_SLOTS % ppb == 0 and n_seq * n_full * ppb >= KMEAN_SLOTS
    return pl.pallas_call(
        functools.partial(_kmean_topk_kernel, n_seq=n_seq, n_full=n_full, ppb=ppb,
                          page_size=page_size, kk=kk),
        grid_spec=pltpu.PrefetchScalarGridSpec(
            num_scalar_prefetch=1,
            grid=(n_seq,),
            in_specs=[_const_spec(q_t.shape), pl.BlockSpec(memory_space=pl.ANY)],
            out_specs=pl.BlockSpec((1, H_B, LANES), lambda b, *_: (b, 0, 0)),
            scratch_shapes=[pltpu.VMEM((KMEAN_SLOTS, H_B, HD_B, page_size), F32),
                            pltpu.SemaphoreType.DMA((KMEAN_SLOTS,)),
                            pltpu.VMEM((W_B, LANES), F32)]),
        out_shape=jax.ShapeDtypeStruct((n_seq, H_B, LANES), jnp.int32),
        compiler_params=pltpu.CompilerParams(dimension_semantics=("arbitrary",),
                                             vmem_limit_bytes=VMEM_LIMIT),
    )(page_ids, q_t, cache_t)


def _moba_sample_kernel(top_ref, pt_ref, q_ref, kn_ref, vn_ref, ck_ref, cv_ref, o_ref,
                        kbuf, vbuf, sems, *, kk, ppb, n_pages, page_size):
    b = pl.program_id(0)
    pages_per_head = kk * ppb

    def copies(h, i):
        blk = top_ref[(b * H_B + h) * kk + i // ppb]
        page = pt_ref[b * n_pages + blk * ppb + i % ppb]
        keys = pl.ds(i * page_size, page_size)
        return (pltpu.make_async_copy(ck_ref.at[page, h], kbuf.at[h, :, keys], sems.at[0, h]),
                pltpu.make_async_copy(cv_ref.at[page, h], vbuf.at[h, :, keys], sems.at[1, h]))

    for h in range(H_B):
        for i in range(pages_per_head):
            for c in copies(h, i):
                c.start()

    q = q_ref[0]
    kn = kn_ref[0]
    vn = vn_ref[0]
    for h in range(H_B):
        for i in range(pages_per_head):
            for c in copies(h, i):
                c.wait()
        sl = slice(h * HD_B, (h + 1) * HD_B)
        qh = q[:, sl] * (HD_B ** -0.5)
        s = _mm(_bf(jnp.broadcast_to(qh, (SUBLANES, HD_B))), _bf(kbuf[h]))[0:1]
        s_own = jnp.sum(qh * kn[:, sl], axis=-1, keepdims=True)
        m = jnp.maximum(jnp.max(s, axis=-1, keepdims=True), s_own)
        p = jnp.exp(s - m)
        p_own = jnp.exp(s_own - m)
        l = jnp.sum(p, axis=-1, keepdims=True) + p_own
        pv = _mm_nt(_bf(jnp.broadcast_to(p, (SUBLANES, p.shape[1]))), _bf(vbuf[h]))[0:1]
        o_ref[0, :, sl] = _bf((pv + p_own * vn[:, sl]) / l)


def _moba_sample(top_flat, pt_flat, q3, kn3, vn3, cache_kt, cache_vt, *, kk, ppb):
    nb = q3.shape[0]
    _, _, _, page_size = cache_kt.shape
    keys = kk * ppb * page_size
    row = pl.BlockSpec((1, 1, W_B), lambda b, *_: (b, 0, 0))
    any_spec = pl.BlockSpec(memory_space=pl.ANY)
    return pl.pallas_call(
        functools.partial(_moba_sample_kernel, kk=kk, ppb=ppb,
                          n_pages=pt_flat.shape[0] // nb, page_size=page_size),
        grid_spec=pltpu.PrefetchScalarGridSpec(
            num_scalar_prefetch=2,
            grid=(nb,),
            in_specs=[row, row, row, any_spec, any_spec],
            out_specs=row,
            scratch_shapes=[pltpu.VMEM((H_B, HD_B, keys), F32),
                            pltpu.VMEM((H_B, HD_B, keys), F32),
                            pltpu.SemaphoreType.DMA((2, H_B))]),
        out_shape=jax.ShapeDtypeStruct((nb, 1, W_B), BF16),
        compiler_params=pltpu.CompilerParams(dimension_semantics=("arbitrary",),
                                             vmem_limit_bytes=VMEM_LIMIT),
    )(top_flat, pt_flat, q3, kn3, vn3, cache_kt, cache_vt)


def _rearranged_w_in(w_in, d_model):
    o_z = C_CONV
    o_lg = o_z + W_V_A
    o_b = o_lg + 2 * H_A
    o_g = o_b + 3 * W_B
    pad = jnp.zeros((d_model, LOGIT_PAD - 2 * H_A), w_in.dtype)
    return _bf(jnp.concatenate(
        [w_in[:, :o_lg], w_in[:, o_b:o_g + 2 * d_model], w_in[:, o_lg:o_b], pad], axis=1))


def kernel(x_prompt, x_sample, cache_k, cache_v, page_table, state_delta, state_conv, norm1_w,
           w_in, conv_w, a_log, dt_bias, delta_norm_w, w_proj_a, w_proj_b, w_out, norm2_w, w_up,
           w_down, norm_f_w):
    depth = w_in.shape[0]
    assert depth == 1, "single-layer trunk"
    bp, sp, d_model = x_prompt.shape
    bs, ss, _ = x_sample.shape
    assert ss == 1, "one new token per sample sequence"
    _, n_pool, page_size, _, _ = cache_k.shape
    n_pages = page_table.shape[1]
    past_len = n_pages * page_size
    ppb = MOBA_BLOCK // page_size
    n_full = past_len // MOBA_BLOCK
    assert n_full * MOBA_BLOCK == past_len, "the sample token starts a fresh MoBA block"
    kk = min(MOBA_TOPK, n_full)
    assert kk >= 1

    w_r = _rearranged_w_in(w_in[0], d_model)
    o_qb = C_CONV + W_V_A + 2 * H_A
    wqkv_t = _bf(w_in[0][:, o_qb:o_qb + 3 * W_B].T)
    wpa, wpb, wo = _bf(w_proj_a[0]), _bf(w_proj_b[0]), _bf(w_out[0])
    wup, wdn = _bf(w_up[0]), _bf(w_down[0])

    tm = 256
    pos_p = jnp.arange(sp, dtype=jnp.int32)
    (qkv, z, qt, kt, vt, ga, gb, lg, conv_p) = _in_proj(
        x_prompt.reshape(bp * sp, d_model), norm1_w[0], w_r, _rope_tables(pos_p),
        tm=tm, rows_per_seq=sp, wqkv_t=wqkv_t, tabs_t=_rope_tables_t(pos_p))
    o_a, s_p = _delta_prompt(qkv.reshape(bp, sp, C_CONV), z.reshape(bp, sp, W_V_A),
                             lg.reshape(bp, sp, LOGIT_PAD), conv_w[0], a_log[0], dt_bias[0],
                             delta_norm_w[0])
    o_b = _moba_prompt(qt, kt, vt)
    y_p = _merge_mlp(x_prompt.reshape(bp * sp, d_model), o_a.reshape(bp * sp, W_V_A),
                     o_b.reshape(bp * sp, W_B), ga, gb, wpa, wpb, wo, norm2_w[0], wup, wdn,
                     norm_f_w, tm=tm)

    pos_s = jnp.full((bs,), past_len, dtype=jnp.int32)
    (qkv_s, z_s, qr_s, kr_s, vr_s, ga_s, gb_s, lg_s) = _in_proj(
        x_sample.reshape(bs, d_model), norm1_w[0], w_r, _rope_tables(pos_s),
        tm=bs, rows_per_seq=bs)
    o_a_s, s_s, conv_s = _delta_step(qkv_s, z_s, lg_s, state_conv[0], conv_w[0], state_delta[0],
                                     a_log[0], dt_bias[0], delta_norm_w[0])
    cache_kt = cache_k.reshape(n_pool, page_size, H_B, HD_B).transpose(0, 2, 3, 1)
    cache_vt = cache_v.reshape(n_pool, page_size, H_B, HD_B).transpose(0, 2, 3, 1)
    top = _kmean_topk(page_table[:, :n_full * ppb].reshape(-1), qr_s.T, cache_kt,
                      n_full=n_full, ppb=ppb, kk=kk)
    top_flat = top[:, :, :kk].reshape(-1)
    o_b_s = _moba_sample(top_flat, page_table.reshape(-1), qr_s.reshape(bs, 1, W_B),
                         kr_s.reshape(bs, 1, W_B), vr_s.reshape(bs, 1, W_B), cache_kt, cache_vt,
                         kk=kk, ppb=ppb)
    y_s = _merge_mlp(x_sample.reshape(bs, d_model), o_a_s.reshape(bs, W_V_A),
                     o_b_s.reshape(bs, W_B), ga_s, gb_s, wpa, wpb, wo, norm2_w[0], wup, wdn,
                     norm_f_w, tm=bs)

    def kv_out(t):
        return t.reshape(1, bp, H_B, HD_B, sp).transpose(0, 1, 4, 2, 3)

    return (y_p.reshape(bp, sp, d_model), y_s.reshape(bs, ss, d_model),
            kv_out(kt), kv_out(vt),
            s_p.reshape(1, bp, H_A, DK_A, DV_A), conv_p.reshape(1, bp, CONV_W - 1, C_CONV),
            kr_s.reshape(1, bs, ss, H_B, HD_B), vr_s.reshape(1, bs, ss, H_B, HD_B),
            s_s.reshape(1, bs, H_A, DK_A, DV_A), conv_s.reshape(1, bs, CONV_W - 1, C_CONV))
```

```python
import functools
import math

import jax
import jax.numpy as jnp
from jax import lax
from jax.experimental import pallas as pl
from jax.experimental.pallas import tpu as pltpu

F32 = jnp.float32
BF16 = jnp.bfloat16
HIGHEST = lax.Precision.HIGHEST

H_A = 4
DK_A = 128
DV_A = 128
W_QK_A = H_A * DK_A
W_V_A = H_A * DV_A
C_CONV = 2 * W_QK_A + W_V_A
CONV_W = 4
H_B = 8
HD_B = 64
W_B = H_B * HD_B
MOBA_BLOCK = 256
MOBA_TOPK = 3
ROPE_THETA = 500000.0
ROT_DIM = HD_B // 4
EPS = 1e-6

LANES = 128
SUBLANES = 8
BF16_SUBLANES = 16
MOBA_HEADS_PER_STEP = 4
HEADS_PER_LANE_TILE = LANES // HD_B
LOGIT_PAD = LANES
DELTA_CHUNK = LANES
DELTA_TILE = 4 * DELTA_CHUNK
VMEM_LIMIT = 56 * 1024 * 1024
NEG_INF = float("-inf")


def _dot(a, b, dims, precision=None):
    return lax.dot_general(a, b, (dims, ((), ())), precision=precision,
                           preferred_element_type=F32)


def _mm(a, b, precision=None):
    return _dot(a, b, ((1,), (0,)), precision)


def _mm_nt(a, b, precision=None):
    return _dot(a, b, ((1,), (1,)), precision)


def _bf(x):
    return x.astype(BF16)


def _sigmoid(x):
    return 1.0 / (1.0 + jnp.exp(-x))


def _silu(x):
    return x * _sigmoid(x)


def _softplus(x):
    return jnp.maximum(x, 0.0) + jnp.log(1.0 + jnp.exp(-jnp.abs(x)))


def _rms(x, w):
    return x * lax.rsqrt(jnp.mean(x * x, axis=-1, keepdims=True) + EPS) * w


def _iota(shape, dim):
    return lax.broadcasted_iota(jnp.int32, shape, dim)


def _const_spec(shape):
    nd = len(shape)
    return pl.BlockSpec(shape, lambda *_: (0,) * nd, pipeline_mode=pl.Buffered(1))


def _block_rank(sc, n_blocks, axis):
    blk = _iota(sc.shape, axis)
    rank = jnp.zeros(sc.shape, F32)
    for j in range(n_blocks):
        cj = sc[j:j + 1, :] if axis == 0 else sc[:, j:j + 1]
        first_on_tie = jnp.where(blk > j, 1.0, 0.0)
        rank = rank + jnp.where(cj > sc, 1.0, jnp.where(cj == sc, first_on_tie, 0.0))
    return rank


def _split_bf16(x):
    hi = _bf(x)
    return hi, _bf(x - hi.astype(F32))


def _mm_3pass(a, b):
    a_hi, a_lo = _split_bf16(a)
    b_hi, b_lo = _split_bf16(b)
    return _mm(jnp.concatenate([a_hi, a_lo, a_hi], axis=1),
               jnp.concatenate([b_hi, b_hi, b_lo], axis=0))


def _in_proj_kernel(x_ref, nw_ref, w_ref, cos_ref, slo_ref, shi_ref, *rest,
                    d_model, tiles_per_seq, kv_transposed):
    if kv_transposed:
        (wqkv_t_ref, cos_t_ref, sin_t_ref,
         qkv_ref, z_ref, qb_ref, kb_ref, vb_ref, ga_ref, gb_ref, lg_ref, conv_ref) = rest
    else:
        qkv_ref, z_ref, qb_ref, kb_ref, vb_ref, ga_ref, gb_ref, lg_ref = rest
    xb = _bf(_rms(x_ref[...], nw_ref[...]))
    tm = xb.shape[0]
    half = ROT_DIM // 2

    def cols(c0, n):
        return _mm(xb, w_ref[:, c0:c0 + n])

    def rope_store(dst_ref, c0):
        cos, slo, shi = cos_ref[...], slo_ref[...], shi_ref[...]
        for c in range(W_B // LANES):
            y = cols(c0 + c * LANES, LANES)
            up = pltpu.roll(y, LANES - half, 1)
            dn = pltpu.roll(y, half, 1)
            dst_ref[:, c * LANES:(c + 1) * LANES] = y * cos + up * slo + dn * shi

    def rope_store_t(dst_ref, r0):
        cos_t, sin_t = cos_t_ref[...], sin_t_ref[...]
        y_t = _mm_nt(wqkv_t_ref[r0:r0 + W_B, :], xb)
        for h in range(H_B):
            r = h * HD_B
            x1, x2 = y_t[r:r + half], y_t[r + half:r + ROT_DIM]
            dst_ref[0, r:r + half, :] = x1 * cos_t - x2 * sin_t
            dst_ref[0, r + half:r + ROT_DIM, :] = x2 * cos_t + x1 * sin_t
            dst_ref[0, r + ROT_DIM:r + HD_B, :] = y_t[r + ROT_DIM:r + HD_B]

    step = 512
    c0 = 0
    for c in range(C_CONV // step):
        qkv_ref[:, c * step:(c + 1) * step] = cols(c0, step)
        c0 += step
    z_ref[...] = cols(c0, W_V_A)
    c0 += W_V_A
    if kv_transposed:
        rope_store_t(qb_ref, 0)
        rope_store_t(kb_ref, W_B)
        vb_ref[0] = _mm_nt(wqkv_t_ref[2 * W_B:3 * W_B, :], xb)
    else:
        rope_store(qb_ref, c0)
        rope_store(kb_ref, c0 + W_B)
        vb_ref[...] = cols(c0 + 2 * W_B, W_B)
    c0 += 3 * W_B
    for g_ref in (ga_ref, gb_ref):
        for c in range(d_model // step):
            g_ref[:, c * step:(c + 1) * step] = cols(c0, step)
            c0 += step
    lg_ref[...] = cols(c0, LOGIT_PAD)

    if kv_transposed:
        @pl.when(pl.program_id(0) % tiles_per_seq == tiles_per_seq - 1)
        def _():
            conv_ref[0] = qkv_ref[tm - (CONV_W - 1):tm, :]


def _in_proj(x2d, norm_w, w_r, tabs, *, tm, rows_per_seq, wqkv_t=None, tabs_t=None):
    rows, d_model = x2d.shape
    n_tiles = rows // tm
    tiles_per_seq = rows_per_seq // tm
    n_seq = rows // rows_per_seq
    kv_transposed = wqkv_t is not None

    def row_spec(n):
        return pl.BlockSpec((tm, n), lambda i: (i, 0))

    def sds(*shape):
        return jax.ShapeDtypeStruct(shape, F32)

    kv_t_spec = pl.BlockSpec((1, W_B, tm), lambda i: (i // tiles_per_seq, 0, i % tiles_per_seq))
    kv_shape = sds(n_seq, W_B, rows_per_seq) if kv_transposed else sds(rows, W_B)
    kv_spec = kv_t_spec if kv_transposed else row_spec(W_B)
    tab_spec = pl.BlockSpec((tm, LANES), lambda i: (i % tiles_per_seq, 0))
    in_specs = [row_spec(d_model), _const_spec((1, d_model)), _const_spec(w_r.shape),
                tab_spec, tab_spec, tab_spec]
    args = [x2d, norm_w.reshape(1, d_model), w_r, *tabs]
    out_shape = [sds(rows, C_CONV), sds(rows, W_V_A), kv_shape, kv_shape, kv_shape,
                 sds(rows, d_model), sds(rows, d_model), sds(rows, LOGIT_PAD)]
    out_specs = [row_spec(C_CONV), row_spec(W_V_A), kv_spec, kv_spec, kv_spec,
                 row_spec(d_model), row_spec(d_model), row_spec(LOGIT_PAD)]
    if kv_transposed:
        tab_t_spec = pl.BlockSpec((ROT_DIM // 2, tm), lambda i: (0, i % tiles_per_seq))
        in_specs += [_const_spec(wqkv_t.shape), tab_t_spec, tab_t_spec]
        args += [wqkv_t, *tabs_t]
        out_shape.append(sds(n_seq, CONV_W - 1, C_CONV))
        out_specs.append(pl.BlockSpec((1, CONV_W - 1, C_CONV),
                                      lambda i: (i // tiles_per_seq, 0, 0)))
    return pl.pallas_call(
        functools.partial(_in_proj_kernel, d_model=d_model, tiles_per_seq=tiles_per_seq,
                          kv_transposed=kv_transposed),
        grid=(n_tiles,),
        in_specs=in_specs,
        out_specs=out_specs,
        out_shape=out_shape,
        compiler_params=pltpu.CompilerParams(dimension_semantics=("arbitrary",),
                                             vmem_limit_bytes=VMEM_LIMIT),
    )(*args)


def _rope_angles(pos):
    half = ROT_DIM // 2
    inv_freq = jnp.exp(jnp.arange(half, dtype=F32) * (-2.0 * math.log(ROPE_THETA) / ROT_DIM))
    ang = pos.astype(F32)[:, None] * inv_freq[None, :]
    return jnp.cos(ang), jnp.sin(ang)


def _rope_tables_t(pos):
    cos, sin = _rope_angles(pos)
    return cos.T, sin.T


def _rope_tables(pos):
    half = ROT_DIM // 2
    cos, sin = _rope_angles(pos)
    n = pos.shape[0]
    ones = jnp.ones((n, HD_B - ROT_DIM), F32)
    zeros_h = jnp.zeros((n, half), F32)
    zeros_r = jnp.zeros((n, HD_B - ROT_DIM), F32)
    cos_h = jnp.concatenate([cos, cos, ones], axis=1)
    slo_h = jnp.concatenate([-sin, zeros_h, zeros_r], axis=1)
    shi_h = jnp.concatenate([zeros_h, sin, zeros_r], axis=1)
    tile = lambda t: jnp.tile(t, (1, HEADS_PER_LANE_TILE))
    return tile(cos_h), tile(slo_h), tile(shi_h)


def _delta_prompt_kernel(alog_ref, dtb_ref, x_ref, z_ref, lg_ref, cw_ref, nw_ref,
                         o_ref, s_ref,
                         hist_s, state_s, u_s, w_s, qg_s, qk_s, kdt_s, gl_s):
    t = pl.program_id(1)
    cs = DELTA_CHUNK
    n_chunks = x_ref.shape[1] // cs
    heads = range(H_A)

    @pl.when(t == 0)
    def _():
        hist_s[...] = jnp.zeros_like(hist_s)
        state_s[...] = jnp.zeros_like(state_s)

    sq = (cs, cs)
    row = _iota(sq, 0)
    col = _iota(sq, 1)
    causal = col <= row
    strict = col < row
    tril_b = _bf(jnp.where(causal, 1.0, 0.0))
    eye_f = jnp.where(col == row, 1.0, 0.0)
    neg_a = [-jnp.exp(jnp.full((1, 1), alog_ref[h], F32)) for h in heads]
    dtb = [dtb_ref[h] for h in heads]
    cw = cw_ref[...]

    def conv_silu(c, r0, c0):
        lanes = slice(c0, c0 + LANES)
        cur = x_ref[0, pl.ds(r0, cs), lanes]
        p0 = pl.multiple_of(jnp.maximum(r0 - SUBLANES, 0), SUBLANES)
        prev = jnp.where(c > 0, x_ref[0, pl.ds(p0, SUBLANES), lanes], hist_s[:, lanes])
        ext = jnp.concatenate([prev, cur], axis=0)
        w = cw[:, lanes]
        y = cur * w[CONV_W - 1:CONV_W]
        for i in range(CONV_W - 1):
            lo = SUBLANES - (CONV_W - 1) + i
            y = y + ext[lo:lo + cs] * w[i:i + 1]
        return _silu(y)

    def l2n(x):
        return x * lax.rsqrt(jnp.sum(x * x, axis=-1, keepdims=True) + EPS)

    def prep(c, carry):
        r0 = pl.multiple_of(c * cs, cs)
        rows = pl.ds(r0, cs)
        q = [l2n(conv_silu(c, r0, h * DK_A)) * (DK_A ** -0.5) for h in heads]
        k = [l2n(conv_silu(c, r0, W_QK_A + h * DK_A)) for h in heads]
        v = [conv_silu(c, r0, 2 * W_QK_A + h * DV_A) for h in heads]
        lg = lg_ref[0, rows, :]
        beta = [_sigmoid(lg[:, h:h + 1]) for h in heads]
        g = [neg_a[h] * _softplus(lg[:, H_A + h:H_A + h + 1] + dtb[h]) for h in heads]
        pieces = []
        for h in heads:
            rem = jnp.broadcast_to(g[h], (cs, LANES))
            for _ in range(3):
                piece = _bf(rem)
                pieces.append(piece)
                rem = rem - piece.astype(F32)
        csum = _mm(tril_b, jnp.concatenate(pieces, axis=1))
        gc = [csum[:, 3 * h * LANES:(3 * h + 1) * LANES]
              + csum[:, (3 * h + 1) * LANES:(3 * h + 2) * LANES]
              + csum[:, (3 * h + 2) * LANES:(3 * h + 3) * LANES] for h in heads]
        decay = [jnp.exp(jnp.where(causal, gc[h] - gc[h].T, NEG_INF)) for h in heads]
        kb = [k[h] * beta[h] for h in heads]
        a = [jnp.where(strict, _mm_nt(_bf(kb[h]), _bf(k[h])) * decay[h], 0.0) for h in heads]
        t_inv = [eye_f - a[h] for h in heads]
        a_pow = [_mm_3pass(a[h], a[h]) for h in heads]
        n_levels = int(math.log2(cs)) - 1
        for lvl in range(n_levels):
            if lvl < n_levels - 1:
                prod = [_mm_3pass(a_pow[h], jnp.concatenate([a_pow[h], t_inv[h]], axis=1))
                        for h in heads]
                a_pow = [prod[h][:, :cs] for h in heads]
                t_inv = [t_inv[h] + prod[h][:, cs:] for h in heads]
            else:
                t_inv = [t_inv[h] + _mm_3pass(a_pow[h], t_inv[h]) for h in heads]
        e_gc = [jnp.exp(gc[h]) for h in heads]
        uw = [_mm(_bf(t_inv[h]),
                  jnp.concatenate([_bf(v[h] * beta[h]), _bf(kb[h] * e_gc[h])], axis=1))
              for h in heads]
        qk = [jnp.where(causal, _mm_nt(_bf(q[h]), _bf(k[h])) * decay[h], 0.0) for h in heads]
        for h in heads:
            g_last = gc[h][cs - 1:cs, :]
            u_s[h, rows, :] = uw[h][:, :DV_A]
            w_s[h, rows, :] = _bf(uw[h][:, DV_A:])
            qk_s[h, rows, :] = _bf(qk[h])
            qg_s[h, rows, :] = _bf(q[h] * e_gc[h])
            kdt_s[h, c] = _bf((k[h] * jnp.exp(g_last - gc[h])).T)
            gl_s[h, c] = jnp.broadcast_to(jnp.exp(g_last), (SUBLANES, LANES))
        return carry

    lax.fori_loop(0, n_chunks, prep, 0)
    hist_s[...] = x_ref[0, n_chunks * cs - SUBLANES:n_chunks * cs, :]

    nw = nw_ref[...]

    def scan(c, carry):
        rows = pl.ds(pl.multiple_of(c * cs, cs), cs)
        state = [state_s[h] for h in heads]
        sb = [_bf(state[h]) for h in heads]
        ws = [_mm(jnp.concatenate([w_s[h, rows, :], qg_s[h, rows, :]], axis=0), sb[h])
              for h in heads]
        vb = [_bf(u_s[h, rows, :] - ws[h][:cs]) for h in heads]
        o = [ws[h][cs:] + _mm(qk_s[h, rows, :], vb[h]) for h in heads]
        new = [state[h] * gl_s[h, c][0:1] + _mm(kdt_s[h, c], vb[h]) for h in heads]
        for h in heads:
            state_s[h] = new[h]
            lanes = slice(h * DV_A, (h + 1) * DV_A)
            o_ref[0, rows, lanes] = _bf(_rms(o[h], nw) * _silu(z_ref[0, rows, lanes]))
        return carry

    lax.fori_loop(0, n_chunks, scan, 0)

    @pl.when(t == pl.num_programs(1) - 1)
    def _():
        s_ref[0] = state_s[...]


def _delta_prompt(qkv, z, lg, conv_w, a_log, dt_bias, norm_w):
    bsz, seq, _ = qkv.shape
    ts = min(DELTA_TILE, seq)
    n_chunks = ts // DELTA_CHUNK
    assert seq % ts == 0 and ts % DELTA_CHUNK == 0 and DK_A == LANES and DV_A == LANES

    def tile_spec(n):
        return pl.BlockSpec((1, ts, n), lambda b, t: (b, t, 0))

    def per_head(shape, dtype):
        return pltpu.VMEM((H_A,) + shape, dtype)

    smem = pl.BlockSpec(memory_space=pltpu.SMEM)
    return pl.pallas_call(
        _delta_prompt_kernel,
        grid=(bsz, seq // ts),
        in_specs=[smem, smem, tile_spec(C_CONV), tile_spec(W_V_A), tile_spec(LOGIT_PAD),
                  _const_spec((CONV_W, C_CONV)), _const_spec((1, DV_A))],
        out_specs=[tile_spec(W_V_A),
                   pl.BlockSpec((1, H_A, DK_A, DV_A), lambda b, t: (b, 0, 0, 0))],
        out_shape=[jax.ShapeDtypeStruct((bsz, seq, W_V_A), BF16),
                   jax.ShapeDtypeStruct((bsz, H_A, DK_A, DV_A), F32)],
        scratch_shapes=[pltpu.VMEM((SUBLANES, C_CONV), F32),
                        per_head((DK_A, DV_A), F32),
                        per_head((ts, DV_A), F32),
                        per_head((ts, DK_A), BF16),
                        per_head((ts, DK_A), BF16),
                        per_head((ts, DELTA_CHUNK), BF16),
                        per_head((n_chunks, DK_A, DELTA_CHUNK), BF16),
                        per_head((n_chunks, SUBLANES, LANES), F32)],
        compiler_params=pltpu.CompilerParams(dimension_semantics=("arbitrary", "arbitrary"),
                                             vmem_limit_bytes=VMEM_LIMIT),
    )(a_log, dt_bias, qkv, z, lg, conv_w, norm_w.reshape(1, DV_A))


def _moba_prompt_kernel(qt_ref, kt_ref, vt_ref, o_ref, kn_s, vt_s, km_s, ch_s, *, n_blocks):
    qi = pl.program_id(2)
    bs = MOBA_BLOCK
    n_pairs = qt_ref.shape[1] // LANES
    heads = range(n_pairs * HEADS_PER_LANE_TILE)
    pair_of = [h // HEADS_PER_LANE_TILE for h in heads]

    @pl.when(qi == 0)
    def _():
        km_s[...] = jnp.zeros_like(km_s)
        ones = jnp.ones((BF16_SUBLANES, bs), BF16)
        for j in range(n_blocks):
            keys = slice(j * bs, (j + 1) * bs)
            for pr in range(n_pairs):
                kj = kt_ref[0, pr * LANES:(pr + 1) * LANES, keys].T
                kn_s[j, pr] = _bf(kj)
                km_s[pr, j:j + 1, :] = jnp.mean(kj, axis=0, keepdims=True)
            for h in heads:
                vt_s[j, h, 0:HD_B, :] = _bf(vt_ref[0, h * HD_B:(h + 1) * HD_B, keys])
                vt_s[j, h, HD_B:, :] = ones

    d_row = _iota((LANES, bs), 0)
    blk = _iota((km_s.shape[1], bs), 0)
    key_le_query = _iota((bs, bs), 0) <= _iota((bs, bs), 1)
    qh = []
    for h in heads:
        hh = h % HEADS_PER_LANE_TILE
        qt = qt_ref[0, pair_of[h] * LANES:(pair_of[h] + 1) * LANES, :]
        qh.append(jnp.where((d_row >= hh * HD_B) & (d_row < (hh + 1) * HD_B), qt, 0.0))
    sc = [jnp.where(blk < qi, _mm(km_s[pair_of[h]], qh[h], HIGHEST), NEG_INF) for h in heads]
    rank = [_block_rank(sc[h], n_blocks, 0) for h in heads]
    for h in heads:
        ch_s[h] = jnp.where(blk < qi, jnp.where(rank[h] < MOBA_TOPK, 1.0, 0.0), 0.0)

    qs = [_bf(qh[h] * (HD_B ** -0.5 * math.log2(math.e))) for h in heads]
    s = [jnp.where(key_le_query, _mm(kn_s[qi, pair_of[h]], qs[h]), NEG_INF)
         for h in heads]
    m = [jnp.max(s[h], axis=0, keepdims=True) for h in heads]
    acc = [_mm(vt_s[qi, h], _bf(jnp.exp2(s[h] - m[h]))) for h in heads]

    def two_past_blocks(jj, carry):
        m, acc = carry
        js = (2 * jj, 2 * jj + 1)
        s = [[jnp.where(ch_s[h, pl.ds(j, 1), :] > 0.0, _mm(kn_s[j, pair_of[h]], qs[h]), NEG_INF)
              for j in js] for h in heads]
        m_new = [jnp.maximum(m[h], jnp.maximum(jnp.max(s[h][0], axis=0, keepdims=True),
                                               jnp.max(s[h][1], axis=0, keepdims=True)))
                 for h in heads]
        acc = [jnp.exp2(m[h] - m_new[h]) * acc[h]
               + _mm(vt_s[js[0], h], _bf(jnp.exp2(s[h][0] - m_new[h])))
               + _mm(vt_s[js[1], h], _bf(jnp.exp2(s[h][1] - m_new[h]))) for h in heads]
        return m_new, acc

    m, acc = lax.fori_loop(0, (qi + 1) // 2, two_past_blocks, (m, acc))
    o_t = jnp.concatenate([acc[h][:HD_B] / acc[h][HD_B:HD_B + 1] for h in heads], axis=0)
    o_ref[0] = _bf(o_t.T)


def _moba_prompt(qt, kt, vt):
    bsz, _, seq = qt.shape
    n_blocks = seq // MOBA_BLOCK
    assert seq % MOBA_BLOCK == 0
    blocks_pad = -(-n_blocks // SUBLANES) * SUBLANES
    n_pairs = MOBA_HEADS_PER_STEP // HEADS_PER_LANE_TILE
    rows = n_pairs * LANES
    q_spec = pl.BlockSpec((1, rows, MOBA_BLOCK), lambda b, p, i: (b, p, i))
    kv_spec = pl.BlockSpec((1, rows, seq), lambda b, p, i: (b, p, 0))
    return pl.pallas_call(
        functools.partial(_moba_prompt_kernel, n_blocks=n_blocks),
        grid=(bsz, H_B // MOBA_HEADS_PER_STEP, n_blocks),
        in_specs=[q_spec, kv_spec, kv_spec],
        out_specs=pl.BlockSpec((1, MOBA_BLOCK, rows), lambda b, p, i: (b, i, p)),
        out_shape=jax.ShapeDtypeStruct((bsz, seq, W_B), BF16),
        scratch_shapes=[pltpu.VMEM((n_blocks, n_pairs, MOBA_BLOCK, LANES), BF16),
                        pltpu.VMEM((n_blocks, MOBA_HEADS_PER_STEP, HD_B + BF16_SUBLANES,
                                    MOBA_BLOCK), BF16),
                        pltpu.VMEM((n_pairs, blocks_pad, LANES), F32),
                        pltpu.VMEM((MOBA_HEADS_PER_STEP, blocks_pad, MOBA_BLOCK), F32)],
        compiler_params=pltpu.CompilerParams(
            dimension_semantics=("arbitrary", "arbitrary", "arbitrary"),
            vmem_limit_bytes=VMEM_LIMIT),
    )(qt, kt, vt)


def _merge_mlp_kernel(x_ref, oa_ref, ob_ref, ga_ref, gb_ref, wpa_ref, wpb_ref, wo_ref, n2_ref,
                      wup_ref, wdn_ref, nf_ref, y_ref, *, ff_step):
    y_a = _mm(oa_ref[...], wpa_ref[...])
    y_b = _mm(ob_ref[...], wpb_ref[...])
    mixed = _sigmoid(ga_ref[...]) * y_a + _sigmoid(gb_ref[...]) * y_b
    h = x_ref[...] + _mm(_bf(mixed), wo_ref[...])
    hn = _bf(_rms(h, n2_ref[...]))
    out = h
    d_ff = wup_ref.shape[1]
    for c in range(d_ff // ff_step):
        u = jnp.maximum(_mm(hn, wup_ref[:, c * ff_step:(c + 1) * ff_step]), 0.0)
        out = out + _mm(_bf(u * u), wdn_ref[c * ff_step:(c + 1) * ff_step, :])
    y_ref[...] = _rms(out, nf_ref[...])


def _merge_mlp(x2d, oa, ob, ga, gb, wpa, wpb, wo, n2, wup, wdn, nf, *, tm):
    rows, d_model = x2d.shape
    d_ff = wup.shape[1]

    def row_spec(n):
        return pl.BlockSpec((tm, n), lambda i: (i, 0))

    return pl.pallas_call(
        functools.partial(_merge_mlp_kernel, ff_step=min(d_ff, 1024)),
        grid=(rows // tm,),
        in_specs=[row_spec(d_model), row_spec(W_V_A), row_spec(W_B), row_spec(d_model),
                  row_spec(d_model),
                  _const_spec(wpa.shape), _const_spec(wpb.shape), _const_spec(wo.shape),
                  _const_spec((1, d_model)), _const_spec(wup.shape), _const_spec(wdn.shape),
                  _const_spec((1, d_model))],
        out_specs=row_spec(d_model),
        out_shape=jax.ShapeDtypeStruct((rows, d_model), F32),
        compiler_params=pltpu.CompilerParams(dimension_semantics=("arbitrary",),
                                             vmem_limit_bytes=VMEM_LIMIT),
    )(x2d, oa, ob, ga, gb, wpa, wpb, wo, n2.reshape(1, d_model), wup, wdn,
      nf.reshape(1, d_model))


def _delta_step_kernel(alog_ref, dtb_ref, x_ref, z_ref, lg_ref, sc_ref, cw_ref, sd_ref, nw_ref,
                       o_ref, snew_ref, cnew_ref):
    x = x_ref[0]
    hist = sc_ref[0]
    w = cw_ref[...]
    y = x * w[CONV_W - 1:CONV_W]
    for i in range(CONV_W - 1):
        y = y + hist[i:i + 1] * w[i:i + 1]
    y = _silu(y)
    cnew_ref[0, 0:CONV_W - 2, :] = hist[1:CONV_W - 1]
    cnew_ref[0, CONV_W - 2:CONV_W - 1, :] = x
    lg = lg_ref[0]
    zz = z_ref[0]
    nw = nw_ref[...]
    sq = (DK_A, DK_A)
    eye = _iota(sq, 0) == _iota(sq, 1)

    def l2n(t):
        return t * lax.rsqrt(jnp.sum(t * t, axis=-1, keepdims=True) + EPS)

    for h in range(H_A):
        q = l2n(y[:, h * DK_A:(h + 1) * DK_A]) * (DK_A ** -0.5)
        k = l2n(y[:, W_QK_A + h * DK_A:W_QK_A + (h + 1) * DK_A])
        v = y[:, 2 * W_QK_A + h * DV_A:2 * W_QK_A + (h + 1) * DV_A]
        beta = _sigmoid(lg[:, h:h + 1])
        neg_a = -jnp.exp(jnp.full((1, 1), alog_ref[h], F32))
        g = neg_a * _softplus(lg[:, H_A + h:H_A + h + 1] + dtb_ref[h])
        state = sd_ref[0, h] * jnp.exp(g)
        kv = _mm(jnp.broadcast_to(k, (SUBLANES, DK_A)), state, HIGHEST)[0:1]
        dv = (v - kv) * beta
        k_diag = jnp.where(eye, jnp.broadcast_to(k, sq), 0.0)
        state = state + _mm(k_diag, jnp.broadcast_to(dv, (DK_A, DV_A)), HIGHEST)
        snew_ref[0, h] = state
        o = _mm(jnp.broadcast_to(q, (SUBLANES, DK_A)), state, HIGHEST)[0:1]
        gate = _silu(zz[:, h * DV_A:(h + 1) * DV_A])
        o_ref[0, :, h * DV_A:(h + 1) * DV_A] = _bf(_rms(o, nw) * gate)


def _delta_step(qkv, z, lg, state_conv, conv_w, state_delta, a_log, dt_bias, norm_w):
    nb = qkv.shape[0]
    smem = pl.BlockSpec(memory_space=pltpu.SMEM)

    def per_b(shape):
        nd = len(shape)
        return pl.BlockSpec((1,) + shape, lambda b: (b,) + (0,) * nd)

    return pl.pallas_call(
        _delta_step_kernel,
        grid=(nb,),
        in_specs=[smem, smem, per_b((1, C_CONV)), per_b((1, W_V_A)), per_b((1, LOGIT_PAD)),
                  per_b((CONV_W - 1, C_CONV)), _const_spec((CONV_W, C_CONV)),
                  per_b((H_A, DK_A, DV_A)), _const_spec((1, DV_A))],
        out_specs=[per_b((1, W_V_A)), per_b((H_A, DK_A, DV_A)), per_b((CONV_W - 1, C_CONV))],
        out_shape=[jax.ShapeDtypeStruct((nb, 1, W_V_A), BF16),
                   jax.ShapeDtypeStruct((nb, H_A, DK_A, DV_A), F32),
                   jax.ShapeDtypeStruct((nb, CONV_W - 1, C_CONV), F32)],
        compiler_params=pltpu.CompilerParams(dimension_semantics=("arbitrary",),
                                             vmem_limit_bytes=VMEM_LIMIT),
    )(a_log, dt_bias, qkv.reshape(nb, 1, C_CONV), z.reshape(nb, 1, W_V_A),
      lg.reshape(nb, 1, LOGIT_PAD), state_conv, conv_w, state_delta, norm_w.reshape(1, DV_A))


KMEAN_SLOTS = 16

_PAGE_DMA_PARAMS = pltpu.CompilerParams(dimension_semantics=("arbitrary",),
                                        vmem_limit_bytes=VMEM_LIMIT, disable_bounds_checks=True)


def _kmean_topk_kernel(pt_ref, qt_ref, cache_ref, idx_ref, buf, sems, km_s,
                       *, n_seq, n_full, ppb, page_size, kk):
    b = pl.program_id(0)
    pages_per_seq = n_full * ppb
    total = n_seq * pages_per_seq

    def copy(p, slot):
        return pltpu.make_async_copy(cache_ref.at[pt_ref[p]], buf.at[slot], sems.at[slot])

    @pl.when(b == 0)
    def _():
        for s in range(KMEAN_SLOTS):
            copy(s, s).start()

    base = b * pages_per_seq
    lane = _iota((HD_B, LANES), 1)
    km_s[...] = jnp.zeros_like(km_s)

    def block(n, carry):
        p0 = base + n * ppb
        slot0 = p0 % KMEAN_SLOTS
        for i in range(ppb):
            copy(p0 + i, slot0 + i).wait()
        for h in range(H_B):
            x = buf[slot0, h]
            for i in range(1, ppb):
                x = x + buf[slot0 + i, h]
            mean = jnp.sum(x, axis=-1, keepdims=True) * (1.0 / (ppb * page_size))
            rows = slice(h * HD_B, (h + 1) * HD_B)
            km_s[rows, :] = jnp.where(lane == n, mean, km_s[rows, :])
        for i in range(ppb):
            @pl.when(p0 + i + KMEAN_SLOTS < total)
            def _():
                copy(p0 + i + KMEAN_SLOTS, slot0 + i).start()
        return carry

    lax.fori_loop(0, n_full, block, 0)

    qt = qt_ref[...]
    q_col = jnp.sum(jnp.where(_iota(qt.shape, 1) == b, qt, 0.0), axis=-1, keepdims=True)
    prod = km_s[...] * q_col
    sc = jnp.concatenate(
        [jnp.sum(prod[h * HD_B:(h + 1) * HD_B], axis=0, keepdims=True) for h in range(H_B)],
        axis=0)
    blk = _iota(sc.shape, 1)
    sc = jnp.where(blk < n_full, sc, NEG_INF)
    rank = _block_rank(sc, n_full, 1)
    blk_f = blk.astype(F32)
    out = jnp.zeros(sc.shape, F32)
    for r in range(kk):
        pick = jnp.sum(jnp.where(rank == float(r), blk_f, 0.0), axis=-1, keepdims=True)
        out = jnp.where(blk == r, pick, out)
    idx_ref[0] = out.astype(jnp.int32)


def _kmean_topk(page_ids, q_t, cache_t, *, n_full, ppb, kk):
    _, n_seq = q_t.shape
    _, _, _, page_size = cache_t.shape
    assert n_full <= LANES and KMEAN_SLOTS % ppb == 0 and n_seq * n_full * ppb >= KMEAN_SLOTS
    return pl.pallas_call(
        functools.partial(_kmean_topk_kernel, n_seq=n_seq, n_full=n_full, ppb=ppb,
                          page_size=page_size, kk=kk),
        grid_spec=pltpu.PrefetchScalarGridSpec(
            num_scalar_prefetch=1,
            grid=(n_seq,),
            in_specs=[_const_spec(q_t.shape), pl.BlockSpec(memory_space=pl.ANY)],
            out_specs=pl.BlockSpec((1, H_B, LANES), lambda b, *_: (b, 0, 0)),
            scratch_shapes=[pltpu.VMEM((KMEAN_SLOTS, H_B, HD_B, page_size), F32),
                            pltpu.SemaphoreType.DMA((KMEAN_SLOTS,)),
                            pltpu.VMEM((W_B, LANES), F32)]),
        out_shape=jax.ShapeDtypeStruct((n_seq, H_B, LANES), jnp.int32),
        compiler_params=_PAGE_DMA_PARAMS,
    )(page_ids, q_t, cache_t)


def _moba_sample_kernel(top_ref, pt_ref, q_ref, kn_ref, vn_ref, ck_ref, cv_ref, o_ref,
                        kbuf, vbuf, sems, *, kk, ppb, n_pages, page_size, n_seq):
    b = pl.program_id(0)
    pages_per_head = kk * ppb

    def copies(seq, h, i, page=None):
        slot = seq % 2
        if page is None:
            blk = top_ref[(seq * H_B + h) * kk + i // ppb]
            page = pt_ref[seq * n_pages + blk * ppb + i % ppb]
        keys = pl.ds(i * page_size, page_size)
        return (pltpu.make_async_copy(ck_ref.at[page, h], kbuf.at[slot, h, :, keys],
                                      sems.at[slot, 0, h]),
                pltpu.make_async_copy(cv_ref.at[page, h], vbuf.at[slot, h, :, keys],
                                      sems.at[slot, 1, h]))

    def start_fetch(seq):
        for h in range(H_B):
            for i in range(pages_per_head):
                for c in copies(seq, h, i):
                    c.start()

    @pl.when(b == 0)
    def _():
        start_fetch(b)

    @pl.when(b + 1 < n_seq)
    def _():
        start_fetch(b + 1)

    slot = b % 2
    q = q_ref[0]
    kn = kn_ref[0]
    vn = vn_ref[0]
    for h in range(H_B):
        for i in range(pages_per_head):
            for c in copies(b, h, i, page=0):
                c.wait()
        sl = slice(h * HD_B, (h + 1) * HD_B)
        qh = q[:, sl] * (HD_B ** -0.5)
        s = _mm(_bf(jnp.broadcast_to(qh, (SUBLANES, HD_B))), _bf(kbuf[slot, h]))[0:1]
        s_own = jnp.sum(qh * kn[:, sl], axis=-1, keepdims=True)
        m = jnp.maximum(jnp.max(s, axis=-1, keepdims=True), s_own)
        p = jnp.exp(s - m)
        p_own = jnp.exp(s_own - m)
        l = jnp.sum(p, axis=-1, keepdims=True) + p_own
        pv = _mm_nt(_bf(jnp.broadcast_to(p, (SUBLANES, p.shape[1]))), _bf(vbuf[slot, h]))[0:1]
        o_ref[0, :, sl] = _bf((pv + p_own * vn[:, sl]) / l)


def _moba_sample(top_flat, pt_flat, q3, kn3, vn3, cache_kt, cache_vt, *, kk, ppb):
    nb = q3.shape[0]
    _, _, _, page_size = cache_kt.shape
    keys = kk * ppb * page_size
    row = pl.BlockSpec((1, 1, W_B), lambda b, *_: (b, 0, 0))
    any_spec = pl.BlockSpec(memory_space=pl.ANY)
    return pl.pallas_call(
        functools.partial(_moba_sample_kernel, kk=kk, ppb=ppb,
                          n_pages=pt_flat.shape[0] // nb, page_size=page_size, n_seq=nb),
        grid_spec=pltpu.PrefetchScalarGridSpec(
            num_scalar_prefetch=2,
            grid=(nb,),
            in_specs=[row, row, row, any_spec, any_spec],
            out_specs=row,
            scratch_shapes=[pltpu.VMEM((2, H_B, HD_B, keys), F32),
                            pltpu.VMEM((2, H_B, HD_B, keys), F32),
                            pltpu.SemaphoreType.DMA((2, 2, H_B))]),
        out_shape=jax.ShapeDtypeStruct((nb, 1, W_B), BF16),
        compiler_params=_PAGE_DMA_PARAMS,
    )(top_flat, pt_flat, q3, kn3, vn3, cache_kt, cache_vt)


def _rearranged_w_in(w_in, d_model):
    o_z = C_CONV
    o_lg = o_z + W_V_A
    o_b = o_lg + 2 * H_A
    o_g = o_b + 3 * W_B
    pad = jnp.zeros((d_model, LOGIT_PAD - 2 * H_A), w_in.dtype)
    return _bf(jnp.concatenate(
        [w_in[:, :o_lg], w_in[:, o_b:o_g + 2 * d_model], w_in[:, o_lg:o_b], pad], axis=1))


def kernel(x_prompt, x_sample, cache_k, cache_v, page_table, state_delta, state_conv, norm1_w,
           w_in, conv_w, a_log, dt_bias, delta_norm_w, w_proj_a, w_proj_b, w_out, norm2_w, w_up,
           w_down, norm_f_w):
    depth = w_in.shape[0]
    assert depth == 1, "single-layer trunk"
    bp, sp, d_model = x_prompt.shape
    bs, ss, _ = x_sample.shape
    assert ss == 1, "one new token per sample sequence"
    _, n_pool, page_size, _, _ = cache_k.shape
    n_pages = page_table.shape[1]
    past_len = n_pages * page_size
    ppb = MOBA_BLOCK // page_size
    n_full = past_len // MOBA_BLOCK
    assert n_full * MOBA_BLOCK == past_len, "the sample token starts a fresh MoBA block"
    kk = min(MOBA_TOPK, n_full)
    assert kk >= 1

    w_r = _rearranged_w_in(w_in[0], d_model)
    o_qb = C_CONV + W_V_A + 2 * H_A
    wqkv_t = _bf(w_in[0][:, o_qb:o_qb + 3 * W_B].T)
    wpa, wpb, wo = _bf(w_proj_a[0]), _bf(w_proj_b[0]), _bf(w_out[0])
    wup, wdn = _bf(w_up[0]), _bf(w_down[0])

    tm = 256
    pos_p = jnp.arange(sp, dtype=jnp.int32)
    (qkv, z, qt, kt, vt, ga, gb, lg, conv_p) = _in_proj(
        x_prompt.reshape(bp * sp, d_model), norm1_w[0], w_r, _rope_tables(pos_p),
        tm=tm, rows_per_seq=sp, wqkv_t=wqkv_t, tabs_t=_rope_tables_t(pos_p))
    o_a, s_p = _delta_prompt(qkv.reshape(bp, sp, C_CONV), z.reshape(bp, sp, W_V_A),
                             lg.reshape(bp, sp, LOGIT_PAD), conv_w[0], a_log[0], dt_bias[0],
                             delta_norm_w[0])
    o_b = _moba_prompt(qt, kt, vt)
    y_p = _merge_mlp(x_prompt.reshape(bp * sp, d_model), o_a.reshape(bp * sp, W_V_A),
                     o_b.reshape(bp * sp, W_B), ga, gb, wpa, wpb, wo, norm2_w[0], wup, wdn,
                     norm_f_w, tm=tm)

    pos_s = jnp.full((bs,), past_len, dtype=jnp.int32)
    (qkv_s, z_s, qr_s, kr_s, vr_s, ga_s, gb_s, lg_s) = _in_proj(
        x_sample.reshape(bs, d_model), norm1_w[0], w_r, _rope_tables(pos_s),
        tm=bs, rows_per_seq=bs)
    o_a_s, s_s, conv_s = _delta_step(qkv_s, z_s, lg_s, state_conv[0], conv_w[0], state_delta[0],
                                     a_log[0], dt_bias[0], delta_norm_w[0])
    cache_kt = cache_k.reshape(n_pool, page_size, H_B, HD_B).transpose(0, 2, 3, 1)
    cache_vt = cache_v.reshape(n_pool, page_size, H_B, HD_B).transpose(0, 2, 3, 1)
    top = _kmean_topk(page_table[:, :n_full * ppb].reshape(-1), qr_s.T, cache_kt,
                      n_full=n_full, ppb=ppb, kk=kk)
    top_flat = top[:, :, :kk].reshape(-1)
    o_b_s = _moba_sample(top_flat, page_table.reshape(-1), qr_s.reshape(bs, 1, W_B),
                         kr_s.reshape(bs, 1, W_B), vr_s.reshape(bs, 1, W_B), cache_kt, cache_vt,
                         kk=kk, ppb=ppb)
    y_s = _merge_mlp(x_sample.reshape(bs, d_model), o_a_s.reshape(bs, W_V_A),
                     o_b_s.reshape(bs, W_B), ga_s, gb_s, wpa, wpb, wo, norm2_w[0], wup, wdn,
                     norm_f_w, tm=bs)

    def kv_out(t):
        return t.reshape(1, bp, H_B, HD_B, sp).transpose(0, 1, 4, 2, 3)

    return (y_p.reshape(bp, sp, d_model), y_s.reshape(bs, ss, d_model),
            kv_out(kt), kv_out(vt),
            s_p.reshape(1, bp, H_A, DK_A, DV_A), conv_p.reshape(1, bp, CONV_W - 1, C_CONV),
            kr_s.reshape(1, bs, ss, H_B, HD_B), vr_s.reshape(1, bs, ss, H_B, HD_B),
            s_s.reshape(1, bs, H_A, DK_A, DV_A), conv_s.reshape(1, bs, CONV_W - 1, C_CONV))
```

```python
import functools
import math

import jax
import jax.numpy as jnp
from jax import lax
from jax.experimental import pallas as pl
from jax.experimental.pallas import tpu as pltpu

F32 = jnp.float32
BF16 = jnp.bfloat16
HIGHEST = lax.Precision.HIGHEST

H_A = 4
DK_A = 128
DV_A = 128
W_QK_A = H_A * DK_A
W_V_A = H_A * DV_A
C_CONV = 2 * W_QK_A + W_V_A
CONV_W = 4
H_B = 8
HD_B = 64
W_B = H_B * HD_B
MOBA_BLOCK = 256
MOBA_TOPK = 3
ROPE_THETA = 500000.0
ROT_DIM = HD_B // 4
EPS = 1e-6

LANES = 128
SUBLANES = 8
BF16_SUBLANES = 16
MOBA_HEADS_PER_STEP = 4
KMEAN_SLOTS = 32
KMEAN_GROUP = 16
HEADS_PER_LANE_TILE = LANES // HD_B
LOGIT_PAD = LANES
DELTA_CHUNK = LANES
DELTA_TILE = 4 * DELTA_CHUNK
VMEM_LIMIT = 56 * 1024 * 1024
NEG_INF = float("-inf")


def _dot(a, b, dims, precision=None):
    return lax.dot_general(a, b, (dims, ((), ())), precision=precision,
                           preferred_element_type=F32)


def _mm(a, b, precision=None):
    return _dot(a, b, ((1,), (0,)), precision)


def _mm_nt(a, b, precision=None):
    return _dot(a, b, ((1,), (1,)), precision)


def _bf(x):
    return x.astype(BF16)


def _sigmoid(x):
    return 1.0 / (1.0 + jnp.exp(-x))


def _silu(x):
    return x * _sigmoid(x)


def _softplus(x):
    return jnp.maximum(x, 0.0) + jnp.log(1.0 + jnp.exp(-jnp.abs(x)))


def _rms(x, w):
    return x * lax.rsqrt(jnp.mean(x * x, axis=-1, keepdims=True) + EPS) * w


def _iota(shape, dim):
    return lax.broadcasted_iota(jnp.int32, shape, dim)


def _const_spec(shape):
    nd = len(shape)
    return pl.BlockSpec(shape, lambda *_: (0,) * nd, pipeline_mode=pl.Buffered(1))


def _block_rank(sc, n_blocks, axis):
    blk = _iota(sc.shape, axis)
    rank = jnp.zeros(sc.shape, F32)
    for j in range(n_blocks):
        cj = sc[j:j + 1, :] if axis == 0 else sc[:, j:j + 1]
        first_on_tie = jnp.where(blk > j, 1.0, 0.0)
        rank = rank + jnp.where(cj > sc, 1.0, jnp.where(cj == sc, first_on_tie, 0.0))
    return rank


def _split_bf16(x):
    hi = _bf(x)
    return hi, _bf(x - hi.astype(F32))


def _mm_3pass(a, b):
    a_hi, a_lo = _split_bf16(a)
    b_hi, b_lo = _split_bf16(b)
    return _mm(jnp.concatenate([a_hi, a_lo, a_hi], axis=1),
               jnp.concatenate([b_hi, b_hi, b_lo], axis=0))


def _in_proj_kernel(x_ref, nw_ref, w_ref, cos_ref, slo_ref, shi_ref, *rest,
                    d_model, tiles_per_seq, kv_transposed):
    if kv_transposed:
        (wqkv_t_ref, cos_t_ref, sin_t_ref,
         qkv_ref, z_ref, qb_ref, kb_ref, vb_ref, ga_ref, gb_ref, lg_ref, conv_ref) = rest
    else:
        qkv_ref, z_ref, qb_ref, kb_ref, vb_ref, ga_ref, gb_ref, lg_ref = rest
    xb = _bf(_rms(x_ref[...], nw_ref[...]))
    tm = xb.shape[0]
    half = ROT_DIM // 2

    def cols(c0, n):
        return _mm(xb, w_ref[:, c0:c0 + n])

    def rope_store(dst_ref, c0):
        cos, slo, shi = cos_ref[...], slo_ref[...], shi_ref[...]
        for c in range(W_B // LANES):
            y = cols(c0 + c * LANES, LANES)
            up = pltpu.roll(y, LANES - half, 1)
            dn = pltpu.roll(y, half, 1)
            dst_ref[:, c * LANES:(c + 1) * LANES] = y * cos + up * slo + dn * shi

    def rope_store_t(dst_ref, r0):
        cos_t, sin_t = cos_t_ref[...], sin_t_ref[...]
        y_t = _mm_nt(wqkv_t_ref[r0:r0 + W_B, :], xb)
        for h in range(H_B):
            r = h * HD_B
            x1, x2 = y_t[r:r + half], y_t[r + half:r + ROT_DIM]
            dst_ref[0, r:r + half, :] = x1 * cos_t - x2 * sin_t
            dst_ref[0, r + half:r + ROT_DIM, :] = x2 * cos_t + x1 * sin_t
            dst_ref[0, r + ROT_DIM:r + HD_B, :] = y_t[r + ROT_DIM:r + HD_B]

    step = 512
    c0 = 0
    for c in range(C_CONV // step):
        qkv_ref[:, c * step:(c + 1) * step] = cols(c0, step)
        c0 += step
    z_ref[...] = cols(c0, W_V_A)
    c0 += W_V_A
    if kv_transposed:
        rope_store_t(qb_ref, 0)
        rope_store_t(kb_ref, W_B)
        vb_ref[0] = _mm_nt(wqkv_t_ref[2 * W_B:3 * W_B, :], xb)
    else:
        rope_store(qb_ref, c0)
        rope_store(kb_ref, c0 + W_B)
        vb_ref[...] = cols(c0 + 2 * W_B, W_B)
    c0 += 3 * W_B
    for g_ref in (ga_ref, gb_ref):
        for c in range(d_model // step):
            g_ref[:, c * step:(c + 1) * step] = cols(c0, step)
            c0 += step
    lg_ref[...] = cols(c0, LOGIT_PAD)

    if kv_transposed:
        @pl.when(pl.program_id(0) % tiles_per_seq == tiles_per_seq - 1)
        def _():
            conv_ref[0] = qkv_ref[tm - (CONV_W - 1):tm, :]


def _in_proj(x2d, norm_w, w_r, tabs, *, tm, rows_per_seq, wqkv_t=None, tabs_t=None):
    rows, d_model = x2d.shape
    n_tiles = rows // tm
    tiles_per_seq = rows_per_seq // tm
    n_seq = rows // rows_per_seq
    kv_transposed = wqkv_t is not None

    def row_spec(n):
        return pl.BlockSpec((tm, n), lambda i: (i, 0))

    def sds(*shape):
        return jax.ShapeDtypeStruct(shape, F32)

    kv_t_spec = pl.BlockSpec((1, W_B, tm), lambda i: (i // tiles_per_seq, 0, i % tiles_per_seq))
    kv_shape = sds(n_seq, W_B, rows_per_seq) if kv_transposed else sds(rows, W_B)
    kv_spec = kv_t_spec if kv_transposed else row_spec(W_B)
    tab_spec = pl.BlockSpec((tm, LANES), lambda i: (i % tiles_per_seq, 0))
    in_specs = [row_spec(d_model), _const_spec((1, d_model)), _const_spec(w_r.shape),
                tab_spec, tab_spec, tab_spec]
    args = [x2d, norm_w.reshape(1, d_model), w_r, *tabs]
    out_shape = [sds(rows, C_CONV), sds(rows, W_V_A), kv_shape, kv_shape, kv_shape,
                 sds(rows, d_model), sds(rows, d_model), sds(rows, LOGIT_PAD)]
    out_specs = [row_spec(C_CONV), row_spec(W_V_A), kv_spec, kv_spec, kv_spec,
                 row_spec(d_model), row_spec(d_model), row_spec(LOGIT_PAD)]
    if kv_transposed:
        tab_t_spec = pl.BlockSpec((ROT_DIM // 2, tm), lambda i: (0, i % tiles_per_seq))
        in_specs += [_const_spec(wqkv_t.shape), tab_t_spec, tab_t_spec]
        args += [wqkv_t, *tabs_t]
        out_shape.append(sds(n_seq, CONV_W - 1, C_CONV))
        out_specs.append(pl.BlockSpec((1, CONV_W - 1, C_CONV),
                                      lambda i: (i // tiles_per_seq, 0, 0)))
    return pl.pallas_call(
        functools.partial(_in_proj_kernel, d_model=d_model, tiles_per_seq=tiles_per_seq,
                          kv_transposed=kv_transposed),
        grid=(n_tiles,),
        in_specs=in_specs,
        out_specs=out_specs,
        out_shape=out_shape,
        compiler_params=pltpu.CompilerParams(dimension_semantics=("arbitrary",),
                                             vmem_limit_bytes=VMEM_LIMIT),
    )(*args)


def _rope_angles(pos):
    half = ROT_DIM // 2
    inv_freq = jnp.exp(jnp.arange(half, dtype=F32) * (-2.0 * math.log(ROPE_THETA) / ROT_DIM))
    ang = pos.astype(F32)[:, None] * inv_freq[None, :]
    return jnp.cos(ang), jnp.sin(ang)


def _rope_tables_t(pos):
    cos, sin = _rope_angles(pos)
    return cos.T, sin.T


def _rope_tables(pos):
    half = ROT_DIM // 2
    cos, sin = _rope_angles(pos)
    n = pos.shape[0]
    ones = jnp.ones((n, HD_B - ROT_DIM), F32)
    zeros_h = jnp.zeros((n, half), F32)
    zeros_r = jnp.zeros((n, HD_B - ROT_DIM), F32)
    cos_h = jnp.concatenate([cos, cos, ones], axis=1)
    slo_h = jnp.concatenate([-sin, zeros_h, zeros_r], axis=1)
    shi_h = jnp.concatenate([zeros_h, sin, zeros_r], axis=1)
    tile = lambda t: jnp.tile(t, (1, HEADS_PER_LANE_TILE))
    return tile(cos_h), tile(slo_h), tile(shi_h)


def _delta_prompt_kernel(alog_ref, dtb_ref, x_ref, z_ref, lg_ref, cw_ref, nw_ref,
                         o_ref, s_ref,
                         hist_s, state_s, u_s, w_s, qg_s, qk_s, kdt_s, gl_s):
    t = pl.program_id(1)
    cs = DELTA_CHUNK
    n_chunks = x_ref.shape[1] // cs
    heads = range(H_A)

    @pl.when(t == 0)
    def _():
        hist_s[...] = jnp.zeros_like(hist_s)
        state_s[...] = jnp.zeros_like(state_s)

    sq = (cs, cs)
    row = _iota(sq, 0)
    col = _iota(sq, 1)
    causal = col <= row
    strict = col < row
    tril_b = _bf(jnp.where(causal, 1.0, 0.0))
    eye_f = jnp.where(col == row, 1.0, 0.0)
    neg_a = [-jnp.exp(jnp.full((1, 1), alog_ref[h], F32)) for h in heads]
    dtb = [dtb_ref[h] for h in heads]
    cw = cw_ref[...]

    def conv_silu(c, r0, c0):
        lanes = slice(c0, c0 + LANES)
        cur = x_ref[0, pl.ds(r0, cs), lanes]
        p0 = pl.multiple_of(jnp.maximum(r0 - SUBLANES, 0), SUBLANES)
        prev = jnp.where(c > 0, x_ref[0, pl.ds(p0, SUBLANES), lanes], hist_s[:, lanes])
        ext = jnp.concatenate([prev, cur], axis=0)
        w = cw[:, lanes]
        y = cur * w[CONV_W - 1:CONV_W]
        for i in range(CONV_W - 1):
            lo = SUBLANES - (CONV_W - 1) + i
            y = y + ext[lo:lo + cs] * w[i:i + 1]
        return _silu(y)

    def l2n(x):
        return x * lax.rsqrt(jnp.sum(x * x, axis=-1, keepdims=True) + EPS)

    def prep(c, carry):
        r0 = pl.multiple_of(c * cs, cs)
        rows = pl.ds(r0, cs)
        q = [l2n(conv_silu(c, r0, h * DK_A)) * (DK_A ** -0.5) for h in heads]
        k = [l2n(conv_silu(c, r0, W_QK_A + h * DK_A)) for h in heads]
        v = [conv_silu(c, r0, 2 * W_QK_A + h * DV_A) for h in heads]
        lg = lg_ref[0, rows, :]
        beta = [_sigmoid(lg[:, h:h + 1]) for h in heads]
        g = [neg_a[h] * _softplus(lg[:, H_A + h:H_A + h + 1] + dtb[h]) for h in heads]
        pieces = []
        for h in heads:
            rem = jnp.broadcast_to(g[h], (cs, LANES))
            for _ in range(3):
                piece = _bf(rem)
                pieces.append(piece)
                rem = rem - piece.astype(F32)
        csum = _mm(tril_b, jnp.concatenate(pieces, axis=1))
        gc = [csum[:, 3 * h * LANES:(3 * h + 1) * LANES]
              + csum[:, (3 * h + 1) * LANES:(3 * h + 2) * LANES]
              + csum[:, (3 * h + 2) * LANES:(3 * h + 3) * LANES] for h in heads]
        decay = [jnp.exp(jnp.where(causal, gc[h] - gc[h].T, NEG_INF)) for h in heads]
        kb = [k[h] * beta[h] for h in heads]
        a = [jnp.where(strict, _mm_nt(_bf(kb[h]), _bf(k[h])) * decay[h], 0.0) for h in heads]
        t_inv = [eye_f - a[h] for h in heads]
        a_pow = [_mm_3pass(a[h], a[h]) for h in heads]
        n_levels = int(math.log2(cs)) - 1
        for lvl in range(n_levels):
            if lvl < n_levels - 1:
                prod = [_mm_3pass(a_pow[h], jnp.concatenate([a_pow[h], t_inv[h]], axis=1))
                        for h in heads]
                a_pow = [prod[h][:, :cs] for h in heads]
                t_inv = [t_inv[h] + prod[h][:, cs:] for h in heads]
            else:
                t_inv = [t_inv[h] + _mm_3pass(a_pow[h], t_inv[h]) for h in heads]
        e_gc = [jnp.exp(gc[h]) for h in heads]
        uw = [_mm(_bf(t_inv[h]),
                  jnp.concatenate([_bf(v[h] * beta[h]), _bf(kb[h] * e_gc[h])], axis=1))
              for h in heads]
        qk = [jnp.where(causal, _mm_nt(_bf(q[h]), _bf(k[h])) * decay[h], 0.0) for h in heads]
        for h in heads:
            g_last = gc[h][cs - 1:cs, :]
            u_s[h, rows, :] = uw[h][:, :DV_A]
            w_s[h, rows, :] = _bf(uw[h][:, DV_A:])
            qk_s[h, rows, :] = _bf(qk[h])
            qg_s[h, rows, :] = _bf(q[h] * e_gc[h])
            kdt_s[h, c] = _bf((k[h] * jnp.exp(g_last - gc[h])).T)
            gl_s[h, c] = jnp.broadcast_to(jnp.exp(g_last), (SUBLANES, LANES))
        return carry

    lax.fori_loop(0, n_chunks, prep, 0)
    hist_s[...] = x_ref[0, n_chunks * cs - SUBLANES:n_chunks * cs, :]

    nw = nw_ref[...]

    def scan(c, carry):
        rows = pl.ds(pl.multiple_of(c * cs, cs), cs)
        state = [state_s[h] for h in heads]
        sb = [_bf(state[h]) for h in heads]
        ws = [_mm(jnp.concatenate([w_s[h, rows, :], qg_s[h, rows, :]], axis=0), sb[h])
              for h in heads]
        vb = [_bf(u_s[h, rows, :] - ws[h][:cs]) for h in heads]
        o = [ws[h][cs:] + _mm(qk_s[h, rows, :], vb[h]) for h in heads]
        new = [state[h] * gl_s[h, c][0:1] + _mm(kdt_s[h, c], vb[h]) for h in heads]
        for h in heads:
            state_s[h] = new[h]
            lanes = slice(h * DV_A, (h + 1) * DV_A)
            o_ref[0, rows, lanes] = _bf(_rms(o[h], nw) * _silu(z_ref[0, rows, lanes]))
        return carry

    lax.fori_loop(0, n_chunks, scan, 0)

    @pl.when(t == pl.num_programs(1) - 1)
    def _():
        s_ref[0] = state_s[...]


def _delta_prompt(qkv, z, lg, conv_w, a_log, dt_bias, norm_w):
    bsz, seq, _ = qkv.shape
    ts = min(DELTA_TILE, seq)
    n_chunks = ts // DELTA_CHUNK
    assert seq % ts == 0 and ts % DELTA_CHUNK == 0 and DK_A == LANES and DV_A == LANES

    def tile_spec(n):
        return pl.BlockSpec((1, ts, n), lambda b, t: (b, t, 0))

    def per_head(shape, dtype):
        return pltpu.VMEM((H_A,) + shape, dtype)

    smem = pl.BlockSpec(memory_space=pltpu.SMEM)
    return pl.pallas_call(
        _delta_prompt_kernel,
        grid=(bsz, seq // ts),
        in_specs=[smem, smem, tile_spec(C_CONV), tile_spec(W_V_A), tile_spec(LOGIT_PAD),
                  _const_spec((CONV_W, C_CONV)), _const_spec((1, DV_A))],
        out_specs=[tile_spec(W_V_A),
                   pl.BlockSpec((1, H_A, DK_A, DV_A), lambda b, t: (b, 0, 0, 0))],
        out_shape=[jax.ShapeDtypeStruct((bsz, seq, W_V_A), BF16),
                   jax.ShapeDtypeStruct((bsz, H_A, DK_A, DV_A), F32)],
        scratch_shapes=[pltpu.VMEM((SUBLANES, C_CONV), F32),
                        per_head((DK_A, DV_A), F32),
                        per_head((ts, DV_A), F32),
                        per_head((ts, DK_A), BF16),
                        per_head((ts, DK_A), BF16),
                        per_head((ts, DELTA_CHUNK), BF16),
                        per_head((n_chunks, DK_A, DELTA_CHUNK), BF16),
                        per_head((n_chunks, SUBLANES, LANES), F32)],
        compiler_params=pltpu.CompilerParams(dimension_semantics=("arbitrary", "arbitrary"),
                                             vmem_limit_bytes=VMEM_LIMIT),
    )(a_log, dt_bias, qkv, z, lg, conv_w, norm_w.reshape(1, DV_A))


def _moba_prompt_kernel(qt_ref, kt_ref, vt_ref, o_ref, kn_s, vt_s, km_s, ch_s, *, n_blocks):
    qi = pl.program_id(2)
    bs = MOBA_BLOCK
    n_pairs = qt_ref.shape[1] // LANES
    heads = range(n_pairs * HEADS_PER_LANE_TILE)
    pair_of = [h // HEADS_PER_LANE_TILE for h in heads]

    @pl.when(qi == 0)
    def _():
        km_s[...] = jnp.zeros_like(km_s)
        ones = jnp.ones((BF16_SUBLANES, bs), BF16)
        for j in range(n_blocks):
            keys = slice(j * bs, (j + 1) * bs)
            for pr in range(n_pairs):
                kj = kt_ref[0, pr * LANES:(pr + 1) * LANES, keys].T
                kn_s[j, pr] = _bf(kj)
                km_s[pr, j:j + 1, :] = jnp.mean(kj, axis=0, keepdims=True)
            for h in heads:
                vt_s[j, h, 0:HD_B, :] = _bf(vt_ref[0, h * HD_B:(h + 1) * HD_B, keys])
                vt_s[j, h, HD_B:, :] = ones

    d_row = _iota((LANES, bs), 0)
    blk = _iota((km_s.shape[1], bs), 0)
    key_le_query = _iota((bs, bs), 0) <= _iota((bs, bs), 1)
    qh = []
    for h in heads:
        hh = h % HEADS_PER_LANE_TILE
        qt = qt_ref[0, pair_of[h] * LANES:(pair_of[h] + 1) * LANES, :]
        qh.append(jnp.where((d_row >= hh * HD_B) & (d_row < (hh + 1) * HD_B), qt, 0.0))
    sc = [jnp.where(blk < qi, _mm(km_s[pair_of[h]], qh[h], HIGHEST), NEG_INF) for h in heads]
    rank = [_block_rank(sc[h], n_blocks, 0) for h in heads]
    for h in heads:
        ch_s[h] = jnp.where(blk < qi, jnp.where(rank[h] < MOBA_TOPK, 1.0, 0.0), 0.0)

    qs = [_bf(qh[h] * (HD_B ** -0.5 * math.log2(math.e))) for h in heads]
    s = [jnp.where(key_le_query, _mm(kn_s[qi, pair_of[h]], qs[h]), NEG_INF)
         for h in heads]
    m = [jnp.max(s[h], axis=0, keepdims=True) for h in heads]
    acc = [_mm(vt_s[qi, h], _bf(jnp.exp2(s[h] - m[h]))) for h in heads]

    def two_past_blocks(jj, carry):
        m, acc = carry
        js = (2 * jj, 2 * jj + 1)
        s = [[jnp.where(ch_s[h, pl.ds(j, 1), :] > 0.0, _mm(kn_s[j, pair_of[h]], qs[h]), NEG_INF)
              for j in js] for h in heads]
        m_new = [jnp.maximum(m[h], jnp.maximum(jnp.max(s[h][0], axis=0, keepdims=True),
                                               jnp.max(s[h][1], axis=0, keepdims=True)))
                 for h in heads]
        acc = [jnp.exp2(m[h] - m_new[h]) * acc[h]
               + _mm(vt_s[js[0], h], _bf(jnp.exp2(s[h][0] - m_new[h])))
               + _mm(vt_s[js[1], h], _bf(jnp.exp2(s[h][1] - m_new[h]))) for h in heads]
        return m_new, acc

    m, acc = lax.fori_loop(0, (qi + 1) // 2, two_past_blocks, (m, acc))
    o_t = jnp.concatenate([acc[h][:HD_B] / acc[h][HD_B:HD_B + 1] for h in heads], axis=0)
    o_ref[0] = _bf(o_t.T)


def _moba_prompt(qt, kt, vt):
    bsz, _, seq = qt.shape
    n_blocks = seq // MOBA_BLOCK
    assert seq % MOBA_BLOCK == 0
    blocks_pad = -(-n_blocks // SUBLANES) * SUBLANES
    n_pairs = MOBA_HEADS_PER_STEP // HEADS_PER_LANE_TILE
    rows = n_pairs * LANES
    q_spec = pl.BlockSpec((1, rows, MOBA_BLOCK), lambda b, p, i: (b, p, i))
    kv_spec = pl.BlockSpec((1, rows, seq), lambda b, p, i: (b, p, 0))
    return pl.pallas_call(
        functools.partial(_moba_prompt_kernel, n_blocks=n_blocks),
        grid=(bsz, H_B // MOBA_HEADS_PER_STEP, n_blocks),
        in_specs=[q_spec, kv_spec, kv_spec],
        out_specs=pl.BlockSpec((1, MOBA_BLOCK, rows), lambda b, p, i: (b, i, p)),
        out_shape=jax.ShapeDtypeStruct((bsz, seq, W_B), BF16),
        scratch_shapes=[pltpu.VMEM((n_blocks, n_pairs, MOBA_BLOCK, LANES), BF16),
                        pltpu.VMEM((n_blocks, MOBA_HEADS_PER_STEP, HD_B + BF16_SUBLANES,
                                    MOBA_BLOCK), BF16),
                        pltpu.VMEM((n_pairs, blocks_pad, LANES), F32),
                        pltpu.VMEM((MOBA_HEADS_PER_STEP, blocks_pad, MOBA_BLOCK), F32)],
        compiler_params=pltpu.CompilerParams(
            dimension_semantics=("arbitrary", "arbitrary", "arbitrary"),
            vmem_limit_bytes=VMEM_LIMIT),
    )(qt, kt, vt)


def _key_mean_groups(pt_ref, cache_ref, km_ref, page_buf, page_sems, stream):
    pages_per_step, pages_per_seq, total, ppb, page_size = stream
    step_i = pl.program_id(0)
    base = step_i * pages_per_step
    blocks_per_group = KMEAN_GROUP // ppb
    lane = _iota((HD_B, LANES), 1)

    def page_copy(p, slot):
        return pltpu.make_async_copy(cache_ref.at[pt_ref[p]], page_buf.at[slot], page_sems.at[slot])

    @pl.when(step_i == 0)
    def _():
        for s in range(KMEAN_SLOTS):
            page_copy(s, s).start()

    @pl.when(base % pages_per_seq == 0)
    def _():
        km_ref[...] = jnp.zeros_like(km_ref)

    def consume_group(g):
        p0 = base + g * KMEAN_GROUP
        slot0 = (g * KMEAN_GROUP) % KMEAN_SLOTS
        for j in range(KMEAN_GROUP):
            page_copy(p0 + j, slot0 + j).wait()
        blk0 = (base % pages_per_seq) // ppb + g * blocks_per_group
        for h in range(H_B):
            acc = jnp.zeros((HD_B, LANES), F32)
            for k in range(blocks_per_group):
                x = page_buf[slot0 + k * ppb, h]
                for r in range(1, ppb):
                    x = x + page_buf[slot0 + k * ppb + r, h]
                mean = jnp.sum(x, axis=-1, keepdims=True) * (1.0 / (ppb * page_size))
                acc = jnp.where(lane == blk0 + k, mean, acc)
            rows = slice(h * HD_B, (h + 1) * HD_B)
            km_ref[0, rows, :] = km_ref[0, rows, :] + acc
        for j in range(KMEAN_GROUP):
            @pl.when(p0 + j + KMEAN_SLOTS < total)
            def _():
                page_copy(p0 + j + KMEAN_SLOTS, slot0 + j).start()

    return [functools.partial(consume_group, g) for g in range(pages_per_step // KMEAN_GROUP)]


def _merge_mlp_kernel(*refs, ff_step, stream):
    if stream is None:
        (x_ref, oa_ref, ob_ref, ga_ref, gb_ref, wpa_ref, wpb_ref, wo_ref, n2_ref, wup_ref, wdn_ref,
         nf_ref, y_ref) = refs
        groups = []
    else:
        (pt_ref, x_ref, oa_ref, ob_ref, ga_ref, gb_ref, wpa_ref, wpb_ref, wo_ref, n2_ref, wup_ref,
         wdn_ref, nf_ref, cache_ref, y_ref, km_ref, page_buf, page_sems) = refs
        groups = _key_mean_groups(pt_ref, cache_ref, km_ref, page_buf, page_sems, stream)
    d_ff = wup_ref.shape[1]
    n_stages = 1 + d_ff // ff_step

    def run_groups(stage):
        for g in groups[stage * len(groups) // n_stages:(stage + 1) * len(groups) // n_stages]:
            g()

    y_a = _mm(oa_ref[...], wpa_ref[...])
    y_b = _mm(ob_ref[...], wpb_ref[...])
    mixed = _sigmoid(ga_ref[...]) * y_a + _sigmoid(gb_ref[...]) * y_b
    h = x_ref[...] + _mm(_bf(mixed), wo_ref[...])
    hn = _bf(_rms(h, n2_ref[...]))
    run_groups(0)
    out = h
    for c in range(d_ff // ff_step):
        u = jnp.maximum(_mm(hn, wup_ref[:, c * ff_step:(c + 1) * ff_step]), 0.0)
        out = out + _mm(_bf(u * u), wdn_ref[c * ff_step:(c + 1) * ff_step, :])
        run_groups(1 + c)
    y_ref[...] = _rms(out, nf_ref[...])


def _merge_mlp(x2d, oa, ob, ga, gb, wpa, wpb, wo, n2, wup, wdn, nf, *, tm, key_stream=None):
    rows, d_model = x2d.shape
    d_ff = wup.shape[1]
    n_tiles = rows // tm

    def row_spec(n):
        return pl.BlockSpec((tm, n), lambda i, *_: (i, 0))

    in_specs = [row_spec(d_model), row_spec(W_V_A), row_spec(W_B), row_spec(d_model),
                row_spec(d_model),
                _const_spec(wpa.shape), _const_spec(wpb.shape), _const_spec(wo.shape),
                _const_spec((1, d_model)), _const_spec(wup.shape), _const_spec(wdn.shape),
                _const_spec((1, d_model))]
    args = [x2d, oa, ob, ga, gb, wpa, wpb, wo, n2.reshape(1, d_model), wup, wdn,
            nf.reshape(1, d_model)]
    y_shape = jax.ShapeDtypeStruct((rows, d_model), F32)
    params = dict(dimension_semantics=("arbitrary",), vmem_limit_bytes=VMEM_LIMIT)
    ff_step = min(d_ff, 1024)
    if key_stream is None:
        return pl.pallas_call(
            functools.partial(_merge_mlp_kernel, ff_step=ff_step, stream=None),
            grid=(n_tiles,), in_specs=in_specs, out_specs=row_spec(d_model), out_shape=y_shape,
            compiler_params=pltpu.CompilerParams(**params),
        )(*args)

    page_ids, cache_t, ppb, pages_per_seq = key_stream
    total = page_ids.shape[0]
    _, _, _, page_size = cache_t.shape
    pages_per_step = total // n_tiles
    assert (pages_per_step * n_tiles == total and pages_per_step % KMEAN_SLOTS == 0
            and pages_per_seq % pages_per_step == 0 and KMEAN_GROUP % ppb == 0
            and KMEAN_SLOTS % KMEAN_GROUP == 0 and pages_per_seq // ppb <= LANES)
    km_shape = jax.ShapeDtypeStruct((total // pages_per_seq, W_B, LANES), F32)
    km_spec = pl.BlockSpec((1, W_B, LANES),
                           lambda i, *_: (i * pages_per_step // pages_per_seq, 0, 0))
    return pl.pallas_call(
        functools.partial(_merge_mlp_kernel, ff_step=ff_step,
                          stream=(pages_per_step, pages_per_seq, total, ppb, page_size)),
        grid_spec=pltpu.PrefetchScalarGridSpec(
            num_scalar_prefetch=1, grid=(n_tiles,),
            in_specs=in_specs + [pl.BlockSpec(memory_space=pl.ANY)],
            out_specs=[row_spec(d_model), km_spec],
            scratch_shapes=[pltpu.VMEM((KMEAN_SLOTS, H_B, HD_B, page_size), F32),
                            pltpu.SemaphoreType.DMA((KMEAN_SLOTS,))]),
        out_shape=[y_shape, km_shape],
        compiler_params=pltpu.CompilerParams(disable_bounds_checks=True, **params),
    )(page_ids, *args, cache_t)


def _delta_step_kernel(alog_ref, dtb_ref, x_ref, z_ref, lg_ref, sc_ref, cw_ref, sd_ref, nw_ref,
                       o_ref, snew_ref, cnew_ref):
    x = x_ref[0]
    hist = sc_ref[0]
    w = cw_ref[...]
    y = x * w[CONV_W - 1:CONV_W]
    for i in range(CONV_W - 1):
        y = y + hist[i:i + 1] * w[i:i + 1]
    y = _silu(y)
    cnew_ref[0, 0:CONV_W - 2, :] = hist[1:CONV_W - 1]
    cnew_ref[0, CONV_W - 2:CONV_W - 1, :] = x
    lg = lg_ref[0]
    zz = z_ref[0]
    nw = nw_ref[...]
    sq = (DK_A, DK_A)
    eye = _iota(sq, 0) == _iota(sq, 1)

    def l2n(t):
        return t * lax.rsqrt(jnp.sum(t * t, axis=-1, keepdims=True) + EPS)

    for h in range(H_A):
        q = l2n(y[:, h * DK_A:(h + 1) * DK_A]) * (DK_A ** -0.5)
        k = l2n(y[:, W_QK_A + h * DK_A:W_QK_A + (h + 1) * DK_A])
        v = y[:, 2 * W_QK_A + h * DV_A:2 * W_QK_A + (h + 1) * DV_A]
        beta = _sigmoid(lg[:, h:h + 1])
        neg_a = -jnp.exp(jnp.full((1, 1), alog_ref[h], F32))
        g = neg_a * _softplus(lg[:, H_A + h:H_A + h + 1] + dtb_ref[h])
        state = sd_ref[0, h] * jnp.exp(g)
        kv = _mm(jnp.broadcast_to(k, (SUBLANES, DK_A)), state, HIGHEST)[0:1]
        dv = (v - kv) * beta
        k_diag = jnp.where(eye, jnp.broadcast_to(k, sq), 0.0)
        state = state + _mm(k_diag, jnp.broadcast_to(dv, (DK_A, DV_A)), HIGHEST)
        snew_ref[0, h] = state
        o = _mm(jnp.broadcast_to(q, (SUBLANES, DK_A)), state, HIGHEST)[0:1]
        gate = _silu(zz[:, h * DV_A:(h + 1) * DV_A])
        o_ref[0, :, h * DV_A:(h + 1) * DV_A] = _bf(_rms(o, nw) * gate)


def _delta_step(qkv, z, lg, state_conv, conv_w, state_delta, a_log, dt_bias, norm_w):
    nb = qkv.shape[0]
    smem = pl.BlockSpec(memory_space=pltpu.SMEM)

    def per_b(shape):
        nd = len(shape)
        return pl.BlockSpec((1,) + shape, lambda b: (b,) + (0,) * nd)

    return pl.pallas_call(
        _delta_step_kernel,
        grid=(nb,),
        in_specs=[smem, smem, per_b((1, C_CONV)), per_b((1, W_V_A)), per_b((1, LOGIT_PAD)),
                  per_b((CONV_W - 1, C_CONV)), _const_spec((CONV_W, C_CONV)),
                  per_b((H_A, DK_A, DV_A)), _const_spec((1, DV_A))],
        out_specs=[per_b((1, W_V_A)), per_b((H_A, DK_A, DV_A)), per_b((CONV_W - 1, C_CONV))],
        out_shape=[jax.ShapeDtypeStruct((nb, 1, W_V_A), BF16),
                   jax.ShapeDtypeStruct((nb, H_A, DK_A, DV_A), F32),
                   jax.ShapeDtypeStruct((nb, CONV_W - 1, C_CONV), F32)],
        compiler_params=pltpu.CompilerParams(dimension_semantics=("arbitrary",),
                                             vmem_limit_bytes=VMEM_LIMIT),
    )(a_log, dt_bias, qkv.reshape(nb, 1, C_CONV), z.reshape(nb, 1, W_V_A),
      lg.reshape(nb, 1, LOGIT_PAD), state_conv, conv_w, state_delta, norm_w.reshape(1, DV_A))


def _topk_kernel(qt_ref, km_ref, idx_ref, *, n_full, kk):
    b = pl.program_id(0)
    qt = qt_ref[...]
    q_col = jnp.sum(jnp.where(_iota(qt.shape, 1) == b, qt, 0.0), axis=-1, keepdims=True)
    prod = km_ref[0] * q_col
    sc = jnp.concatenate(
        [jnp.sum(prod[h * HD_B:(h + 1) * HD_B], axis=0, keepdims=True) for h in range(H_B)],
        axis=0)
    blk = _iota(sc.shape, 1)
    sc = jnp.where(blk < n_full, sc, NEG_INF)
    rank = _block_rank(sc, n_full, 1)
    blk_f = blk.astype(F32)
    out = jnp.zeros(sc.shape, F32)
    for r in range(kk):
        pick = jnp.sum(jnp.where(rank == float(r), blk_f, 0.0), axis=-1, keepdims=True)
        out = jnp.where(blk == r, pick, out)
    idx_ref[0] = out.astype(jnp.int32)


def _topk(q_t, km, *, n_full, kk):
    n_seq = km.shape[0]
    return pl.pallas_call(
        functools.partial(_topk_kernel, n_full=n_full, kk=kk),
        grid=(n_seq,),
        in_specs=[_const_spec(q_t.shape), pl.BlockSpec((1, W_B, LANES), lambda b: (b, 0, 0))],
        out_specs=pl.BlockSpec((1, H_B, LANES), lambda b: (b, 0, 0)),
        out_shape=jax.ShapeDtypeStruct((n_seq, H_B, LANES), jnp.int32),
        compiler_params=pltpu.CompilerParams(dimension_semantics=("arbitrary",),
                                             vmem_limit_bytes=VMEM_LIMIT),
    )(q_t, km)


_PAGE_DMA_PARAMS = pltpu.CompilerParams(dimension_semantics=("arbitrary",),
                                        vmem_limit_bytes=VMEM_LIMIT, disable_bounds_checks=True)


def _moba_sample_kernel(top_ref, pt_ref, q_ref, kn_ref, vn_ref, ck_ref, cv_ref, o_ref,
                        kbuf, vbuf, sems, *, kk, ppb, n_pages, page_size, n_seq):
    b = pl.program_id(0)
    pages_per_head = kk * ppb

    def copies(seq, h, i, page=None):
        slot = seq % 2
        if page is None:
            blk = top_ref[(seq * H_B + h) * kk + i // ppb]
            page = pt_ref[seq * n_pages + blk * ppb + i % ppb]
        keys = pl.ds(i * page_size, page_size)
        return (pltpu.make_async_copy(ck_ref.at[page, h], kbuf.at[slot, h, :, keys],
                                      sems.at[slot, 0, h]),
                pltpu.make_async_copy(cv_ref.at[page, h], vbuf.at[slot, h, :, keys],
                                      sems.at[slot, 1, h]))

    def start_fetch(seq):
        for h in range(H_B):
            for i in range(pages_per_head):
                for c in copies(seq, h, i):
                    c.start()

    @pl.when(b == 0)
    def _():
        start_fetch(b)

    @pl.when(b + 1 < n_seq)
    def _():
        start_fetch(b + 1)

    slot = b % 2
    q = q_ref[0]
    kn = kn_ref[0]
    vn = vn_ref[0]
    for h in range(H_B):
        for i in range(pages_per_head):
            for c in copies(b, h, i, page=0):
                c.wait()
        sl = slice(h * HD_B, (h + 1) * HD_B)
        qh = q[:, sl] * (HD_B ** -0.5)
        s = _mm(_bf(jnp.broadcast_to(qh, (SUBLANES, HD_B))), _bf(kbuf[slot, h]))[0:1]
        s_own = jnp.sum(qh * kn[:, sl], axis=-1, keepdims=True)
        m = jnp.maximum(jnp.max(s, axis=-1, keepdims=True), s_own)
        p = jnp.exp(s - m)
        p_own = jnp.exp(s_own - m)
        l = jnp.sum(p, axis=-1, keepdims=True) + p_own
        pv = _mm_nt(_bf(jnp.broadcast_to(p, (SUBLANES, p.shape[1]))), _bf(vbuf[slot, h]))[0:1]
        o_ref[0, :, sl] = _bf((pv + p_own * vn[:, sl]) / l)


def _moba_sample(top_flat, pt_flat, q3, kn3, vn3, cache_kt, cache_vt, *, kk, ppb):
    nb = q3.shape[0]
    _, _, _, page_size = cache_kt.shape
    keys = kk * ppb * page_size
    row = pl.BlockSpec((1, 1, W_B), lambda b, *_: (b, 0, 0))
    any_spec = pl.BlockSpec(memory_space=pl.ANY)
    return pl.pallas_call(
        functools.partial(_moba_sample_kernel, kk=kk, ppb=ppb,
                          n_pages=pt_flat.shape[0] // nb, page_size=page_size, n_seq=nb),
        grid_spec=pltpu.PrefetchScalarGridSpec(
            num_scalar_prefetch=2,
            grid=(nb,),
            in_specs=[row, row, row, any_spec, any_spec],
            out_specs=row,
            scratch_shapes=[pltpu.VMEM((2, H_B, HD_B, keys), F32),
                            pltpu.VMEM((2, H_B, HD_B, keys), F32),
                            pltpu.SemaphoreType.DMA((2, 2, H_B))]),
        out_shape=jax.ShapeDtypeStruct((nb, 1, W_B), BF16),
        compiler_params=_PAGE_DMA_PARAMS,
    )(top_flat, pt_flat, q3, kn3, vn3, cache_kt, cache_vt)


def _rearranged_w_in(w_in, d_model):
    o_z = C_CONV
    o_lg = o_z + W_V_A
    o_b = o_lg + 2 * H_A
    o_g = o_b + 3 * W_B
    pad = jnp.zeros((d_model, LOGIT_PAD - 2 * H_A), w_in.dtype)
    return _bf(jnp.concatenate(
        [w_in[:, :o_lg], w_in[:, o_b:o_g + 2 * d_model], w_in[:, o_lg:o_b], pad], axis=1))


def kernel(x_prompt, x_sample, cache_k, cache_v, page_table, state_delta, state_conv, norm1_w,
           w_in, conv_w, a_log, dt_bias, delta_norm_w, w_proj_a, w_proj_b, w_out, norm2_w, w_up,
           w_down, norm_f_w):
    depth = w_in.shape[0]
    assert depth == 1, "single-layer trunk"
    bp, sp, d_model = x_prompt.shape
    bs, ss, _ = x_sample.shape
    assert ss == 1, "one new token per sample sequence"
    _, n_pool, page_size, _, _ = cache_k.shape
    n_pages = page_table.shape[1]
    past_len = n_pages * page_size
    ppb = MOBA_BLOCK // page_size
    n_full = past_len // MOBA_BLOCK
    assert n_full * MOBA_BLOCK == past_len, "the sample token starts a fresh MoBA block"
    kk = min(MOBA_TOPK, n_full)
    assert kk >= 1

    w_r = _rearranged_w_in(w_in[0], d_model)
    o_qb = C_CONV + W_V_A + 2 * H_A
    wqkv_t = _bf(w_in[0][:, o_qb:o_qb + 3 * W_B].T)
    wpa, wpb, wo = _bf(w_proj_a[0]), _bf(w_proj_b[0]), _bf(w_out[0])
    wup, wdn = _bf(w_up[0]), _bf(w_down[0])

    tm = 256
    pos_p = jnp.arange(sp, dtype=jnp.int32)
    (qkv, z, qt, kt, vt, ga, gb, lg, conv_p) = _in_proj(
        x_prompt.reshape(bp * sp, d_model), norm1_w[0], w_r, _rope_tables(pos_p),
        tm=tm, rows_per_seq=sp, wqkv_t=wqkv_t, tabs_t=_rope_tables_t(pos_p))
    o_a, s_p = _delta_prompt(qkv.reshape(bp, sp, C_CONV), z.reshape(bp, sp, W_V_A),
                             lg.reshape(bp, sp, LOGIT_PAD), conv_w[0], a_log[0], dt_bias[0],
                             delta_norm_w[0])
    o_b = _moba_prompt(qt, kt, vt)
    cache_kt = cache_k.reshape(n_pool, page_size, H_B, HD_B).transpose(0, 2, 3, 1)
    cache_vt = cache_v.reshape(n_pool, page_size, H_B, HD_B).transpose(0, 2, 3, 1)
    past_pages = page_table[:, :n_full * ppb].reshape(-1)
    y_p, km = _merge_mlp(x_prompt.reshape(bp * sp, d_model), o_a.reshape(bp * sp, W_V_A),
                         o_b.reshape(bp * sp, W_B), ga, gb, wpa, wpb, wo, norm2_w[0], wup, wdn,
                         norm_f_w, tm=tm, key_stream=(past_pages, cache_kt, ppb, n_full * ppb))

    pos_s = jnp.full((bs,), past_len, dtype=jnp.int32)
    (qkv_s, z_s, qr_s, kr_s, vr_s, ga_s, gb_s, lg_s) = _in_proj(
        x_sample.reshape(bs, d_model), norm1_w[0], w_r, _rope_tables(pos_s),
        tm=bs, rows_per_seq=bs)
    o_a_s, s_s, conv_s = _delta_step(qkv_s, z_s, lg_s, state_conv[0], conv_w[0], state_delta[0],
                                     a_log[0], dt_bias[0], delta_norm_w[0])
    top = _topk(qr_s.T, km, n_full=n_full, kk=kk)
    top_flat = top[:, :, :kk].reshape(-1)
    o_b_s = _moba_sample(top_flat, page_table.reshape(-1), qr_s.reshape(bs, 1, W_B),
                         kr_s.reshape(bs, 1, W_B), vr_s.reshape(bs, 1, W_B), cache_kt, cache_vt,
                         kk=kk, ppb=ppb)
    y_s = _merge_mlp(x_sample.reshape(bs, d_model), o_a_s.reshape(bs, W_V_A),
                     o_b_s.reshape(bs, W_B), ga_s, gb_s, wpa, wpb, wo, norm2_w[0], wup, wdn,
                     norm_f_w, tm=bs)

    def kv_out(t):
        return t.reshape(1, bp, H_B, HD_B, sp).transpose(0, 1, 4, 2, 3)

    return (y_p.reshape(bp, sp, d_model), y_s.reshape(bs, ss, d_model),
            kv_out(kt), kv_out(vt),
            s_p.reshape(1, bp, H_A, DK_A, DV_A), conv_p.reshape(1, bp, CONV_W - 1, C_CONV),
            kr_s.reshape(1, bs, ss, H_B, HD_B), vr_s.reshape(1, bs, ss, H_B, HD_B),
            s_s.reshape(1, bs, H_A, DK_A, DV_A), conv_s.reshape(1, bs, CONV_W - 1, C_CONV))
```

```python
import functools
import math

import jax
import jax.numpy as jnp
from jax import lax
from jax.experimental import pallas as pl
from jax.experimental.pallas import tpu as pltpu

F32 = jnp.float32
BF16 = jnp.bfloat16
HIGHEST = lax.Precision.HIGHEST

H_A = 4
DK_A = 128
DV_A = 128
W_QK_A = H_A * DK_A
W_V_A = H_A * DV_A
C_CONV = 2 * W_QK_A + W_V_A
CONV_W = 4
H_B = 8
HD_B = 64
W_B = H_B * HD_B
MOBA_BLOCK = 256
MOBA_TOPK = 3
ROPE_THETA = 500000.0
ROT_DIM = HD_B // 4
EPS = 1e-6

LANES = 128
SUBLANES = 8
BF16_SUBLANES = 16
MOBA_HEADS_PER_STEP = 4
KMEAN_SLOTS = 64
KMEAN_GROUP = 16
CONV_ROWS = 64
HEADS_PER_LANE_TILE = LANES // HD_B
LOGIT_PAD = LANES
DELTA_CHUNK = LANES
DELTA_TILE = 4 * DELTA_CHUNK
VMEM_LIMIT = 56 * 1024 * 1024
NEG_INF = float("-inf")


def _dot(a, b, dims, precision=None):
    return lax.dot_general(a, b, (dims, ((), ())), precision=precision,
                           preferred_element_type=F32)


def _mm(a, b, precision=None):
    return _dot(a, b, ((1,), (0,)), precision)


def _mm_nt(a, b, precision=None):
    return _dot(a, b, ((1,), (1,)), precision)


def _bf(x):
    return x.astype(BF16)


def _sigmoid(x):
    return 1.0 / (1.0 + jnp.exp(-x))


def _silu(x):
    return x * _sigmoid(x)


def _softplus(x):
    return jnp.maximum(x, 0.0) + jnp.log(1.0 + jnp.exp(-jnp.abs(x)))


def _rms(x, w):
    return x * lax.rsqrt(jnp.mean(x * x, axis=-1, keepdims=True) + EPS) * w


def _iota(shape, dim):
    return lax.broadcasted_iota(jnp.int32, shape, dim)


def _const_spec(shape):
    nd = len(shape)
    return pl.BlockSpec(shape, lambda *_: (0,) * nd, pipeline_mode=pl.Buffered(1))


def _block_rank(sc, n_blocks, axis):
    blk = _iota(sc.shape, axis)
    rank = jnp.zeros(sc.shape, F32)
    for j in range(n_blocks):
        cj = sc[j:j + 1, :] if axis == 0 else sc[:, j:j + 1]
        first_on_tie = jnp.where(blk > j, 1.0, 0.0)
        rank = rank + jnp.where(cj > sc, 1.0, jnp.where(cj == sc, first_on_tie, 0.0))
    return rank


def _split_bf16(x):
    hi = _bf(x)
    return hi, _bf(x - hi.astype(F32))


def _mm_3pass(a, b):
    a_hi, a_lo = _split_bf16(a)
    b_hi, b_lo = _split_bf16(b)
    return _mm(jnp.concatenate([a_hi, a_lo, a_hi], axis=1),
               jnp.concatenate([b_hi, b_hi, b_lo], axis=0))


def _in_proj_kernel(x_ref, nw_ref, w_ref, cos_ref, slo_ref, shi_ref, *rest,
                    d_model, tiles_per_seq, kv_transposed):
    if kv_transposed:
        (wqkv_t_ref, cos_t_ref, sin_t_ref, cw_ref,
         qkv_ref, z_ref, qb_ref, kb_ref, vb_ref, ga_ref, gb_ref, lg_ref, conv_ref, hist_s) = rest
    else:
        qkv_ref, z_ref, qb_ref, kb_ref, vb_ref, ga_ref, gb_ref, lg_ref = rest
    step_i = pl.program_id(0)
    xb = _bf(_rms(x_ref[...], nw_ref[...]))
    tm = xb.shape[0]
    half = ROT_DIM // 2

    def conv_act_in_place():
        last = qkv_ref[tm - SUBLANES:tm, :]
        conv_ref[0] = last[SUBLANES - (CONV_W - 1):, :]
        for j in range(C_CONV // LANES):
            lanes = slice(j * LANES, (j + 1) * LANES)
            w = cw_ref[:, lanes]
            for rb in reversed(range(tm // CONV_ROWS)):
                r0 = rb * CONV_ROWS
                y = qkv_ref[r0:r0 + CONV_ROWS, lanes]
                prev = hist_s[:, lanes] if rb == 0 else qkv_ref[r0 - SUBLANES:r0, lanes]
                ext = jnp.concatenate([prev, y], axis=0)
                acc = y * w[CONV_W - 1:CONV_W]
                for i in range(CONV_W - 1):
                    lo = SUBLANES - (CONV_W - 1) + i
                    acc = acc + ext[lo:lo + CONV_ROWS] * w[i:i + 1]
                act = _silu(acc)
                if lanes.start < 2 * W_QK_A:
                    act = act * lax.rsqrt(jnp.sum(act * act, axis=-1, keepdims=True) + EPS)
                if lanes.start < W_QK_A:
                    act = act * (DK_A ** -0.5)
                qkv_ref[r0:r0 + CONV_ROWS, lanes] = act
        hist_s[...] = last

    def cols(c0, n):
        return _mm(xb, w_ref[:, c0:c0 + n])

    def rope_store(dst_ref, c0):
        cos, slo, shi = cos_ref[...], slo_ref[...], shi_ref[...]
        for c in range(W_B // LANES):
            y = cols(c0 + c * LANES, LANES)
            up = pltpu.roll(y, LANES - half, 1)
            dn = pltpu.roll(y, half, 1)
            dst_ref[:, c * LANES:(c + 1) * LANES] = y * cos + up * slo + dn * shi

    def rope_store_t(dst_ref, r0):
        cos_t, sin_t = cos_t_ref[...], sin_t_ref[...]
        y_t = _mm_nt(wqkv_t_ref[r0:r0 + W_B, :], xb)
        for h in range(H_B):
            r = h * HD_B
            x1, x2 = y_t[r:r + half], y_t[r + half:r + ROT_DIM]
            dst_ref[0, r:r + half, :] = x1 * cos_t - x2 * sin_t
            dst_ref[0, r + half:r + ROT_DIM, :] = x2 * cos_t + x1 * sin_t
            dst_ref[0, r + ROT_DIM:r + HD_B, :] = y_t[r + ROT_DIM:r + HD_B]

    if kv_transposed:
        @pl.when(step_i % tiles_per_seq == 0)
        def _():
            hist_s[...] = jnp.zeros_like(hist_s)

    step = 512
    c0 = 0
    for c in range(C_CONV // step):
        qkv_ref[:, c0:c0 + step] = cols(c0, step)
        c0 += step
    if kv_transposed:
        conv_act_in_place()
    z_ref[...] = cols(c0, W_V_A)
    c0 += W_V_A
    if kv_transposed:
        rope_store_t(qb_ref, 0)
        rope_store_t(kb_ref, W_B)
        vb_ref[0] = _mm_nt(wqkv_t_ref[2 * W_B:3 * W_B, :], xb)
    else:
        rope_store(qb_ref, c0)
        rope_store(kb_ref, c0 + W_B)
        vb_ref[...] = cols(c0 + 2 * W_B, W_B)
    c0 += 3 * W_B
    for g_ref in (ga_ref, gb_ref):
        for c in range(d_model // step):
            g_ref[:, c * step:(c + 1) * step] = cols(c0, step)
            c0 += step
    lg_ref[...] = cols(c0, LOGIT_PAD)


def _in_proj(x2d, norm_w, w_r, tabs, *, tm, rows_per_seq, wqkv_t=None, tabs_t=None, conv_w=None):
    rows, d_model = x2d.shape
    n_tiles = rows // tm
    tiles_per_seq = rows_per_seq // tm
    n_seq = rows // rows_per_seq
    kv_transposed = wqkv_t is not None

    def row_spec(n):
        return pl.BlockSpec((tm, n), lambda i: (i, 0))

    def sds(*shape):
        return jax.ShapeDtypeStruct(shape, F32)

    kv_t_spec = pl.BlockSpec((1, W_B, tm), lambda i: (i // tiles_per_seq, 0, i % tiles_per_seq))
    kv_shape = sds(n_seq, W_B, rows_per_seq) if kv_transposed else sds(rows, W_B)
    kv_spec = kv_t_spec if kv_transposed else row_spec(W_B)
    tab_spec = pl.BlockSpec((tm, LANES), lambda i: (i % tiles_per_seq, 0))
    in_specs = [row_spec(d_model), _const_spec((1, d_model)), _const_spec(w_r.shape),
                tab_spec, tab_spec, tab_spec]
    args = [x2d, norm_w.reshape(1, d_model), w_r, *tabs]
    out_shape = [sds(rows, C_CONV), sds(rows, W_V_A), kv_shape, kv_shape, kv_shape,
                 sds(rows, d_model), sds(rows, d_model), sds(rows, LOGIT_PAD)]
    out_specs = [row_spec(C_CONV), row_spec(W_V_A), kv_spec, kv_spec, kv_spec,
                 row_spec(d_model), row_spec(d_model), row_spec(LOGIT_PAD)]
    if kv_transposed:
        tab_t_spec = pl.BlockSpec((ROT_DIM // 2, tm), lambda i: (0, i % tiles_per_seq))
        in_specs += [_const_spec(wqkv_t.shape), tab_t_spec, tab_t_spec, _const_spec(conv_w.shape)]
        args += [wqkv_t, *tabs_t, conv_w]
        out_shape.append(sds(n_seq, CONV_W - 1, C_CONV))
        out_specs.append(pl.BlockSpec((1, CONV_W - 1, C_CONV),
                                      lambda i: (i // tiles_per_seq, 0, 0)))
        assert tm >= SUBLANES
        scratch = [pltpu.VMEM((SUBLANES, C_CONV), F32)]
    else:
        scratch = []
    return pl.pallas_call(
        functools.partial(_in_proj_kernel, d_model=d_model, tiles_per_seq=tiles_per_seq,
                          kv_transposed=kv_transposed),
        grid=(n_tiles,),
        in_specs=in_specs,
        out_specs=out_specs,
        out_shape=out_shape,
        scratch_shapes=scratch,
        compiler_params=pltpu.CompilerParams(dimension_semantics=("arbitrary",),
                                             vmem_limit_bytes=VMEM_LIMIT),
    )(*args)


def _rope_angles(pos):
    half = ROT_DIM // 2
    inv_freq = jnp.exp(jnp.arange(half, dtype=F32) * (-2.0 * math.log(ROPE_THETA) / ROT_DIM))
    ang = pos.astype(F32)[:, None] * inv_freq[None, :]
    return jnp.cos(ang), jnp.sin(ang)


def _rope_tables_t(pos):
    cos, sin = _rope_angles(pos)
    return cos.T, sin.T


def _rope_tables(pos):
    half = ROT_DIM // 2
    cos, sin = _rope_angles(pos)
    n = pos.shape[0]
    ones = jnp.ones((n, HD_B - ROT_DIM), F32)
    zeros_h = jnp.zeros((n, half), F32)
    zeros_r = jnp.zeros((n, HD_B - ROT_DIM), F32)
    cos_h = jnp.concatenate([cos, cos, ones], axis=1)
    slo_h = jnp.concatenate([-sin, zeros_h, zeros_r], axis=1)
    shi_h = jnp.concatenate([zeros_h, sin, zeros_r], axis=1)
    tile = lambda t: jnp.tile(t, (1, HEADS_PER_LANE_TILE))
    return tile(cos_h), tile(slo_h), tile(shi_h)


def _delta_prompt_kernel(alog_ref, dtb_ref, x_ref, z_ref, lg_ref, nw_ref,
                         o_ref, s_ref,
                         state_s, u_s, w_s, qg_s, qk_s, kdt_s, gl_s):
    t = pl.program_id(1)
    cs = DELTA_CHUNK
    n_chunks = x_ref.shape[1] // cs
    heads = range(H_A)

    @pl.when(t == 0)
    def _():
        state_s[...] = jnp.zeros_like(state_s)

    sq = (cs, cs)
    row = _iota(sq, 0)
    col = _iota(sq, 1)
    causal = col <= row
    strict = col < row
    tril_b = _bf(jnp.where(causal, 1.0, 0.0))
    eye_f = jnp.where(col == row, 1.0, 0.0)
    lane_row = _iota((1, LANES), 1)
    alog_row = jnp.zeros((1, LANES), F32)
    dtb_row = jnp.zeros((1, LANES), F32)
    for h in heads:
        alog_row = jnp.where(lane_row == H_A + h, alog_ref[h], alog_row)
        dtb_row = jnp.where(lane_row == H_A + h, dtb_ref[h], dtb_row)
    neg_a_row = -jnp.exp(alog_row)

    def prep(c, carry):
        rows = pl.ds(pl.multiple_of(c * cs, cs), cs)
        q = [x_ref[0, rows, h * DK_A:(h + 1) * DK_A] for h in heads]
        k = [x_ref[0, rows, W_QK_A + h * DK_A:W_QK_A + (h + 1) * DK_A] for h in heads]
        v = [x_ref[0, rows, 2 * W_QK_A + h * DV_A:2 * W_QK_A + (h + 1) * DV_A] for h in heads]
        lg = lg_ref[0, rows, :]
        sig = _sigmoid(lg)
        beta = [sig[:, h:h + 1] for h in heads]
        g_all = neg_a_row * _softplus(lg + dtb_row)
        pieces = []
        rem = g_all
        for _ in range(3):
            piece = _bf(rem)
            pieces.append(piece)
            rem = rem - piece.astype(F32)
        csum = _mm(tril_b, jnp.concatenate(pieces, axis=1))
        gc_all = csum[:, :LANES] + csum[:, LANES:2 * LANES] + csum[:, 2 * LANES:]
        gc = [jnp.broadcast_to(gc_all[:, H_A + h:H_A + h + 1], (cs, LANES)) for h in heads]
        decay = [jnp.exp(jnp.where(causal, gc[h] - gc[h].T, NEG_INF)) for h in heads]
        kb = [k[h] * beta[h] for h in heads]
        a = [jnp.where(strict, _mm_nt(_bf(kb[h]), _bf(k[h])) * decay[h], 0.0) for h in heads]
        t_inv = [eye_f - a[h] for h in heads]
        a_pow = [_mm_3pass(a[h], a[h]) for h in heads]
        n_levels = int(math.log2(cs)) - 1
        for lvl in range(n_levels):
            if lvl < n_levels - 1:
                prod = [_mm_3pass(a_pow[h], jnp.concatenate([a_pow[h], t_inv[h]], axis=1))
                        for h in heads]
                a_pow = [prod[h][:, :cs] for h in heads]
                t_inv = [t_inv[h] + prod[h][:, cs:] for h in heads]
            else:
                t_inv = [t_inv[h] + _mm_3pass(a_pow[h], t_inv[h]) for h in heads]
        e_gc = [jnp.exp(gc[h]) for h in heads]
        uw = [_mm(_bf(t_inv[h]),
                  jnp.concatenate([_bf(v[h] * beta[h]), _bf(kb[h] * e_gc[h])], axis=1))
              for h in heads]
        qk = [jnp.where(causal, _mm_nt(_bf(q[h]), _bf(k[h])) * decay[h], 0.0) for h in heads]
        for h in heads:
            g_last = gc[h][cs - 1:cs, :]
            u_s[h, rows, :] = uw[h][:, :DV_A]
            w_s[h, rows, :] = _bf(uw[h][:, DV_A:])
            qk_s[h, rows, :] = _bf(qk[h])
            qg_s[h, rows, :] = _bf(q[h] * e_gc[h])
            kdt_s[h, c] = _bf((k[h] * jnp.exp(g_last - gc[h])).T)
            gl_s[h, c] = jnp.broadcast_to(jnp.exp(g_last), (SUBLANES, LANES))
        return carry

    lax.fori_loop(0, n_chunks, prep, 0)

    nw = nw_ref[...]

    def scan(c, carry):
        rows = pl.ds(pl.multiple_of(c * cs, cs), cs)
        state = [state_s[h] for h in heads]
        sb = [_bf(state[h]) for h in heads]
        ws = [_mm(jnp.concatenate([w_s[h, rows, :], qg_s[h, rows, :]], axis=0), sb[h])
              for h in heads]
        vb = [_bf(u_s[h, rows, :] - ws[h][:cs]) for h in heads]
        o = [ws[h][cs:] + _mm(qk_s[h, rows, :], vb[h]) for h in heads]
        new = [state[h] * gl_s[h, c][0:1] + _mm(kdt_s[h, c], vb[h]) for h in heads]
        for h in heads:
            state_s[h] = new[h]
            lanes = slice(h * DV_A, (h + 1) * DV_A)
            o_ref[0, rows, lanes] = _bf(_rms(o[h], nw) * _silu(z_ref[0, rows, lanes]))
        return carry

    lax.fori_loop(0, n_chunks, scan, 0)

    @pl.when(t == pl.num_programs(1) - 1)
    def _():
        s_ref[0] = state_s[...]


def _delta_prompt(qkv, z, lg, a_log, dt_bias, norm_w):
    bsz, seq, _ = qkv.shape
    ts = min(DELTA_TILE, seq)
    n_chunks = ts // DELTA_CHUNK
    assert seq % ts == 0 and ts % DELTA_CHUNK == 0 and DK_A == LANES and DV_A == LANES

    def tile_spec(n):
        return pl.BlockSpec((1, ts, n), lambda b, t: (b, t, 0))

    def per_head(shape, dtype):
        return pltpu.VMEM((H_A,) + shape, dtype)

    smem = pl.BlockSpec(memory_space=pltpu.SMEM)
    return pl.pallas_call(
        _delta_prompt_kernel,
        grid=(bsz, seq // ts),
        in_specs=[smem, smem, tile_spec(C_CONV), tile_spec(W_V_A), tile_spec(LOGIT_PAD),
                  _const_spec((1, DV_A))],
        out_specs=[tile_spec(W_V_A),
                   pl.BlockSpec((1, H_A, DK_A, DV_A), lambda b, t: (b, 0, 0, 0))],
        out_shape=[jax.ShapeDtypeStruct((bsz, seq, W_V_A), BF16),
                   jax.ShapeDtypeStruct((bsz, H_A, DK_A, DV_A), F32)],
        scratch_shapes=[per_head((DK_A, DV_A), F32),
                        per_head((ts, DV_A), F32),
                        per_head((ts, DK_A), BF16),
                        per_head((ts, DK_A), BF16),
                        per_head((ts, DELTA_CHUNK), BF16),
                        per_head((n_chunks, DK_A, DELTA_CHUNK), BF16),
                        per_head((n_chunks, SUBLANES, LANES), F32)],
        compiler_params=pltpu.CompilerParams(dimension_semantics=("arbitrary", "arbitrary"),
                                             vmem_limit_bytes=VMEM_LIMIT),
    )(a_log, dt_bias, qkv, z, lg, norm_w.reshape(1, DV_A))


def _moba_prompt_kernel(qt_ref, kt_ref, vt_ref, o_ref, kn_s, vt_s, km_s, ch_s, *, n_blocks):
    qi = pl.program_id(2)
    bs = MOBA_BLOCK
    n_pairs = qt_ref.shape[1] // LANES
    heads = range(n_pairs * HEADS_PER_LANE_TILE)
    pair_of = [h // HEADS_PER_LANE_TILE for h in heads]

    @pl.when(qi == 0)
    def _():
        km_s[...] = jnp.zeros_like(km_s)
        ones = jnp.ones((BF16_SUBLANES, bs), BF16)
        for j in range(n_blocks):
            keys = slice(j * bs, (j + 1) * bs)
            for pr in range(n_pairs):
                kj = kt_ref[0, pr * LANES:(pr + 1) * LANES, keys].T
                kn_s[j, pr] = _bf(kj)
                km_s[pr, j:j + 1, :] = jnp.mean(kj, axis=0, keepdims=True)
            for h in heads:
                vt_s[j, h, 0:HD_B, :] = _bf(vt_ref[0, h * HD_B:(h + 1) * HD_B, keys])
                vt_s[j, h, HD_B:, :] = ones

    d_row = _iota((LANES, bs), 0)
    blk = _iota((km_s.shape[1], bs), 0)
    key_le_query = _iota((bs, bs), 0) <= _iota((bs, bs), 1)
    qh = []
    for h in heads:
        hh = h % HEADS_PER_LANE_TILE
        qt = qt_ref[0, pair_of[h] * LANES:(pair_of[h] + 1) * LANES, :]
        qh.append(jnp.where((d_row >= hh * HD_B) & (d_row < (hh + 1) * HD_B), qt, 0.0))
    sc = [jnp.where(blk < qi, _mm(km_s[pair_of[h]], qh[h], HIGHEST), NEG_INF) for h in heads]
    rank = [_block_rank(sc[h], n_blocks, 0) for h in heads]
    for h in heads:
        ch_s[h] = jnp.where(blk < qi, jnp.where(rank[h] < MOBA_TOPK, 1.0, 0.0), 0.0)

    qs = [_bf(qh[h] * (HD_B ** -0.5 * math.log2(math.e))) for h in heads]
    s = [jnp.where(key_le_query, _mm(kn_s[qi, pair_of[h]], qs[h]), NEG_INF)
         for h in heads]
    m = [jnp.max(s[h], axis=0, keepdims=True) for h in heads]
    acc = [_mm(vt_s[qi, h], _bf(jnp.exp2(s[h] - m[h]))) for h in heads]

    def two_past_blocks(jj, carry):
        m, acc = carry
        js = (2 * jj, 2 * jj + 1)
        s = [[jnp.where(ch_s[h, pl.ds(j, 1), :] > 0.0, _mm(kn_s[j, pair_of[h]], qs[h]), NEG_INF)
              for j in js] for h in heads]
        m_new = [jnp.maximum(m[h], jnp.maximum(jnp.max(s[h][0], axis=0, keepdims=True),
                                               jnp.max(s[h][1], axis=0, keepdims=True)))
                 for h in heads]
        acc = [jnp.exp2(m[h] - m_new[h]) * acc[h]
               + _mm(vt_s[js[0], h], _bf(jnp.exp2(s[h][0] - m_new[h])))
               + _mm(vt_s[js[1], h], _bf(jnp.exp2(s[h][1] - m_new[h]))) for h in heads]
        return m_new, acc

    m, acc = lax.fori_loop(0, (qi + 1) // 2, two_past_blocks, (m, acc))
    o_t = jnp.concatenate([acc[h][:HD_B] / acc[h][HD_B:HD_B + 1] for h in heads], axis=0)
    o_ref[0] = _bf(o_t.T)


def _moba_prompt(qt, kt, vt):
    bsz, _, seq = qt.shape
    n_blocks = seq // MOBA_BLOCK
    assert seq % MOBA_BLOCK == 0
    blocks_pad = -(-n_blocks // SUBLANES) * SUBLANES
    n_pairs = MOBA_HEADS_PER_STEP // HEADS_PER_LANE_TILE
    rows = n_pairs * LANES
    q_spec = pl.BlockSpec((1, rows, MOBA_BLOCK), lambda b, p, i: (b, p, i))
    kv_spec = pl.BlockSpec((1, rows, seq), lambda b, p, i: (b, p, 0))
    return pl.pallas_call(
        functools.partial(_moba_prompt_kernel, n_blocks=n_blocks),
        grid=(bsz, H_B // MOBA_HEADS_PER_STEP, n_blocks),
        in_specs=[q_spec, kv_spec, kv_spec],
        out_specs=pl.BlockSpec((1, MOBA_BLOCK, rows), lambda b, p, i: (b, i, p)),
        out_shape=jax.ShapeDtypeStruct((bsz, seq, W_B), BF16),
        scratch_shapes=[pltpu.VMEM((n_blocks, n_pairs, MOBA_BLOCK, LANES), BF16),
                        pltpu.VMEM((n_blocks, MOBA_HEADS_PER_STEP, HD_B + BF16_SUBLANES,
                                    MOBA_BLOCK), BF16),
                        pltpu.VMEM((n_pairs, blocks_pad, LANES), F32),
                        pltpu.VMEM((MOBA_HEADS_PER_STEP, blocks_pad, MOBA_BLOCK), F32)],
        compiler_params=pltpu.CompilerParams(
            dimension_semantics=("arbitrary", "arbitrary", "arbitrary"),
            vmem_limit_bytes=VMEM_LIMIT),
    )(qt, kt, vt)


def _key_mean_groups(pt_ref, cache_ref, km_ref, page_buf, page_sems, stream):
    pages_per_step, pages_per_seq, total, ppb, page_size = stream
    step_i = pl.program_id(0)
    base = step_i * pages_per_step
    blocks_per_group = KMEAN_GROUP // ppb
    lane = _iota((HD_B, LANES), 1)

    def page_copy(p, slot):
        return pltpu.make_async_copy(cache_ref.at[pt_ref[p]], page_buf.at[slot], page_sems.at[slot])

    @pl.when(step_i == 0)
    def _():
        for s in range(KMEAN_SLOTS):
            page_copy(s, s).start()

    @pl.when(base % pages_per_seq == 0)
    def _():
        km_ref[...] = jnp.zeros_like(km_ref)

    def consume_group(g):
        p0 = base + g * KMEAN_GROUP
        slot0 = (g * KMEAN_GROUP) % KMEAN_SLOTS
        for j in range(KMEAN_GROUP):
            page_copy(p0 + j, slot0 + j).wait()
        blk0 = (base % pages_per_seq) // ppb + g * blocks_per_group
        for h in range(H_B):
            acc = jnp.zeros((HD_B, LANES), F32)
            for k in range(blocks_per_group):
                x = page_buf[slot0 + k * ppb, h]
                for r in range(1, ppb):
                    x = x + page_buf[slot0 + k * ppb + r, h]
                mean = jnp.sum(x, axis=-1, keepdims=True) * (1.0 / (ppb * page_size))
                acc = jnp.where(lane == blk0 + k, mean, acc)
            rows = slice(h * HD_B, (h + 1) * HD_B)
            km_ref[0, rows, :] = km_ref[0, rows, :] + acc
        for j in range(KMEAN_GROUP):
            @pl.when(p0 + j + KMEAN_SLOTS < total)
            def _():
                page_copy(p0 + j + KMEAN_SLOTS, slot0 + j).start()

    return [functools.partial(consume_group, g) for g in range(pages_per_step // KMEAN_GROUP)]


def _merge_mlp_kernel(*refs, ff_step, stream):
    if stream is None:
        (x_ref, oa_ref, ob_ref, ga_ref, gb_ref, wpa_ref, wpb_ref, wo_ref, n2_ref, wup_ref, wdn_ref,
         nf_ref, y_ref) = refs
        groups = []
    else:
        (pt_ref, x_ref, oa_ref, ob_ref, ga_ref, gb_ref, wpa_ref, wpb_ref, wo_ref, n2_ref, wup_ref,
         wdn_ref, nf_ref, cache_ref, y_ref, km_ref, page_buf, page_sems) = refs
        groups = _key_mean_groups(pt_ref, cache_ref, km_ref, page_buf, page_sems, stream)
    d_ff = wup_ref.shape[1]
    n_stages = 1 + d_ff // ff_step

    def run_groups(stage):
        for g in groups[stage * len(groups) // n_stages:(stage + 1) * len(groups) // n_stages]:
            g()

    y_a = _mm(oa_ref[...], wpa_ref[...])
    y_b = _mm(ob_ref[...], wpb_ref[...])
    mixed = _sigmoid(ga_ref[...]) * y_a + _sigmoid(gb_ref[...]) * y_b
    h = x_ref[...] + _mm(_bf(mixed), wo_ref[...])
    hn = _bf(_rms(h, n2_ref[...]))
    run_groups(0)
    out = h
    for c in range(d_ff // ff_step):
        u = jnp.maximum(_mm(hn, wup_ref[:, c * ff_step:(c + 1) * ff_step]), 0.0)
        out = out + _mm(_bf(u * u), wdn_ref[c * ff_step:(c + 1) * ff_step, :])
        run_groups(1 + c)
    y_ref[...] = _rms(out, nf_ref[...])


def _merge_mlp(x2d, oa, ob, ga, gb, wpa, wpb, wo, n2, wup, wdn, nf, *, tm, key_stream=None):
    rows, d_model = x2d.shape
    d_ff = wup.shape[1]
    n_tiles = rows // tm

    def row_spec(n):
        return pl.BlockSpec((tm, n), lambda i, *_: (i, 0))

    in_specs = [row_spec(d_model), row_spec(W_V_A), row_spec(W_B), row_spec(d_model),
                row_spec(d_model),
                _const_spec(wpa.shape), _const_spec(wpb.shape), _const_spec(wo.shape),
                _const_spec((1, d_model)), _const_spec(wup.shape), _const_spec(wdn.shape),
                _const_spec((1, d_model))]
    args = [x2d, oa, ob, ga, gb, wpa, wpb, wo, n2.reshape(1, d_model), wup, wdn,
            nf.reshape(1, d_model)]
    y_shape = jax.ShapeDtypeStruct((rows, d_model), F32)
    params = dict(dimension_semantics=("arbitrary",), vmem_limit_bytes=VMEM_LIMIT)
    ff_step = min(d_ff, 1024)
    if key_stream is None:
        return pl.pallas_call(
            functools.partial(_merge_mlp_kernel, ff_step=ff_step, stream=None),
            grid=(n_tiles,), in_specs=in_specs, out_specs=row_spec(d_model), out_shape=y_shape,
            compiler_params=pltpu.CompilerParams(**params),
        )(*args)

    page_ids, cache_t, ppb, pages_per_seq = key_stream
    total = page_ids.shape[0]
    _, _, _, page_size = cache_t.shape
    pages_per_step = total // n_tiles
    assert (pages_per_step * n_tiles == total and pages_per_step % KMEAN_SLOTS == 0
            and pages_per_seq % pages_per_step == 0 and KMEAN_GROUP % ppb == 0
            and KMEAN_SLOTS % KMEAN_GROUP == 0 and pages_per_seq // ppb <= LANES)
    km_shape = jax.ShapeDtypeStruct((total // pages_per_seq, W_B, LANES), F32)
    km_spec = pl.BlockSpec((1, W_B, LANES),
                           lambda i, *_: (i * pages_per_step // pages_per_seq, 0, 0))
    return pl.pallas_call(
        functools.partial(_merge_mlp_kernel, ff_step=ff_step,
                          stream=(pages_per_step, pages_per_seq, total, ppb, page_size)),
        grid_spec=pltpu.PrefetchScalarGridSpec(
            num_scalar_prefetch=1, grid=(n_tiles,),
            in_specs=in_specs + [pl.BlockSpec(memory_space=pl.ANY)],
            out_specs=[row_spec(d_model), km_spec],
            scratch_shapes=[pltpu.VMEM((KMEAN_SLOTS, H_B, HD_B, page_size), F32),
                            pltpu.SemaphoreType.DMA((KMEAN_SLOTS,))]),
        out_shape=[y_shape, km_shape],
        compiler_params=pltpu.CompilerParams(disable_bounds_checks=True, **params),
    )(page_ids, *args, cache_t)


def _delta_step_kernel(alog_ref, dtb_ref, x_ref, z_ref, lg_ref, sc_ref, cw_ref, sd_ref, nw_ref,
                       o_ref, snew_ref, cnew_ref):
    x = x_ref[0]
    hist = sc_ref[0]
    w = cw_ref[...]
    y = x * w[CONV_W - 1:CONV_W]
    for i in range(CONV_W - 1):
        y = y + hist[i:i + 1] * w[i:i + 1]
    y = _silu(y)
    cnew_ref[0, 0:CONV_W - 2, :] = hist[1:CONV_W - 1]
    cnew_ref[0, CONV_W - 2:CONV_W - 1, :] = x
    lg = lg_ref[0]
    zz = z_ref[0]
    nw = nw_ref[...]
    sq = (DK_A, DK_A)
    eye = _iota(sq, 0) == _iota(sq, 1)

    def l2n(t):
        return t * lax.rsqrt(jnp.sum(t * t, axis=-1, keepdims=True) + EPS)

    heads = range(H_A)
    q = [l2n(y[:, h * DK_A:(h + 1) * DK_A]) * (DK_A ** -0.5) for h in heads]
    k = [l2n(y[:, W_QK_A + h * DK_A:W_QK_A + (h + 1) * DK_A]) for h in heads]
    v = [y[:, 2 * W_QK_A + h * DV_A:2 * W_QK_A + (h + 1) * DV_A] for h in heads]
    beta = [_sigmoid(lg[:, h:h + 1]) for h in heads]
    g = [-jnp.exp(jnp.full((1, 1), alog_ref[h], F32))
         * _softplus(lg[:, H_A + h:H_A + h + 1] + dtb_ref[h]) for h in heads]
    state = [sd_ref[0, h] * jnp.exp(g[h]) for h in heads]
    kv = [_mm(jnp.broadcast_to(k[h], (SUBLANES, DK_A)), state[h], HIGHEST)[0:1] for h in heads]
    dv = [(v[h] - kv[h]) * beta[h] for h in heads]
    k_diag = [jnp.where(eye, jnp.broadcast_to(k[h], sq), 0.0) for h in heads]
    state = [state[h] + _mm(k_diag[h], jnp.broadcast_to(dv[h], (DK_A, DV_A)), HIGHEST)
             for h in heads]
    o = [_mm(jnp.broadcast_to(q[h], (SUBLANES, DK_A)), state[h], HIGHEST)[0:1] for h in heads]
    for h in heads:
        snew_ref[0, h] = state[h]
        gate = _silu(zz[:, h * DV_A:(h + 1) * DV_A])
        o_ref[0, :, h * DV_A:(h + 1) * DV_A] = _bf(_rms(o[h], nw) * gate)


def _delta_step(qkv, z, lg, state_conv, conv_w, state_delta, a_log, dt_bias, norm_w):
    nb = qkv.shape[0]
    smem = pl.BlockSpec(memory_space=pltpu.SMEM)

    def per_b(shape):
        nd = len(shape)
        return pl.BlockSpec((1,) + shape, lambda b: (b,) + (0,) * nd)

    return pl.pallas_call(
        _delta_step_kernel,
        grid=(nb,),
        in_specs=[smem, smem, per_b((1, C_CONV)), per_b((1, W_V_A)), per_b((1, LOGIT_PAD)),
                  per_b((CONV_W - 1, C_CONV)), _const_spec((CONV_W, C_CONV)),
                  per_b((H_A, DK_A, DV_A)), _const_spec((1, DV_A))],
        out_specs=[per_b((1, W_V_A)), per_b((H_A, DK_A, DV_A)), per_b((CONV_W - 1, C_CONV))],
        out_shape=[jax.ShapeDtypeStruct((nb, 1, W_V_A), BF16),
                   jax.ShapeDtypeStruct((nb, H_A, DK_A, DV_A), F32),
                   jax.ShapeDtypeStruct((nb, CONV_W - 1, C_CONV), F32)],
        compiler_params=pltpu.CompilerParams(dimension_semantics=("arbitrary",),
                                             vmem_limit_bytes=VMEM_LIMIT),
    )(a_log, dt_bias, qkv.reshape(nb, 1, C_CONV), z.reshape(nb, 1, W_V_A),
      lg.reshape(nb, 1, LOGIT_PAD), state_conv, conv_w, state_delta, norm_w.reshape(1, DV_A))


def _topk_kernel(qt_ref, km_ref, idx_ref, *, n_full, kk):
    b = pl.program_id(0)
    qt = qt_ref[...]
    q_col = jnp.sum(jnp.where(_iota(qt.shape, 1) == b, qt, 0.0), axis=-1, keepdims=True)
    prod = km_ref[0] * q_col
    sc = jnp.concatenate(
        [jnp.sum(prod[h * HD_B:(h + 1) * HD_B], axis=0, keepdims=True) for h in range(H_B)],
        axis=0)
    blk = _iota(sc.shape, 1)
    sc = jnp.where(blk < n_full, sc, NEG_INF)
    rank = _block_rank(sc, n_full, 1)
    blk_f = blk.astype(F32)
    out = jnp.zeros(sc.shape, F32)
    for r in range(kk):
        pick = jnp.sum(jnp.where(rank == float(r), blk_f, 0.0), axis=-1, keepdims=True)
        out = jnp.where(blk == r, pick, out)
    idx_ref[0] = out.astype(jnp.int32)


def _topk(q_t, km, *, n_full, kk):
    n_seq = km.shape[0]
    return pl.pallas_call(
        functools.partial(_topk_kernel, n_full=n_full, kk=kk),
        grid=(n_seq,),
        in_specs=[_const_spec(q_t.shape), pl.BlockSpec((1, W_B, LANES), lambda b: (b, 0, 0))],
        out_specs=pl.BlockSpec((1, H_B, LANES), lambda b: (b, 0, 0)),
        out_shape=jax.ShapeDtypeStruct((n_seq, H_B, LANES), jnp.int32),
        compiler_params=pltpu.CompilerParams(dimension_semantics=("arbitrary",),
                                             vmem_limit_bytes=VMEM_LIMIT),
    )(q_t, km)


_PAGE_DMA_PARAMS = pltpu.CompilerParams(dimension_semantics=("arbitrary",),
                                        vmem_limit_bytes=VMEM_LIMIT, disable_bounds_checks=True)


def _moba_sample_kernel(top_ref, pt_ref, q_ref, kn_ref, vn_ref, ck_ref, cv_ref, o_ref,
                        kbuf, vbuf, sems, *, kk, ppb, n_pages, page_size, n_seq):
    b = pl.program_id(0)
    pages_per_head = kk * ppb

    def copies(seq, h, i, page=None):
        slot = seq % 2
        if page is None:
            blk = top_ref[(seq * H_B + h) * kk + i // ppb]
            page = pt_ref[seq * n_pages + blk * ppb + i % ppb]
        keys = pl.ds(i * page_size, page_size)
        return (pltpu.make_async_copy(ck_ref.at[page, h], kbuf.at[slot, h, :, keys],
                                      sems.at[slot, 0, h]),
                pltpu.make_async_copy(cv_ref.at[page, h], vbuf.at[slot, h, :, keys],
                                      sems.at[slot, 1, h]))

    def start_fetch(seq):
        for h in range(H_B):
            for i in range(pages_per_head):
                for c in copies(seq, h, i):
                    c.start()

    @pl.when(b == 0)
    def _():
        start_fetch(b)

    @pl.when(b + 1 < n_seq)
    def _():
        start_fetch(b + 1)

    slot = b % 2
    q = q_ref[0]
    kn = kn_ref[0]
    vn = vn_ref[0]
    for h in range(H_B):
        for i in range(pages_per_head):
            for c in copies(b, h, i, page=0):
                c.wait()
        sl = slice(h * HD_B, (h + 1) * HD_B)
        qh = q[:, sl] * (HD_B ** -0.5)
        s = _mm(_bf(jnp.broadcast_to(qh, (SUBLANES, HD_B))), _bf(kbuf[slot, h]))[0:1]
        s_own = jnp.sum(qh * kn[:, sl], axis=-1, keepdims=True)
        m = jnp.maximum(jnp.max(s, axis=-1, keepdims=True), s_own)
        p = jnp.exp(s - m)
        p_own = jnp.exp(s_own - m)
        l = jnp.sum(p, axis=-1, keepdims=True) + p_own
        pv = _mm_nt(_bf(jnp.broadcast_to(p, (SUBLANES, p.shape[1]))), _bf(vbuf[slot, h]))[0:1]
        o_ref[0, :, sl] = _bf((pv + p_own * vn[:, sl]) / l)


def _moba_sample(top_flat, pt_flat, q3, kn3, vn3, cache_kt, cache_vt, *, kk, ppb):
    nb = q3.shape[0]
    _, _, _, page_size = cache_kt.shape
    keys = kk * ppb * page_size
    row = pl.BlockSpec((1, 1, W_B), lambda b, *_: (b, 0, 0))
    any_spec = pl.BlockSpec(memory_space=pl.ANY)
    return pl.pallas_call(
        functools.partial(_moba_sample_kernel, kk=kk, ppb=ppb,
                          n_pages=pt_flat.shape[0] // nb, page_size=page_size, n_seq=nb),
        grid_spec=pltpu.PrefetchScalarGridSpec(
            num_scalar_prefetch=2,
            grid=(nb,),
            in_specs=[row, row, row, any_spec, any_spec],
            out_specs=row,
            scratch_shapes=[pltpu.VMEM((2, H_B, HD_B, keys), F32),
                            pltpu.VMEM((2, H_B, HD_B, keys), F32),
                            pltpu.SemaphoreType.DMA((2, 2, H_B))]),
        out_shape=jax.ShapeDtypeStruct((nb, 1, W_B), BF16),
        compiler_params=_PAGE_DMA_PARAMS,
    )(top_flat, pt_flat, q3, kn3, vn3, cache_kt, cache_vt)


def _rearranged_w_in(w_in, d_model):
    o_z = C_CONV
    o_lg = o_z + W_V_A
    o_b = o_lg + 2 * H_A
    o_g = o_b + 3 * W_B
    pad = jnp.zeros((d_model, LOGIT_PAD - 2 * H_A), w_in.dtype)
    return _bf(jnp.concatenate(
        [w_in[:, :o_lg], w_in[:, o_b:o_g + 2 * d_model], w_in[:, o_lg:o_b], pad], axis=1))


def kernel(x_prompt, x_sample, cache_k, cache_v, page_table, state_delta, state_conv, norm1_w,
           w_in, conv_w, a_log, dt_bias, delta_norm_w, w_proj_a, w_proj_b, w_out, norm2_w, w_up,
           w_down, norm_f_w):
    depth = w_in.shape[0]
    assert depth == 1, "single-layer trunk"
    bp, sp, d_model = x_prompt.shape
    bs, ss, _ = x_sample.shape
    assert ss == 1, "one new token per sample sequence"
    _, n_pool, page_size, _, _ = cache_k.shape
    n_pages = page_table.shape[1]
    past_len = n_pages * page_size
    ppb = MOBA_BLOCK // page_size
    n_full = past_len // MOBA_BLOCK
    assert n_full * MOBA_BLOCK == past_len, "the sample token starts a fresh MoBA block"
    kk = min(MOBA_TOPK, n_full)
    assert kk >= 1

    w_r = _rearranged_w_in(w_in[0], d_model)
    o_qb = C_CONV + W_V_A + 2 * H_A
    wqkv_t = _bf(w_in[0][:, o_qb:o_qb + 3 * W_B].T)
    wpa, wpb, wo = _bf(w_proj_a[0]), _bf(w_proj_b[0]), _bf(w_out[0])
    wup, wdn = _bf(w_up[0]), _bf(w_down[0])

    tm = 256
    pos_p = jnp.arange(sp, dtype=jnp.int32)
    (qkv, z, qt, kt, vt, ga, gb, lg, conv_p) = _in_proj(
        x_prompt.reshape(bp * sp, d_model), norm1_w[0], w_r, _rope_tables(pos_p),
        tm=tm, rows_per_seq=sp, wqkv_t=wqkv_t, tabs_t=_rope_tables_t(pos_p), conv_w=conv_w[0])
    o_a, s_p = _delta_prompt(qkv.reshape(bp, sp, C_CONV), z.reshape(bp, sp, W_V_A),
                             lg.reshape(bp, sp, LOGIT_PAD), a_log[0], dt_bias[0],
                             delta_norm_w[0])
    o_b = _moba_prompt(qt, kt, vt)
    cache_kt = cache_k.reshape(n_pool, page_size, H_B, HD_B).transpose(0, 2, 3, 1)
    cache_vt = cache_v.reshape(n_pool, page_size, H_B, HD_B).transpose(0, 2, 3, 1)
    past_pages = page_table[:, :n_full * ppb].reshape(-1)
    y_p, km = _merge_mlp(x_prompt.reshape(bp * sp, d_model), o_a.reshape(bp * sp, W_V_A),
                         o_b.reshape(bp * sp, W_B), ga, gb, wpa, wpb, wo, norm2_w[0], wup, wdn,
                         norm_f_w, tm=tm, key_stream=(past_pages, cache_kt, ppb, n_full * ppb))

    pos_s = jnp.full((bs,), past_len, dtype=jnp.int32)
    (qkv_s, z_s, qr_s, kr_s, vr_s, ga_s, gb_s, lg_s) = _in_proj(
        x_sample.reshape(bs, d_model), norm1_w[0], w_r, _rope_tables(pos_s),
        tm=bs, rows_per_seq=bs)
    o_a_s, s_s, conv_s = _delta_step(qkv_s, z_s, lg_s, state_conv[0], conv_w[0], state_delta[0],
                                     a_log[0], dt_bias[0], delta_norm_w[0])
    top = _topk(qr_s.T, km, n_full=n_full, kk=kk)
    top_flat = top[:, :, :kk].reshape(-1)
    o_b_s = _moba_sample(top_flat, page_table.reshape(-1), qr_s.reshape(bs, 1, W_B),
                         kr_s.reshape(bs, 1, W_B), vr_s.reshape(bs, 1, W_B), cache_kt, cache_vt,
                         kk=kk, ppb=ppb)
    y_s = _merge_mlp(x_sample.reshape(bs, d_model), o_a_s.reshape(bs, W_V_A),
                     o_b_s.reshape(bs, W_B), ga_s, gb_s, wpa, wpb, wo, norm2_w[0], wup, wdn,
                     norm_f_w, tm=bs)

    def kv_out(t):
        return t.reshape(1, bp, H_B, HD_B, sp).transpose(0, 1, 4, 2, 3)

    return (y_p.reshape(bp, sp, d_model), y_s.reshape(bs, ss, d_model),
            kv_out(kt), kv_out(vt),
            s_p.reshape(1, bp, H_A, DK_A, DV_A), conv_p.reshape(1, bp, CONV_W - 1, C_CONV),
            kr_s.reshape(1, bs, ss, H_B, HD_B), vr_s.reshape(1, bs, ss, H_B, HD_B),
            s_s.reshape(1, bs, H_A, DK_A, DV_A), conv_s.reshape(1, bs, CONV_W - 1, C_CONV))
```

```python
import functools
import math

import jax
import jax.numpy as jnp
from jax import lax
from jax.experimental import pallas as pl
from jax.experimental.pallas import tpu as pltpu

F32 = jnp.float32
BF16 = jnp.bfloat16
HIGHEST = lax.Precision.HIGHEST

H_A = 4
DK_A = 128
DV_A = 128
W_QK_A = H_A * DK_A
W_V_A = H_A * DV_A
C_CONV = 2 * W_QK_A + W_V_A
CONV_W = 4
H_B = 8
HD_B = 64
W_B = H_B * HD_B
MOBA_BLOCK = 256
MOBA_TOPK = 3
ROPE_THETA = 500000.0
ROT_DIM = HD_B // 4
EPS = 1e-6

LANES = 128
SUBLANES = 8
BF16_SUBLANES = 16
MOBA_HEADS_PER_STEP = 8
KMEAN_SLOTS = 64
KMEAN_GROUP = 16
CONV_ROWS = 64
HEADS_PER_LANE_TILE = LANES // HD_B
LOGIT_PAD = LANES
DELTA_CHUNK = LANES
DELTA_TILE = 4 * DELTA_CHUNK
VMEM_LIMIT = 56 * 1024 * 1024
NEG_INF = float("-inf")


def _dot(a, b, dims, precision=None):
    return lax.dot_general(a, b, (dims, ((), ())), precision=precision,
                           preferred_element_type=F32)


def _mm(a, b, precision=None):
    return _dot(a, b, ((1,), (0,)), precision)


def _mm_nt(a, b, precision=None):
    return _dot(a, b, ((1,), (1,)), precision)


def _bf(x):
    return x.astype(BF16)


def _sigmoid(x):
    return 1.0 / (1.0 + jnp.exp(-x))


def _silu(x):
    return x * _sigmoid(x)


def _softplus(x):
    return jnp.maximum(x, 0.0) + jnp.log(1.0 + jnp.exp(-jnp.abs(x)))


def _rms(x, w):
    return x * lax.rsqrt(jnp.mean(x * x, axis=-1, keepdims=True) + EPS) * w


def _iota(shape, dim):
    return lax.broadcasted_iota(jnp.int32, shape, dim)


def _const_spec(shape):
    nd = len(shape)
    return pl.BlockSpec(shape, lambda *_: (0,) * nd, pipeline_mode=pl.Buffered(1))


def _block_rank(sc, n_blocks, axis):
    blk = _iota(sc.shape, axis)
    rank = jnp.zeros(sc.shape, F32)
    for j in range(n_blocks):
        cj = sc[j:j + 1, :] if axis == 0 else sc[:, j:j + 1]
        first_on_tie = jnp.where(blk > j, 1.0, 0.0)
        rank = rank + jnp.where(cj > sc, 1.0, jnp.where(cj == sc, first_on_tie, 0.0))
    return rank


def _split_bf16(x):
    hi = _bf(x)
    return hi, _bf(x - hi.astype(F32))


def _mm_3pass(a, b):
    a_hi, a_lo = _split_bf16(a)
    b_hi, b_lo = _split_bf16(b)
    return _mm(jnp.concatenate([a_hi, a_lo, a_hi], axis=1),
               jnp.concatenate([b_hi, b_hi, b_lo], axis=0))


def _in_proj_kernel(x_ref, nw_ref, w_ref, cos_ref, slo_ref, shi_ref, *rest,
                    d_model, tiles_per_seq, kv_transposed):
    if kv_transposed:
        (wqkv_t_ref, cos_t_ref, sin_t_ref, cw_ref,
         qkv_ref, z_ref, qb_ref, kb_ref, vb_ref, ga_ref, gb_ref, lg_ref, conv_ref, hist_s) = rest
    else:
        qkv_ref, z_ref, qb_ref, kb_ref, vb_ref, ga_ref, gb_ref, lg_ref = rest
    step_i = pl.program_id(0)
    xb = _bf(_rms(x_ref[...], nw_ref[...]))
    tm = xb.shape[0]
    half = ROT_DIM // 2

    def conv_act_in_place():
        last = qkv_ref[tm - SUBLANES:tm, :]
        conv_ref[0] = last[SUBLANES - (CONV_W - 1):, :]
        for j in range(C_CONV // LANES):
            lanes = slice(j * LANES, (j + 1) * LANES)
            w = cw_ref[:, lanes]
            for rb in reversed(range(tm // CONV_ROWS)):
                r0 = rb * CONV_ROWS
                y = qkv_ref[r0:r0 + CONV_ROWS, lanes]
                prev = hist_s[:, lanes] if rb == 0 else qkv_ref[r0 - SUBLANES:r0, lanes]
                ext = jnp.concatenate([prev, y], axis=0)
                acc = y * w[CONV_W - 1:CONV_W]
                for i in range(CONV_W - 1):
                    lo = SUBLANES - (CONV_W - 1) + i
                    acc = acc + ext[lo:lo + CONV_ROWS] * w[i:i + 1]
                act = _silu(acc)
                if lanes.start < 2 * W_QK_A:
                    act = act * lax.rsqrt(jnp.sum(act * act, axis=-1, keepdims=True) + EPS)
                if lanes.start < W_QK_A:
                    act = act * (DK_A ** -0.5)
                qkv_ref[r0:r0 + CONV_ROWS, lanes] = act
        hist_s[...] = last

    def cols(c0, n):
        return _mm(xb, w_ref[:, c0:c0 + n])

    def rope_store(dst_ref, c0):
        cos, slo, shi = cos_ref[...], slo_ref[...], shi_ref[...]
        for c in range(W_B // LANES):
            y = cols(c0 + c * LANES, LANES)
            up = pltpu.roll(y, LANES - half, 1)
            dn = pltpu.roll(y, half, 1)
            dst_ref[:, c * LANES:(c + 1) * LANES] = y * cos + up * slo + dn * shi

    def rope_store_t(dst_ref, r0):
        cos_t, sin_t = cos_t_ref[...], sin_t_ref[...]
        y_t = _mm_nt(wqkv_t_ref[r0:r0 + W_B, :], xb)
        for h in range(H_B):
            r = h * HD_B
            x1, x2 = y_t[r:r + half], y_t[r + half:r + ROT_DIM]
            dst_ref[0, r:r + half, :] = x1 * cos_t - x2 * sin_t
            dst_ref[0, r + half:r + ROT_DIM, :] = x2 * cos_t + x1 * sin_t
            dst_ref[0, r + ROT_DIM:r + HD_B, :] = y_t[r + ROT_DIM:r + HD_B]

    if kv_transposed:
        @pl.when(step_i % tiles_per_seq == 0)
        def _():
            hist_s[...] = jnp.zeros_like(hist_s)

    step = 512
    c0 = 0
    for c in range(C_CONV // step):
        qkv_ref[:, c0:c0 + step] = cols(c0, step)
        c0 += step
    if kv_transposed:
        conv_act_in_place()
    z_ref[...] = cols(c0, W_V_A)
    c0 += W_V_A
    if kv_transposed:
        rope_store_t(qb_ref, 0)
        rope_store_t(kb_ref, W_B)
        vb_ref[0] = _mm_nt(wqkv_t_ref[2 * W_B:3 * W_B, :], xb)
    else:
        rope_store(qb_ref, c0)
        rope_store(kb_ref, c0 + W_B)
        vb_ref[...] = cols(c0 + 2 * W_B, W_B)
    c0 += 3 * W_B
    for g_ref in (ga_ref, gb_ref):
        for c in range(d_model // step):
            g_ref[:, c * step:(c + 1) * step] = cols(c0, step)
            c0 += step
    lg_ref[...] = cols(c0, LOGIT_PAD)


def _in_proj(x2d, norm_w, w_r, tabs, *, tm, rows_per_seq, wqkv_t=None, tabs_t=None, conv_w=None):
    rows, d_model = x2d.shape
    n_tiles = rows // tm
    tiles_per_seq = rows_per_seq // tm
    n_seq = rows // rows_per_seq
    kv_transposed = wqkv_t is not None

    def row_spec(n):
        return pl.BlockSpec((tm, n), lambda i: (i, 0))

    def sds(*shape):
        return jax.ShapeDtypeStruct(shape, F32)

    kv_t_spec = pl.BlockSpec((1, W_B, tm), lambda i: (i // tiles_per_seq, 0, i % tiles_per_seq))
    kv_shape = sds(n_seq, W_B, rows_per_seq) if kv_transposed else sds(rows, W_B)
    kv_spec = kv_t_spec if kv_transposed else row_spec(W_B)
    tab_spec = pl.BlockSpec((tm, LANES), lambda i: (i % tiles_per_seq, 0))
    in_specs = [row_spec(d_model), _const_spec((1, d_model)), _const_spec(w_r.shape),
                tab_spec, tab_spec, tab_spec]
    args = [x2d, norm_w.reshape(1, d_model), w_r, *tabs]
    out_shape = [sds(rows, C_CONV), sds(rows, W_V_A), kv_shape, kv_shape, kv_shape,
                 sds(rows, d_model), sds(rows, d_model), sds(rows, LOGIT_PAD)]
    out_specs = [row_spec(C_CONV), row_spec(W_V_A), kv_spec, kv_spec, kv_spec,
                 row_spec(d_model), row_spec(d_model), row_spec(LOGIT_PAD)]
    if kv_transposed:
        tab_t_spec = pl.BlockSpec((ROT_DIM // 2, tm), lambda i: (0, i % tiles_per_seq))
        in_specs += [_const_spec(wqkv_t.shape), tab_t_spec, tab_t_spec, _const_spec(conv_w.shape)]
        args += [wqkv_t, *tabs_t, conv_w]
        out_shape.append(sds(n_seq, CONV_W - 1, C_CONV))
        out_specs.append(pl.BlockSpec((1, CONV_W - 1, C_CONV),
                                      lambda i: (i // tiles_per_seq, 0, 0)))
        assert tm >= SUBLANES
        scratch = [pltpu.VMEM((SUBLANES, C_CONV), F32)]
    else:
        scratch = []
    return pl.pallas_call(
        functools.partial(_in_proj_kernel, d_model=d_model, tiles_per_seq=tiles_per_seq,
                          kv_transposed=kv_transposed),
        grid=(n_tiles,),
        in_specs=in_specs,
        out_specs=out_specs,
        out_shape=out_shape,
        scratch_shapes=scratch,
        compiler_params=pltpu.CompilerParams(dimension_semantics=("arbitrary",),
                                             vmem_limit_bytes=VMEM_LIMIT),
    )(*args)


def _rope_angles(pos):
    half = ROT_DIM // 2
    inv_freq = jnp.exp(jnp.arange(half, dtype=F32) * (-2.0 * math.log(ROPE_THETA) / ROT_DIM))
    ang = pos.astype(F32)[:, None] * inv_freq[None, :]
    return jnp.cos(ang), jnp.sin(ang)


def _rope_tables_t(pos):
    cos, sin = _rope_angles(pos)
    return cos.T, sin.T


def _rope_tables(pos):
    half = ROT_DIM // 2
    cos, sin = _rope_angles(pos)
    n = pos.shape[0]
    ones = jnp.ones((n, HD_B - ROT_DIM), F32)
    zeros_h = jnp.zeros((n, half), F32)
    zeros_r = jnp.zeros((n, HD_B - ROT_DIM), F32)
    cos_h = jnp.concatenate([cos, cos, ones], axis=1)
    slo_h = jnp.concatenate([-sin, zeros_h, zeros_r], axis=1)
    shi_h = jnp.concatenate([zeros_h, sin, zeros_r], axis=1)
    tile = lambda t: jnp.tile(t, (1, HEADS_PER_LANE_TILE))
    return tile(cos_h), tile(slo_h), tile(shi_h)


def _delta_prompt_kernel(alog_ref, dtb_ref, x_ref, z_ref, lg_ref, nw_ref,
                         o_ref, s_ref,
                         state_s, u_s, w_s, qg_s, qk_s, kdt_s, gl_s):
    t = pl.program_id(1)
    cs = DELTA_CHUNK
    n_chunks = x_ref.shape[1] // cs
    heads = range(H_A)

    @pl.when(t == 0)
    def _():
        state_s[...] = jnp.zeros_like(state_s)

    sq = (cs, cs)
    row = _iota(sq, 0)
    col = _iota(sq, 1)
    causal = col <= row
    strict = col < row
    tril_b = _bf(jnp.where(causal, 1.0, 0.0))
    eye_f = jnp.where(col == row, 1.0, 0.0)
    lane_row = _iota((1, LANES), 1)
    alog_row = jnp.zeros((1, LANES), F32)
    dtb_row = jnp.zeros((1, LANES), F32)
    for h in heads:
        alog_row = jnp.where(lane_row == H_A + h, alog_ref[h], alog_row)
        dtb_row = jnp.where(lane_row == H_A + h, dtb_ref[h], dtb_row)
    neg_a_row = -jnp.exp(alog_row)

    def prep(c, carry):
        rows = pl.ds(pl.multiple_of(c * cs, cs), cs)
        q = [x_ref[0, rows, h * DK_A:(h + 1) * DK_A] for h in heads]
        k = [x_ref[0, rows, W_QK_A + h * DK_A:W_QK_A + (h + 1) * DK_A] for h in heads]
        v = [x_ref[0, rows, 2 * W_QK_A + h * DV_A:2 * W_QK_A + (h + 1) * DV_A] for h in heads]
        lg = lg_ref[0, rows, :]
        sig = _sigmoid(lg)
        beta = [sig[:, h:h + 1] for h in heads]
        g_all = neg_a_row * _softplus(lg + dtb_row)
        pieces = []
        rem = g_all
        for _ in range(3):
            piece = _bf(rem)
            pieces.append(piece)
            rem = rem - piece.astype(F32)
        csum = _mm(tril_b, jnp.concatenate(pieces, axis=1))
        gc_all = csum[:, :LANES] + csum[:, LANES:2 * LANES] + csum[:, 2 * LANES:]
        gc = [jnp.broadcast_to(gc_all[:, H_A + h:H_A + h + 1], (cs, LANES)) for h in heads]
        decay = [jnp.exp(jnp.where(causal, gc[h] - gc[h].T, NEG_INF)) for h in heads]
        kb = [k[h] * beta[h] for h in heads]
        a = [jnp.where(strict, _mm_nt(_bf(kb[h]), _bf(k[h])) * decay[h], 0.0) for h in heads]
        t_inv = [eye_f - a[h] for h in heads]
        a_pow = [_mm_3pass(a[h], a[h]) for h in heads]
        n_levels = int(math.log2(cs)) - 1
        for lvl in range(n_levels):
            if lvl < n_levels - 1:
                prod = [_mm_3pass(a_pow[h], jnp.concatenate([a_pow[h], t_inv[h]], axis=1))
                        for h in heads]
                a_pow = [prod[h][:, :cs] for h in heads]
                t_inv = [t_inv[h] + prod[h][:, cs:] for h in heads]
            else:
                t_inv = [t_inv[h] + _mm_3pass(a_pow[h], t_inv[h]) for h in heads]
        e_gc = [jnp.exp(gc[h]) for h in heads]
        uw = [_mm(_bf(t_inv[h]),
                  jnp.concatenate([_bf(v[h] * beta[h]), _bf(kb[h] * e_gc[h])], axis=1))
              for h in heads]
        qk = [jnp.where(causal, _mm_nt(_bf(q[h]), _bf(k[h])) * decay[h], 0.0) for h in heads]
        for h in heads:
            g_last = gc[h][cs - 1:cs, :]
            u_s[h, rows, :] = uw[h][:, :DV_A]
            w_s[h, rows, :] = _bf(uw[h][:, DV_A:])
            qk_s[h, rows, :] = _bf(qk[h])
            qg_s[h, rows, :] = _bf(q[h] * e_gc[h])
            kdt_s[h, c] = _bf((k[h] * jnp.exp(g_last - gc[h])).T)
            gl_s[h, c] = jnp.broadcast_to(jnp.exp(g_last), (SUBLANES, LANES))
        return carry

    lax.fori_loop(0, n_chunks, prep, 0)

    nw = nw_ref[...]

    def scan(c, carry):
        rows = pl.ds(pl.multiple_of(c * cs, cs), cs)
        state = [state_s[h] for h in heads]
        sb = [_bf(state[h]) for h in heads]
        ws = [_mm(jnp.concatenate([w_s[h, rows, :], qg_s[h, rows, :]], axis=0), sb[h])
              for h in heads]
        vb = [_bf(u_s[h, rows, :] - ws[h][:cs]) for h in heads]
        o = [ws[h][cs:] + _mm(qk_s[h, rows, :], vb[h]) for h in heads]
        new = [state[h] * gl_s[h, c][0:1] + _mm(kdt_s[h, c], vb[h]) for h in heads]
        for h in heads:
            state_s[h] = new[h]
            lanes = slice(h * DV_A, (h + 1) * DV_A)
            o_ref[0, rows, lanes] = _bf(_rms(o[h], nw) * _silu(z_ref[0, rows, lanes]))
        return carry

    lax.fori_loop(0, n_chunks, scan, 0)

    @pl.when(t == pl.num_programs(1) - 1)
    def _():
        s_ref[0] = state_s[...]


def _delta_prompt(qkv, z, lg, a_log, dt_bias, norm_w):
    bsz, seq, _ = qkv.shape
    ts = min(DELTA_TILE, seq)
    n_chunks = ts // DELTA_CHUNK
    assert seq % ts == 0 and ts % DELTA_CHUNK == 0 and DK_A == LANES and DV_A == LANES

    def tile_spec(n):
        return pl.BlockSpec((1, ts, n), lambda b, t: (b, t, 0))

    def per_head(shape, dtype):
        return pltpu.VMEM((H_A,) + shape, dtype)

    smem = pl.BlockSpec(memory_space=pltpu.SMEM)
    return pl.pallas_call(
        _delta_prompt_kernel,
        grid=(bsz, seq // ts),
        in_specs=[smem, smem, tile_spec(C_CONV), tile_spec(W_V_A), tile_spec(LOGIT_PAD),
                  _const_spec((1, DV_A))],
        out_specs=[tile_spec(W_V_A),
                   pl.BlockSpec((1, H_A, DK_A, DV_A), lambda b, t: (b, 0, 0, 0))],
        out_shape=[jax.ShapeDtypeStruct((bsz, seq, W_V_A), BF16),
                   jax.ShapeDtypeStruct((bsz, H_A, DK_A, DV_A), F32)],
        scratch_shapes=[per_head((DK_A, DV_A), F32),
                        per_head((ts, DV_A), F32),
                        per_head((ts, DK_A), BF16),
                        per_head((ts, DK_A), BF16),
                        per_head((ts, DELTA_CHUNK), BF16),
                        per_head((n_chunks, DK_A, DELTA_CHUNK), BF16),
                        per_head((n_chunks, SUBLANES, LANES), F32)],
        compiler_params=pltpu.CompilerParams(dimension_semantics=("arbitrary", "arbitrary"),
                                             vmem_limit_bytes=VMEM_LIMIT),
    )(a_log, dt_bias, qkv, z, lg, norm_w.reshape(1, DV_A))


def _moba_prompt_kernel(qt_ref, kt_ref, vt_ref, o_ref, kn_s, vt_s, km_s, ch_s, *, n_blocks):
    qi = pl.program_id(2)
    bs = MOBA_BLOCK
    n_pairs = qt_ref.shape[1] // LANES
    heads = range(n_pairs * HEADS_PER_LANE_TILE)
    pair_of = [h // HEADS_PER_LANE_TILE for h in heads]

    @pl.when(qi == 0)
    def _():
        km_s[...] = jnp.zeros_like(km_s)
        ones = jnp.ones((BF16_SUBLANES, bs), BF16)
        for j in range(n_blocks):
            keys = slice(j * bs, (j + 1) * bs)
            for pr in range(n_pairs):
                kj = kt_ref[0, pr * LANES:(pr + 1) * LANES, keys].T
                kn_s[j, pr] = _bf(kj)
                km_s[pr, j:j + 1, :] = jnp.mean(kj, axis=0, keepdims=True)
            for h in heads:
                vt_s[j, h, 0:HD_B, :] = _bf(vt_ref[0, h * HD_B:(h + 1) * HD_B, keys])
                vt_s[j, h, HD_B:, :] = ones

    d_row = _iota((LANES, bs), 0)
    blk = _iota((km_s.shape[1], bs), 0)
    key_le_query = _iota((bs, bs), 0) <= _iota((bs, bs), 1)
    qh = []
    for h in heads:
        hh = h % HEADS_PER_LANE_TILE
        qt = qt_ref[0, pair_of[h] * LANES:(pair_of[h] + 1) * LANES, :]
        qh.append(jnp.where((d_row >= hh * HD_B) & (d_row < (hh + 1) * HD_B), qt, 0.0))
    sc = [jnp.where(blk < qi, _mm(km_s[pair_of[h]], qh[h], HIGHEST), NEG_INF) for h in heads]
    rank = [_block_rank(sc[h], n_blocks, 0) for h in heads]
    for h in heads:
        ch_s[h] = jnp.where(blk < qi, jnp.where(rank[h] < MOBA_TOPK, 1.0, 0.0), 0.0)

    qs = [_bf(qh[h] * (HD_B ** -0.5 * math.log2(math.e))) for h in heads]
    s = [jnp.where(key_le_query, _mm(kn_s[qi, pair_of[h]], qs[h]), NEG_INF)
         for h in heads]
    m = [jnp.max(s[h], axis=0, keepdims=True) for h in heads]
    acc = [_mm(vt_s[qi, h], _bf(jnp.exp2(s[h] - m[h]))) for h in heads]

    def two_past_blocks(jj, carry):
        m, acc = carry
        js = (2 * jj, 2 * jj + 1)
        s = [[jnp.where(ch_s[h, pl.ds(j, 1), :] > 0.0, _mm(kn_s[j, pair_of[h]], qs[h]), NEG_INF)
              for j in js] for h in heads]
        m_new = [jnp.maximum(m[h], jnp.maximum(jnp.max(s[h][0], axis=0, keepdims=True),
                                               jnp.max(s[h][1], axis=0, keepdims=True)))
                 for h in heads]
        acc = [jnp.exp2(m[h] - m_new[h]) * acc[h]
               + _mm(vt_s[js[0], h], _bf(jnp.exp2(s[h][0] - m_new[h])))
               + _mm(vt_s[js[1], h], _bf(jnp.exp2(s[h][1] - m_new[h]))) for h in heads]
        return m_new, acc

    m, acc = lax.fori_loop(0, (qi + 1) // 2, two_past_blocks, (m, acc))
    o_t = jnp.concatenate([acc[h][:HD_B] / acc[h][HD_B:HD_B + 1] for h in heads], axis=0)
    o_ref[0] = _bf(o_t.T)


def _moba_prompt(qt, kt, vt):
    bsz, _, seq = qt.shape
    n_blocks = seq // MOBA_BLOCK
    assert seq % MOBA_BLOCK == 0
    blocks_pad = -(-n_blocks // SUBLANES) * SUBLANES
    n_pairs = MOBA_HEADS_PER_STEP // HEADS_PER_LANE_TILE
    rows = n_pairs * LANES
    q_spec = pl.BlockSpec((1, rows, MOBA_BLOCK), lambda b, p, i: (b, p, i))
    kv_spec = pl.BlockSpec((1, rows, seq), lambda b, p, i: (b, p, 0))
    return pl.pallas_call(
        functools.partial(_moba_prompt_kernel, n_blocks=n_blocks),
        grid=(bsz, H_B // MOBA_HEADS_PER_STEP, n_blocks),
        in_specs=[q_spec, kv_spec, kv_spec],
        out_specs=pl.BlockSpec((1, MOBA_BLOCK, rows), lambda b, p, i: (b, i, p)),
        out_shape=jax.ShapeDtypeStruct((bsz, seq, W_B), BF16),
        scratch_shapes=[pltpu.VMEM((n_blocks, n_pairs, MOBA_BLOCK, LANES), BF16),
                        pltpu.VMEM((n_blocks, MOBA_HEADS_PER_STEP, HD_B + BF16_SUBLANES,
                                    MOBA_BLOCK), BF16),
                        pltpu.VMEM((n_pairs, blocks_pad, LANES), F32),
                        pltpu.VMEM((MOBA_HEADS_PER_STEP, blocks_pad, MOBA_BLOCK), F32)],
        compiler_params=pltpu.CompilerParams(
            dimension_semantics=("arbitrary", "arbitrary", "arbitrary"),
            vmem_limit_bytes=VMEM_LIMIT),
    )(qt, kt, vt)


def _key_mean_groups(pt_ref, cache_ref, km_ref, page_buf, page_sems, stream):
    pages_per_step, pages_per_seq, total, ppb, page_size = stream
    step_i = pl.program_id(0)
    base = step_i * pages_per_step
    blocks_per_group = KMEAN_GROUP // ppb
    lane = _iota((HD_B, LANES), 1)

    def page_copy(p, slot):
        return pltpu.make_async_copy(cache_ref.at[pt_ref[p]], page_buf.at[slot], page_sems.at[slot])

    @pl.when(step_i == 0)
    def _():
        for s in range(KMEAN_SLOTS):
            page_copy(s, s).start()

    @pl.when(base % pages_per_seq == 0)
    def _():
        km_ref[...] = jnp.zeros_like(km_ref)

    def consume_group(g):
        p0 = base + g * KMEAN_GROUP
        slot0 = (g * KMEAN_GROUP) % KMEAN_SLOTS
        for j in range(KMEAN_GROUP):
            page_copy(p0 + j, slot0 + j).wait()
        blk0 = (base % pages_per_seq) // ppb + g * blocks_per_group
        for h in range(H_B):
            acc = jnp.zeros((HD_B, LANES), F32)
            for k in range(blocks_per_group):
                x = page_buf[slot0 + k * ppb, h]
                for r in range(1, ppb):
                    x = x + page_buf[slot0 + k * ppb + r, h]
                mean = jnp.sum(x, axis=-1, keepdims=True) * (1.0 / (ppb * page_size))
                acc = jnp.where(lane == blk0 + k, mean, acc)
            rows = slice(h * HD_B, (h + 1) * HD_B)
            km_ref[0, rows, :] = km_ref[0, rows, :] + acc
        for j in range(KMEAN_GROUP):
            @pl.when(p0 + j + KMEAN_SLOTS < total)
            def _():
                page_copy(p0 + j + KMEAN_SLOTS, slot0 + j).start()

    return [functools.partial(consume_group, g) for g in range(pages_per_step // KMEAN_GROUP)]


def _merge_mlp_kernel(*refs, ff_step, stream):
    if stream is None:
        (x_ref, oa_ref, ob_ref, ga_ref, gb_ref, wpa_ref, wpb_ref, wo_ref, n2_ref, wup_ref, wdn_ref,
         nf_ref, y_ref) = refs
        groups = []
    else:
        (pt_ref, x_ref, oa_ref, ob_ref, ga_ref, gb_ref, wpa_ref, wpb_ref, wo_ref, n2_ref, wup_ref,
         wdn_ref, nf_ref, cache_ref, y_ref, km_ref, page_buf, page_sems) = refs
        groups = _key_mean_groups(pt_ref, cache_ref, km_ref, page_buf, page_sems, stream)
    d_ff = wup_ref.shape[1]
    n_stages = 1 + d_ff // ff_step

    def run_groups(stage):
        for g in groups[stage * len(groups) // n_stages:(stage + 1) * len(groups) // n_stages]:
            g()

    y_a = _mm(oa_ref[...], wpa_ref[...])
    y_b = _mm(ob_ref[...], wpb_ref[...])
    mixed = _sigmoid(ga_ref[...]) * y_a + _sigmoid(gb_ref[...]) * y_b
    h = x_ref[...] + _mm(_bf(mixed), wo_ref[...])
    hn = _bf(_rms(h, n2_ref[...]))
    run_groups(0)
    out = h
    for c in range(d_ff // ff_step):
        u = jnp.maximum(_mm(hn, wup_ref[:, c * ff_step:(c + 1) * ff_step]), 0.0)
        out = out + _mm(_bf(u * u), wdn_ref[c * ff_step:(c + 1) * ff_step, :])
        run_groups(1 + c)
    y_ref[...] = _rms(out, nf_ref[...])


def _merge_mlp(x2d, oa, ob, ga, gb, wpa, wpb, wo, n2, wup, wdn, nf, *, tm, key_stream=None):
    rows, d_model = x2d.shape
    d_ff = wup.shape[1]
    n_tiles = rows // tm

    def row_spec(n):
        return pl.BlockSpec((tm, n), lambda i, *_: (i, 0))

    in_specs = [row_spec(d_model), row_spec(W_V_A), row_spec(W_B), row_spec(d_model),
                row_spec(d_model),
                _const_spec(wpa.shape), _const_spec(wpb.shape), _const_spec(wo.shape),
                _const_spec((1, d_model)), _const_spec(wup.shape), _const_spec(wdn.shape),
                _const_spec((1, d_model))]
    args = [x2d, oa, ob, ga, gb, wpa, wpb, wo, n2.reshape(1, d_model), wup, wdn,
            nf.reshape(1, d_model)]
    y_shape = jax.ShapeDtypeStruct((rows, d_model), F32)
    params = dict(dimension_semantics=("arbitrary",), vmem_limit_bytes=VMEM_LIMIT)
    ff_step = min(d_ff, 1024)
    if key_stream is None:
        return pl.pallas_call(
            functools.partial(_merge_mlp_kernel, ff_step=ff_step, stream=None),
            grid=(n_tiles,), in_specs=in_specs, out_specs=row_spec(d_model), out_shape=y_shape,
            compiler_params=pltpu.CompilerParams(**params),
        )(*args)

    page_ids, cache_t, ppb, pages_per_seq = key_stream
    total = page_ids.shape[0]
    _, _, _, page_size = cache_t.shape
    pages_per_step = total // n_tiles
    assert (pages_per_step * n_tiles == total and pages_per_step % KMEAN_SLOTS == 0
            and pages_per_seq % pages_per_step == 0 and KMEAN_GROUP % ppb == 0
            and KMEAN_SLOTS % KMEAN_GROUP == 0 and pages_per_seq // ppb <= LANES)
    km_shape = jax.ShapeDtypeStruct((total // pages_per_seq, W_B, LANES), F32)
    km_spec = pl.BlockSpec((1, W_B, LANES),
                           lambda i, *_: (i * pages_per_step // pages_per_seq, 0, 0))
    return pl.pallas_call(
        functools.partial(_merge_mlp_kernel, ff_step=ff_step,
                          stream=(pages_per_step, pages_per_seq, total, ppb, page_size)),
        grid_spec=pltpu.PrefetchScalarGridSpec(
            num_scalar_prefetch=1, grid=(n_tiles,),
            in_specs=in_specs + [pl.BlockSpec(memory_space=pl.ANY)],
            out_specs=[row_spec(d_model), km_spec],
            scratch_shapes=[pltpu.VMEM((KMEAN_SLOTS, H_B, HD_B, page_size), F32),
                            pltpu.SemaphoreType.DMA((KMEAN_SLOTS,))]),
        out_shape=[y_shape, km_shape],
        compiler_params=pltpu.CompilerParams(disable_bounds_checks=True, **params),
    )(page_ids, *args, cache_t)


def _delta_step_kernel(alog_ref, dtb_ref, x_ref, z_ref, lg_ref, sc_ref, cw_ref, sd_ref, nw_ref,
                       o_ref, snew_ref, cnew_ref):
    x = x_ref[0]
    hist = sc_ref[0]
    w = cw_ref[...]
    y = x * w[CONV_W - 1:CONV_W]
    for i in range(CONV_W - 1):
        y = y + hist[i:i + 1] * w[i:i + 1]
    y = _silu(y)
    cnew_ref[0, 0:CONV_W - 2, :] = hist[1:CONV_W - 1]
    cnew_ref[0, CONV_W - 2:CONV_W - 1, :] = x
    lg = lg_ref[0]
    zz = z_ref[0]
    nw = nw_ref[...]
    sq = (DK_A, DK_A)
    eye = _iota(sq, 0) == _iota(sq, 1)

    def l2n(t):
        return t * lax.rsqrt(jnp.sum(t * t, axis=-1, keepdims=True) + EPS)

    heads = range(H_A)
    q = [l2n(y[:, h * DK_A:(h + 1) * DK_A]) * (DK_A ** -0.5) for h in heads]
    k = [l2n(y[:, W_QK_A + h * DK_A:W_QK_A + (h + 1) * DK_A]) for h in heads]
    v = [y[:, 2 * W_QK_A + h * DV_A:2 * W_QK_A + (h + 1) * DV_A] for h in heads]
    beta = [_sigmoid(lg[:, h:h + 1]) for h in heads]
    g = [-jnp.exp(jnp.full((1, 1), alog_ref[h], F32))
         * _softplus(lg[:, H_A + h:H_A + h + 1] + dtb_ref[h]) for h in heads]
    state = [sd_ref[0, h] * jnp.exp(g[h]) for h in heads]
    kv = [_mm(jnp.broadcast_to(k[h], (SUBLANES, DK_A)), state[h], HIGHEST)[0:1] for h in heads]
    dv = [(v[h] - kv[h]) * beta[h] for h in heads]
    k_diag = [jnp.where(eye, jnp.broadcast_to(k[h], sq), 0.0) for h in heads]
    state = [state[h] + _mm(k_diag[h], jnp.broadcast_to(dv[h], (DK_A, DV_A)), HIGHEST)
             for h in heads]
    o = [_mm(jnp.broadcast_to(q[h], (SUBLANES, DK_A)), state[h], HIGHEST)[0:1] for h in heads]
    for h in heads:
        snew_ref[0, h] = state[h]
        gate = _silu(zz[:, h * DV_A:(h + 1) * DV_A])
        o_ref[0, :, h * DV_A:(h + 1) * DV_A] = _bf(_rms(o[h], nw) * gate)


def _delta_step(qkv, z, lg, state_conv, conv_w, state_delta, a_log, dt_bias, norm_w):
    nb = qkv.shape[0]
    smem = pl.BlockSpec(memory_space=pltpu.SMEM)

    def per_b(shape):
        nd = len(shape)
        return pl.BlockSpec((1,) + shape, lambda b: (b,) + (0,) * nd)

    return pl.pallas_call(
        _delta_step_kernel,
        grid=(nb,),
        in_specs=[smem, smem, per_b((1, C_CONV)), per_b((1, W_V_A)), per_b((1, LOGIT_PAD)),
                  per_b((CONV_W - 1, C_CONV)), _const_spec((CONV_W, C_CONV)),
                  per_b((H_A, DK_A, DV_A)), _const_spec((1, DV_A))],
        out_specs=[per_b((1, W_V_A)), per_b((H_A, DK_A, DV_A)), per_b((CONV_W - 1, C_CONV))],
        out_shape=[jax.ShapeDtypeStruct((nb, 1, W_V_A), BF16),
                   jax.ShapeDtypeStruct((nb, H_A, DK_A, DV_A), F32),
                   jax.ShapeDtypeStruct((nb, CONV_W - 1, C_CONV), F32)],
        compiler_params=pltpu.CompilerParams(dimension_semantics=("arbitrary",),
                                             vmem_limit_bytes=VMEM_LIMIT),
    )(a_log, dt_bias, qkv.reshape(nb, 1, C_CONV), z.reshape(nb, 1, W_V_A),
      lg.reshape(nb, 1, LOGIT_PAD), state_conv, conv_w, state_delta, norm_w.reshape(1, DV_A))


def _topk_kernel(qt_ref, km_ref, idx_ref, *, n_full, kk):
    n_seq = km_ref.shape[0]
    qt = qt_ref[...]
    per_seq = []
    for b in range(n_seq):
        prod = km_ref[b] * qt[:, b:b + 1]
        per_seq.append(jnp.concatenate(
            [jnp.sum(prod[h * HD_B:(h + 1) * HD_B], axis=0, keepdims=True) for h in range(H_B)],
            axis=0))
    sc = jnp.concatenate(per_seq, axis=0)
    blk = _iota(sc.shape, 1)
    sc = jnp.where(blk < n_full, sc, NEG_INF)
    rank = _block_rank(sc, n_full, 1)
    blk_f = blk.astype(F32)
    out = jnp.zeros(sc.shape, F32)
    for r in range(kk):
        pick = jnp.sum(jnp.where(rank == float(r), blk_f, 0.0), axis=-1, keepdims=True)
        out = jnp.where(blk == r, pick, out)
    out = out.astype(jnp.int32)
    for b in range(n_seq):
        idx_ref[b] = out[b * H_B:(b + 1) * H_B]


def _topk(q_t, km, *, n_full, kk):
    n_seq = km.shape[0]
    return pl.pallas_call(
        functools.partial(_topk_kernel, n_full=n_full, kk=kk),
        grid=(1,),
        in_specs=[_const_spec(q_t.shape), _const_spec(km.shape)],
        out_specs=pl.BlockSpec((n_seq, H_B, LANES), lambda i: (0, 0, 0)),
        out_shape=jax.ShapeDtypeStruct((n_seq, H_B, LANES), jnp.int32),
        compiler_params=pltpu.CompilerParams(dimension_semantics=("arbitrary",),
                                             vmem_limit_bytes=VMEM_LIMIT),
    )(q_t, km)


_PAGE_DMA_PARAMS = pltpu.CompilerParams(dimension_semantics=("arbitrary",),
                                        vmem_limit_bytes=VMEM_LIMIT, disable_bounds_checks=True)


def _moba_sample_kernel(top_ref, pt_ref, q_ref, kn_ref, vn_ref, ck_ref, cv_ref, o_ref,
                        kbuf, vbuf, sems, *, kk, ppb, n_pages, page_size, n_seq):
    b = pl.program_id(0)
    pages_per_head = kk * ppb

    def copies(seq, h, i, page=None):
        slot = seq % 2
        if page is None:
            blk = top_ref[(seq * H_B + h) * kk + i // ppb]
            page = pt_ref[seq * n_pages + blk * ppb + i % ppb]
        keys = pl.ds(i * page_size, page_size)
        return (pltpu.make_async_copy(ck_ref.at[page, h], kbuf.at[slot, h, :, keys],
                                      sems.at[slot, 0, h]),
                pltpu.make_async_copy(cv_ref.at[page, h], vbuf.at[slot, h, :, keys],
                                      sems.at[slot, 1, h]))

    def start_fetch(seq):
        for h in range(H_B):
            for i in range(pages_per_head):
                for c in copies(seq, h, i):
                    c.start()

    @pl.when(b == 0)
    def _():
        start_fetch(b)

    @pl.when(b + 1 < n_seq)
    def _():
        start_fetch(b + 1)

    slot = b % 2
    q = q_ref[0]
    kn = kn_ref[0]
    vn = vn_ref[0]
    heads = range(H_B)
    for h in heads:
        for i in range(pages_per_head):
            for c in copies(b, h, i, page=0):
                c.wait()
    sl = [slice(h * HD_B, (h + 1) * HD_B) for h in heads]
    qh = [q[:, sl[h]] * (HD_B ** -0.5) for h in heads]
    s = [_mm(_bf(jnp.broadcast_to(qh[h], (SUBLANES, HD_B))), _bf(kbuf[slot, h]))[0:1]
         for h in heads]
    s_own = [jnp.sum(qh[h] * kn[:, sl[h]], axis=-1, keepdims=True) for h in heads]
    m = [jnp.maximum(jnp.max(s[h], axis=-1, keepdims=True), s_own[h]) for h in heads]
    p = [jnp.exp(s[h] - m[h]) for h in heads]
    p_own = [jnp.exp(s_own[h] - m[h]) for h in heads]
    l = [jnp.sum(p[h], axis=-1, keepdims=True) + p_own[h] for h in heads]
    pv = [_mm_nt(_bf(jnp.broadcast_to(p[h], (SUBLANES, p[h].shape[1]))), _bf(vbuf[slot, h]))[0:1]
          for h in heads]
    for h in heads:
        o_ref[0, :, sl[h]] = _bf((pv[h] + p_own[h] * vn[:, sl[h]]) / l[h])


def _moba_sample(top_flat, pt_flat, q3, kn3, vn3, cache_kt, cache_vt, *, kk, ppb):
    nb = q3.shape[0]
    _, _, _, page_size = cache_kt.shape
    keys = kk * ppb * page_size
    row = pl.BlockSpec((1, 1, W_B), lambda b, *_: (b, 0, 0))
    any_spec = pl.BlockSpec(memory_space=pl.ANY)
    return pl.pallas_call(
        functools.partial(_moba_sample_kernel, kk=kk, ppb=ppb,
                          n_pages=pt_flat.shape[0] // nb, page_size=page_size, n_seq=nb),
        grid_spec=pltpu.PrefetchScalarGridSpec(
            num_scalar_prefetch=2,
            grid=(nb,),
            in_specs=[row, row, row, any_spec, any_spec],
            out_specs=row,
            scratch_shapes=[pltpu.VMEM((2, H_B, HD_B, keys), F32),
                            pltpu.VMEM((2, H_B, HD_B, keys), F32),
                            pltpu.SemaphoreType.DMA((2, 2, H_B))]),
        out_shape=jax.ShapeDtypeStruct((nb, 1, W_B), BF16),
        compiler_params=_PAGE_DMA_PARAMS,
    )(top_flat, pt_flat, q3, kn3, vn3, cache_kt, cache_vt)


def _rearranged_w_in(w_in, d_model):
    o_z = C_CONV
    o_lg = o_z + W_V_A
    o_b = o_lg + 2 * H_A
    o_g = o_b + 3 * W_B
    pad = jnp.zeros((d_model, LOGIT_PAD - 2 * H_A), w_in.dtype)
    return _bf(jnp.concatenate(
        [w_in[:, :o_lg], w_in[:, o_b:o_g + 2 * d_model], w_in[:, o_lg:o_b], pad], axis=1))


def kernel(x_prompt, x_sample, cache_k, cache_v, page_table, state_delta, state_conv, norm1_w,
           w_in, conv_w, a_log, dt_bias, delta_norm_w, w_proj_a, w_proj_b, w_out, norm2_w, w_up,
           w_down, norm_f_w):
    depth = w_in.shape[0]
    assert depth == 1, "single-layer trunk"
    bp, sp, d_model = x_prompt.shape
    bs, ss, _ = x_sample.shape
    assert ss == 1, "one new token per sample sequence"
    _, n_pool, page_size, _, _ = cache_k.shape
    n_pages = page_table.shape[1]
    past_len = n_pages * page_size
    ppb = MOBA_BLOCK // page_size
    n_full = past_len // MOBA_BLOCK
    assert n_full * MOBA_BLOCK == past_len, "the sample token starts a fresh MoBA block"
    kk = min(MOBA_TOPK, n_full)
    assert kk >= 1

    w_r = _rearranged_w_in(w_in[0], d_model)
    o_qb = C_CONV + W_V_A + 2 * H_A
    wqkv_t = _bf(w_in[0][:, o_qb:o_qb + 3 * W_B].T)
    wpa, wpb, wo = _bf(w_proj_a[0]), _bf(w_proj_b[0]), _bf(w_out[0])
    wup, wdn = _bf(w_up[0]), _bf(w_down[0])

    tm = 256
    pos_p = jnp.arange(sp, dtype=jnp.int32)
    (qkv, z, qt, kt, vt, ga, gb, lg, conv_p) = _in_proj(
        x_prompt.reshape(bp * sp, d_model), norm1_w[0], w_r, _rope_tables(pos_p),
        tm=tm, rows_per_seq=sp, wqkv_t=wqkv_t, tabs_t=_rope_tables_t(pos_p), conv_w=conv_w[0])
    o_a, s_p = _delta_prompt(qkv.reshape(bp, sp, C_CONV), z.reshape(bp, sp, W_V_A),
                             lg.reshape(bp, sp, LOGIT_PAD), a_log[0], dt_bias[0],
                             delta_norm_w[0])
    o_b = _moba_prompt(qt, kt, vt)
    cache_kt = cache_k.reshape(n_pool, page_size, H_B, HD_B).transpose(0, 2, 3, 1)
    cache_vt = cache_v.reshape(n_pool, page_size, H_B, HD_B).transpose(0, 2, 3, 1)
    past_pages = page_table[:, :n_full * ppb].reshape(-1)
    y_p, km = _merge_mlp(x_prompt.reshape(bp * sp, d_model), o_a.reshape(bp * sp, W_V_A),
                         o_b.reshape(bp * sp, W_B), ga, gb, wpa, wpb, wo, norm2_w[0], wup, wdn,
                         norm_f_w, tm=tm, key_stream=(past_pages, cache_kt, ppb, n_full * ppb))

    pos_s = jnp.full((bs,), past_len, dtype=jnp.int32)
    (qkv_s, z_s, qr_s, kr_s, vr_s, ga_s, gb_s, lg_s) = _in_proj(
        x_sample.reshape(bs, d_model), norm1_w[0], w_r, _rope_tables(pos_s),
        tm=bs, rows_per_seq=bs)
    o_a_s, s_s, conv_s = _delta_step(qkv_s, z_s, lg_s, state_conv[0], conv_w[0], state_delta[0],
                                     a_log[0], dt_bias[0], delta_norm_w[0])
    top = _topk(qr_s.T, km, n_full=n_full, kk=kk)
    top_flat = top[:, :, :kk].reshape(-1)
    o_b_s = _moba_sample(top_flat, page_table.reshape(-1), qr_s.reshape(bs, 1, W_B),
                         kr_s.reshape(bs, 1, W_B), vr_s.reshape(bs, 1, W_B), cache_kt, cache_vt,
                         kk=kk, ppb=ppb)
    y_s = _merge_mlp(x_sample.reshape(bs, d_model), o_a_s.reshape(bs, W_V_A),
                     o_b_s.reshape(bs, W_B), ga_s, gb_s, wpa, wpb, wo, norm2_w[0], wup, wdn,
                     norm_f_w, tm=bs)

    def kv_out(t):
        return t.reshape(1, bp, H_B, HD_B, sp).transpose(0, 1, 4, 2, 3)

    return (y_p.reshape(bp, sp, d_model), y_s.reshape(bs, ss, d_model),
            kv_out(kt), kv_out(vt),
            s_p.reshape(1, bp, H_A, DK_A, DV_A), conv_p.reshape(1, bp, CONV_W - 1, C_CONV),
            kr_s.reshape(1, bs, ss, H_B, HD_B), vr_s.reshape(1, bs, ss, H_B, HD_B),
            s_s.reshape(1, bs, H_A, DK_A, DV_A), conv_s.reshape(1, bs, CONV_W - 1, C_CONV))
```

```python
import functools
import math

import jax
import jax.numpy as jnp
from jax import lax
from jax.experimental import pallas as pl
from jax.experimental.pallas import tpu as pltpu

F32 = jnp.float32
BF16 = jnp.bfloat16
HIGHEST = lax.Precision.HIGHEST

H_A = 4
DK_A = 128
DV_A = 128
W_QK_A = H_A * DK_A
W_V_A = H_A * DV_A
C_CONV = 2 * W_QK_A + W_V_A
CONV_W = 4
H_B = 8
HD_B = 64
W_B = H_B * HD_B
MOBA_BLOCK = 256
MOBA_TOPK = 3
ROPE_THETA = 500000.0
ROT_DIM = HD_B // 4
EPS = 1e-6

LANES = 128
SUBLANES = 8
BF16_SUBLANES = 16
MOBA_HEADS_PER_STEP = 8
KMEAN_SLOTS = 64
KMEAN_GROUP = 16
CONV_ROWS = 64
HEADS_PER_LANE_TILE = LANES // HD_B
LOGIT_PAD = LANES
DELTA_CHUNK = LANES
DELTA_TILE = 4 * DELTA_CHUNK
PREP_CHUNKS = 4
VMEM_LIMIT = 56 * 1024 * 1024
NEG_INF = float("-inf")


def _dot(a, b, dims, precision=None):
    return lax.dot_general(a, b, (dims, ((), ())), precision=precision,
                           preferred_element_type=F32)


def _mm(a, b, precision=None):
    return _dot(a, b, ((1,), (0,)), precision)


def _mm_nt(a, b, precision=None):
    return _dot(a, b, ((1,), (1,)), precision)


def _bf(x):
    return x.astype(BF16)


def _sigmoid(x):
    return 1.0 / (1.0 + jnp.exp(-x))


def _silu(x):
    return x * _sigmoid(x)


def _softplus(x):
    return jnp.maximum(x, 0.0) + jnp.log(1.0 + jnp.exp(-jnp.abs(x)))


def _rms(x, w):
    return x * lax.rsqrt(jnp.mean(x * x, axis=-1, keepdims=True) + EPS) * w


def _iota(shape, dim):
    return lax.broadcasted_iota(jnp.int32, shape, dim)


def _const_spec(shape):
    nd = len(shape)
    return pl.BlockSpec(shape, lambda *_: (0,) * nd, pipeline_mode=pl.Buffered(1))


def _block_rank(sc, n_blocks, axis):
    blk = _iota(sc.shape, axis)
    rank = jnp.zeros(sc.shape, F32)
    for j in range(n_blocks):
        cj = sc[j:j + 1, :] if axis == 0 else sc[:, j:j + 1]
        first_on_tie = jnp.where(blk > j, 1.0, 0.0)
        rank = rank + jnp.where(cj > sc, 1.0, jnp.where(cj == sc, first_on_tie, 0.0))
    return rank


def _split_bf16(x):
    hi = _bf(x)
    return hi, _bf(x - hi.astype(F32))


def _mm_3pass(a, b):
    a_hi, a_lo = _split_bf16(a)
    b_hi, b_lo = _split_bf16(b)
    return _mm(jnp.concatenate([a_hi, a_lo, a_hi], axis=1),
               jnp.concatenate([b_hi, b_hi, b_lo], axis=0))


def _in_proj_kernel(x_ref, nw_ref, w_ref, cos_ref, slo_ref, shi_ref, *rest,
                    d_model, tiles_per_seq, kv_transposed):
    if kv_transposed:
        (wqkv_t_ref, cos_t_ref, sin_t_ref, cw_ref,
         qkv_ref, z_ref, qb_ref, kb_ref, vb_ref, ga_ref, gb_ref, lg_ref, conv_ref, hist_s) = rest
    else:
        qkv_ref, z_ref, qb_ref, kb_ref, vb_ref, ga_ref, gb_ref, lg_ref = rest
    step_i = pl.program_id(0)
    xb = _bf(_rms(x_ref[...], nw_ref[...]))
    tm = xb.shape[0]
    half = ROT_DIM // 2

    def conv_act_in_place():
        last = qkv_ref[tm - SUBLANES:tm, :]
        conv_ref[0] = last[SUBLANES - (CONV_W - 1):, :]
        for j in range(C_CONV // LANES):
            lanes = slice(j * LANES, (j + 1) * LANES)
            w = cw_ref[:, lanes]
            for rb in reversed(range(tm // CONV_ROWS)):
                r0 = rb * CONV_ROWS
                y = qkv_ref[r0:r0 + CONV_ROWS, lanes]
                prev = hist_s[:, lanes] if rb == 0 else qkv_ref[r0 - SUBLANES:r0, lanes]
                ext = jnp.concatenate([prev, y], axis=0)
                acc = y * w[CONV_W - 1:CONV_W]
                for i in range(CONV_W - 1):
                    lo = SUBLANES - (CONV_W - 1) + i
                    acc = acc + ext[lo:lo + CONV_ROWS] * w[i:i + 1]
                act = _silu(acc)
                if lanes.start < 2 * W_QK_A:
                    act = act * lax.rsqrt(jnp.sum(act * act, axis=-1, keepdims=True) + EPS)
                if lanes.start < W_QK_A:
                    act = act * (DK_A ** -0.5)
                qkv_ref[r0:r0 + CONV_ROWS, lanes] = act
        hist_s[...] = last

    def cols(c0, n):
        return _mm(xb, w_ref[:, c0:c0 + n])

    def rope_store(dst_ref, c0):
        cos, slo, shi = cos_ref[...], slo_ref[...], shi_ref[...]
        for c in range(W_B // LANES):
            y = cols(c0 + c * LANES, LANES)
            up = pltpu.roll(y, LANES - half, 1)
            dn = pltpu.roll(y, half, 1)
            dst_ref[:, c * LANES:(c + 1) * LANES] = y * cos + up * slo + dn * shi

    def rope_store_t(dst_ref, r0):
        cos_t, sin_t = cos_t_ref[...], sin_t_ref[...]
        y_t = _mm_nt(wqkv_t_ref[r0:r0 + W_B, :], xb)
        for h in range(H_B):
            r = h * HD_B
            x1, x2 = y_t[r:r + half], y_t[r + half:r + ROT_DIM]
            dst_ref[0, r:r + half, :] = x1 * cos_t - x2 * sin_t
            dst_ref[0, r + half:r + ROT_DIM, :] = x2 * cos_t + x1 * sin_t
            dst_ref[0, r + ROT_DIM:r + HD_B, :] = y_t[r + ROT_DIM:r + HD_B]

    if kv_transposed:
        @pl.when(step_i % tiles_per_seq == 0)
        def _():
            hist_s[...] = jnp.zeros_like(hist_s)

    step = 512
    c0 = 0
    for c in range(C_CONV // step):
        qkv_ref[:, c0:c0 + step] = cols(c0, step)
        c0 += step
    if kv_transposed:
        conv_act_in_place()
    z_ref[...] = cols(c0, W_V_A)
    c0 += W_V_A
    if kv_transposed:
        rope_store_t(qb_ref, 0)
        rope_store_t(kb_ref, W_B)
        vb_ref[0] = _mm_nt(wqkv_t_ref[2 * W_B:3 * W_B, :], xb)
    else:
        rope_store(qb_ref, c0)
        rope_store(kb_ref, c0 + W_B)
        vb_ref[...] = cols(c0 + 2 * W_B, W_B)
    c0 += 3 * W_B
    for g_ref in (ga_ref, gb_ref):
        for c in range(d_model // step):
            g_ref[:, c * step:(c + 1) * step] = cols(c0, step)
            c0 += step
    lg_ref[...] = cols(c0, LOGIT_PAD)


def _in_proj(x2d, norm_w, w_r, tabs, *, tm, rows_per_seq, wqkv_t=None, tabs_t=None, conv_w=None):
    rows, d_model = x2d.shape
    n_tiles = rows // tm
    tiles_per_seq = rows_per_seq // tm
    n_seq = rows // rows_per_seq
    kv_transposed = wqkv_t is not None

    def row_spec(n):
        return pl.BlockSpec((tm, n), lambda i: (i, 0))

    def sds(*shape):
        return jax.ShapeDtypeStruct(shape, F32)

    kv_t_spec = pl.BlockSpec((1, W_B, tm), lambda i: (i // tiles_per_seq, 0, i % tiles_per_seq))
    kv_shape = sds(n_seq, W_B, rows_per_seq) if kv_transposed else sds(rows, W_B)
    kv_spec = kv_t_spec if kv_transposed else row_spec(W_B)
    tab_spec = pl.BlockSpec((tm, LANES), lambda i: (i % tiles_per_seq, 0))
    in_specs = [row_spec(d_model), _const_spec((1, d_model)), _const_spec(w_r.shape),
                tab_spec, tab_spec, tab_spec]
    args = [x2d, norm_w.reshape(1, d_model), w_r, *tabs]
    out_shape = [sds(rows, C_CONV), sds(rows, W_V_A), kv_shape, kv_shape, kv_shape,
                 sds(rows, d_model), sds(rows, d_model), sds(rows, LOGIT_PAD)]
    out_specs = [row_spec(C_CONV), row_spec(W_V_A), kv_spec, kv_spec, kv_spec,
                 row_spec(d_model), row_spec(d_model), row_spec(LOGIT_PAD)]
    if kv_transposed:
        tab_t_spec = pl.BlockSpec((ROT_DIM // 2, tm), lambda i: (0, i % tiles_per_seq))
        in_specs += [_const_spec(wqkv_t.shape), tab_t_spec, tab_t_spec, _const_spec(conv_w.shape)]
        args += [wqkv_t, *tabs_t, conv_w]
        out_shape.append(sds(n_seq, CONV_W - 1, C_CONV))
        out_specs.append(pl.BlockSpec((1, CONV_W - 1, C_CONV),
                                      lambda i: (i // tiles_per_seq, 0, 0)))
        assert tm >= SUBLANES
        scratch = [pltpu.VMEM((SUBLANES, C_CONV), F32)]
    else:
        scratch = []
    return pl.pallas_call(
        functools.partial(_in_proj_kernel, d_model=d_model, tiles_per_seq=tiles_per_seq,
                          kv_transposed=kv_transposed),
        grid=(n_tiles,),
        in_specs=in_specs,
        out_specs=out_specs,
        out_shape=out_shape,
        scratch_shapes=scratch,
        compiler_params=pltpu.CompilerParams(dimension_semantics=("arbitrary",),
                                             vmem_limit_bytes=VMEM_LIMIT),
    )(*args)


def _rope_angles(pos):
    half = ROT_DIM // 2
    inv_freq = jnp.exp(jnp.arange(half, dtype=F32) * (-2.0 * math.log(ROPE_THETA) / ROT_DIM))
    ang = pos.astype(F32)[:, None] * inv_freq[None, :]
    return jnp.cos(ang), jnp.sin(ang)


def _rope_tables_t(pos):
    cos, sin = _rope_angles(pos)
    return cos.T, sin.T


def _rope_tables(pos):
    half = ROT_DIM // 2
    cos, sin = _rope_angles(pos)
    n = pos.shape[0]
    ones = jnp.ones((n, HD_B - ROT_DIM), F32)
    zeros_h = jnp.zeros((n, half), F32)
    zeros_r = jnp.zeros((n, HD_B - ROT_DIM), F32)
    cos_h = jnp.concatenate([cos, cos, ones], axis=1)
    slo_h = jnp.concatenate([-sin, zeros_h, zeros_r], axis=1)
    shi_h = jnp.concatenate([zeros_h, sin, zeros_r], axis=1)
    tile = lambda t: jnp.tile(t, (1, HEADS_PER_LANE_TILE))
    return tile(cos_h), tile(slo_h), tile(shi_h)


def _delta_prompt_kernel(alog_ref, dtb_ref, x_ref, z_ref, lg_ref, nw_ref,
                         o_ref, s_ref,
                         state_s, u_s, w_s, qg_s, qk_s, kdt_s, gl_s):
    t = pl.program_id(1)
    cs = DELTA_CHUNK
    n_chunks = x_ref.shape[1] // cs
    heads = range(H_A)

    @pl.when(t == 0)
    def _():
        state_s[...] = jnp.zeros_like(state_s)

    sq = (cs, cs)
    row = _iota(sq, 0)
    col = _iota(sq, 1)
    causal = col <= row
    strict = col < row
    tril_b = _bf(jnp.where(causal, 1.0, 0.0))
    eye_f = jnp.where(col == row, 1.0, 0.0)
    lane_row = _iota((1, LANES), 1)
    alog_row = jnp.zeros((1, LANES), F32)
    dtb_row = jnp.zeros((1, LANES), F32)
    for h in heads:
        alog_row = jnp.where(lane_row == H_A + h, alog_ref[h], alog_row)
        dtb_row = jnp.where(lane_row == H_A + h, dtb_ref[h], dtb_row)
    neg_a_row = -jnp.exp(alog_row)

    def prep(ci, carry):
        cidx = [ci * PREP_CHUNKS + d for d in range(PREP_CHUNKS)]
        rows = [pl.ds(pl.multiple_of(c * cs, cs), cs) for c in cidx]
        units = [(d, h) for d in range(PREP_CHUNKS) for h in heads]
        un = range(len(units))
        q = [x_ref[0, rows[d], h * DK_A:(h + 1) * DK_A] for d, h in units]
        k = [x_ref[0, rows[d], W_QK_A + h * DK_A:W_QK_A + (h + 1) * DK_A] for d, h in units]
        v = [x_ref[0, rows[d], 2 * W_QK_A + h * DV_A:2 * W_QK_A + (h + 1) * DV_A] for d, h in units]
        sig, gc_all = [], []
        for d in range(PREP_CHUNKS):
            lg = lg_ref[0, rows[d], :]
            sig.append(_sigmoid(lg))
            rem = neg_a_row * _softplus(lg + dtb_row)
            pieces = []
            for _ in range(3):
                piece = _bf(rem)
                pieces.append(piece)
                rem = rem - piece.astype(F32)
            csum = _mm(tril_b, jnp.concatenate(pieces, axis=1))
            gc_all.append(csum[:, :LANES] + csum[:, LANES:2 * LANES] + csum[:, 2 * LANES:])
        beta = [sig[d][:, h:h + 1] for d, h in units]
        gc = [jnp.broadcast_to(gc_all[d][:, H_A + h:H_A + h + 1], (cs, LANES)) for d, h in units]
        decay = [jnp.exp(jnp.where(causal, gc[u] - gc[u].T, NEG_INF)) for u in un]
        kb = [k[u] * beta[u] for u in un]
        a = [jnp.where(strict, _mm_nt(_bf(kb[u]), _bf(k[u])) * decay[u], 0.0) for u in un]
        t_inv = [eye_f - a[u] for u in un]
        a_pow = [_mm_3pass(a[u], a[u]) for u in un]
        n_levels = int(math.log2(cs)) - 1
        for lvl in range(n_levels):
            if lvl < n_levels - 1:
                prod = [_mm_3pass(a_pow[u], jnp.concatenate([a_pow[u], t_inv[u]], axis=1))
                        for u in un]
                a_pow = [prod[u][:, :cs] for u in un]
                t_inv = [t_inv[u] + prod[u][:, cs:] for u in un]
            else:
                t_inv = [t_inv[u] + _mm_3pass(a_pow[u], t_inv[u]) for u in un]
        e_gc = [jnp.exp(gc[u]) for u in un]
        uw = [_mm(_bf(t_inv[u]),
                  jnp.concatenate([_bf(v[u] * beta[u]), _bf(kb[u] * e_gc[u])], axis=1))
              for u in un]
        qk = [jnp.where(causal, _mm_nt(_bf(q[u]), _bf(k[u])) * decay[u], 0.0) for u in un]
        for u, (d, h) in enumerate(units):
            g_last = gc[u][cs - 1:cs, :]
            u_s[h, rows[d], :] = uw[u][:, :DV_A]
            w_s[h, rows[d], :] = _bf(uw[u][:, DV_A:])
            qk_s[h, rows[d], :] = _bf(qk[u])
            qg_s[h, rows[d], :] = _bf(q[u] * e_gc[u])
            kdt_s[h, cidx[d]] = _bf((k[u] * jnp.exp(g_last - gc[u])).T)
            gl_s[h, cidx[d]] = jnp.broadcast_to(jnp.exp(g_last), (SUBLANES, LANES))
        return carry

    lax.fori_loop(0, n_chunks // PREP_CHUNKS, prep, 0)

    nw = nw_ref[...]

    def scan(c, carry):
        rows = pl.ds(pl.multiple_of(c * cs, cs), cs)
        state = [state_s[h] for h in heads]
        sb = [_bf(state[h]) for h in heads]
        ws = [_mm(jnp.concatenate([w_s[h, rows, :], qg_s[h, rows, :]], axis=0), sb[h])
              for h in heads]
        vb = [_bf(u_s[h, rows, :] - ws[h][:cs]) for h in heads]
        o = [ws[h][cs:] + _mm(qk_s[h, rows, :], vb[h]) for h in heads]
        new = [state[h] * gl_s[h, c][0:1] + _mm(kdt_s[h, c], vb[h]) for h in heads]
        for h in heads:
            state_s[h] = new[h]
            lanes = slice(h * DV_A, (h + 1) * DV_A)
            o_ref[0, rows, lanes] = _bf(_rms(o[h], nw) * _silu(z_ref[0, rows, lanes]))
        return carry

    lax.fori_loop(0, n_chunks, scan, 0)

    @pl.when(t == pl.num_programs(1) - 1)
    def _():
        s_ref[0] = state_s[...]


def _delta_prompt(qkv, z, lg, a_log, dt_bias, norm_w):
    bsz, seq, _ = qkv.shape
    ts = min(DELTA_TILE, seq)
    n_chunks = ts // DELTA_CHUNK
    assert seq % ts == 0 and ts % (DELTA_CHUNK * PREP_CHUNKS) == 0
    assert DK_A == LANES and DV_A == LANES

    def tile_spec(n):
        return pl.BlockSpec((1, ts, n), lambda b, t: (b, t, 0))

    def per_head(shape, dtype):
        return pltpu.VMEM((H_A,) + shape, dtype)

    smem = pl.BlockSpec(memory_space=pltpu.SMEM)
    return pl.pallas_call(
        _delta_prompt_kernel,
        grid=(bsz, seq // ts),
        in_specs=[smem, smem, tile_spec(C_CONV), tile_spec(W_V_A), tile_spec(LOGIT_PAD),
                  _const_spec((1, DV_A))],
        out_specs=[tile_spec(W_V_A),
                   pl.BlockSpec((1, H_A, DK_A, DV_A), lambda b, t: (b, 0, 0, 0))],
        out_shape=[jax.ShapeDtypeStruct((bsz, seq, W_V_A), BF16),
                   jax.ShapeDtypeStruct((bsz, H_A, DK_A, DV_A), F32)],
        scratch_shapes=[per_head((DK_A, DV_A), F32),
                        per_head((ts, DV_A), F32),
                        per_head((ts, DK_A), BF16),
                        per_head((ts, DK_A), BF16),
                        per_head((ts, DELTA_CHUNK), BF16),
                        per_head((n_chunks, DK_A, DELTA_CHUNK), BF16),
                        per_head((n_chunks, SUBLANES, LANES), F32)],
        compiler_params=pltpu.CompilerParams(dimension_semantics=("arbitrary", "arbitrary"),
                                             vmem_limit_bytes=VMEM_LIMIT),
    )(a_log, dt_bias, qkv, z, lg, norm_w.reshape(1, DV_A))


def _moba_prompt_kernel(qt_ref, kt_ref, vt_ref, o_ref, kn_s, vt_s, km_s, ch_s, *, n_blocks):
    qi = pl.program_id(2)
    bs = MOBA_BLOCK
    n_pairs = qt_ref.shape[1] // LANES
    heads = range(n_pairs * HEADS_PER_LANE_TILE)
    pair_of = [h // HEADS_PER_LANE_TILE for h in heads]

    @pl.when(qi == 0)
    def _():
        km_s[...] = jnp.zeros_like(km_s)
        ones = jnp.ones((BF16_SUBLANES, bs), BF16)
        for j in range(n_blocks):
            keys = slice(j * bs, (j + 1) * bs)
            for pr in range(n_pairs):
                kj = kt_ref[0, pr * LANES:(pr + 1) * LANES, keys].T
                kn_s[j, pr] = _bf(kj)
                km_s[pr, j:j + 1, :] = jnp.mean(kj, axis=0, keepdims=True)
            for h in heads:
                vt_s[j, h, 0:HD_B, :] = _bf(vt_ref[0, h * HD_B:(h + 1) * HD_B, keys])
                vt_s[j, h, HD_B:, :] = ones

    d_row = _iota((LANES, bs), 0)
    blk = _iota((km_s.shape[1], bs), 0)
    key_le_query = _iota((bs, bs), 0) <= _iota((bs, bs), 1)
    qh = []
    for h in heads:
        hh = h % HEADS_PER_LANE_TILE
        qt = qt_ref[0, pair_of[h] * LANES:(pair_of[h] + 1) * LANES, :]
        qh.append(jnp.where((d_row >= hh * HD_B) & (d_row < (hh + 1) * HD_B), qt, 0.0))
    sc = [jnp.where(blk < qi, _mm_3pass(km_s[pair_of[h]], qh[h]), NEG_INF) for h in heads]
    rank = [_block_rank(sc[h], n_blocks, 0) for h in heads]
    for h in heads:
        ch_s[h] = jnp.where(blk < qi, jnp.where(rank[h] < MOBA_TOPK, 1.0, 0.0), 0.0)

    qs = [_bf(qh[h] * (HD_B ** -0.5 * math.log2(math.e))) for h in heads]
    s = [jnp.where(key_le_query, _mm(kn_s[qi, pair_of[h]], qs[h]), NEG_INF)
         for h in heads]
    m = [jnp.max(s[h], axis=0, keepdims=True) for h in heads]
    acc = [_mm(vt_s[qi, h], _bf(jnp.exp2(s[h] - m[h]))) for h in heads]

    def two_past_blocks(jj, carry):
        m, acc = carry
        js = (2 * jj, 2 * jj + 1)
        s = [[jnp.where(ch_s[h, pl.ds(j, 1), :] > 0.0, _mm(kn_s[j, pair_of[h]], qs[h]), NEG_INF)
              for j in js] for h in heads]
        m_new = [jnp.maximum(m[h], jnp.maximum(jnp.max(s[h][0], axis=0, keepdims=True),
                                               jnp.max(s[h][1], axis=0, keepdims=True)))
                 for h in heads]
        acc = [jnp.exp2(m[h] - m_new[h]) * acc[h]
               + _mm(vt_s[js[0], h], _bf(jnp.exp2(s[h][0] - m_new[h])))
               + _mm(vt_s[js[1], h], _bf(jnp.exp2(s[h][1] - m_new[h]))) for h in heads]
        return m_new, acc

    m, acc = lax.fori_loop(0, (qi + 1) // 2, two_past_blocks, (m, acc))
    o_t = jnp.concatenate([acc[h][:HD_B] / acc[h][HD_B:HD_B + 1] for h in heads], axis=0)
    o_ref[0] = _bf(o_t.T)


def _moba_prompt(qt, kt, vt):
    bsz, _, seq = qt.shape
    n_blocks = seq // MOBA_BLOCK
    assert seq % MOBA_BLOCK == 0
    blocks_pad = -(-n_blocks // SUBLANES) * SUBLANES
    n_pairs = MOBA_HEADS_PER_STEP // HEADS_PER_LANE_TILE
    rows = n_pairs * LANES
    q_spec = pl.BlockSpec((1, rows, MOBA_BLOCK), lambda b, p, i: (b, p, i))
    kv_spec = pl.BlockSpec((1, rows, seq), lambda b, p, i: (b, p, 0))
    return pl.pallas_call(
        functools.partial(_moba_prompt_kernel, n_blocks=n_blocks),
        grid=(bsz, H_B // MOBA_HEADS_PER_STEP, n_blocks),
        in_specs=[q_spec, kv_spec, kv_spec],
        out_specs=pl.BlockSpec((1, MOBA_BLOCK, rows), lambda b, p, i: (b, i, p)),
        out_shape=jax.ShapeDtypeStruct((bsz, seq, W_B), BF16),
        scratch_shapes=[pltpu.VMEM((n_blocks, n_pairs, MOBA_BLOCK, LANES), BF16),
                        pltpu.VMEM((n_blocks, MOBA_HEADS_PER_STEP, HD_B + BF16_SUBLANES,
                                    MOBA_BLOCK), BF16),
                        pltpu.VMEM((n_pairs, blocks_pad, LANES), F32),
                        pltpu.VMEM((MOBA_HEADS_PER_STEP, blocks_pad, MOBA_BLOCK), F32)],
        compiler_params=pltpu.CompilerParams(
            dimension_semantics=("arbitrary", "arbitrary", "arbitrary"),
            vmem_limit_bytes=VMEM_LIMIT),
    )(qt, kt, vt)


def _key_mean_groups(pt_ref, cache_ref, km_ref, page_buf, page_sems, stream):
    pages_per_step, pages_per_seq, total, ppb, page_size = stream
    step_i = pl.program_id(0)
    base = step_i * pages_per_step
    blocks_per_group = KMEAN_GROUP // ppb
    lane = _iota((HD_B, LANES), 1)

    def page_copy(p, slot):
        return pltpu.make_async_copy(cache_ref.at[pt_ref[p]], page_buf.at[slot], page_sems.at[slot])

    @pl.when(step_i == 0)
    def _():
        for s in range(KMEAN_SLOTS):
            page_copy(s, s).start()

    @pl.when(base % pages_per_seq == 0)
    def _():
        km_ref[...] = jnp.zeros_like(km_ref)

    def consume_group(g):
        p0 = base + g * KMEAN_GROUP
        slot0 = (g * KMEAN_GROUP) % KMEAN_SLOTS
        for j in range(KMEAN_GROUP):
            page_copy(p0 + j, slot0 + j).wait()
        blk0 = (base % pages_per_seq) // ppb + g * blocks_per_group
        for h in range(H_B):
            acc = jnp.zeros((HD_B, LANES), F32)
            for k in range(blocks_per_group):
                x = page_buf[slot0 + k * ppb, h]
                for r in range(1, ppb):
                    x = x + page_buf[slot0 + k * ppb + r, h]
                mean = jnp.sum(x, axis=-1, keepdims=True) * (1.0 / (ppb * page_size))
                acc = jnp.where(lane == blk0 + k, mean, acc)
            rows = slice(h * HD_B, (h + 1) * HD_B)
            km_ref[0, rows, :] = km_ref[0, rows, :] + acc
        for j in range(KMEAN_GROUP):
            @pl.when(p0 + j + KMEAN_SLOTS < total)
            def _():
                page_copy(p0 + j + KMEAN_SLOTS, slot0 + j).start()

    return [functools.partial(consume_group, g) for g in range(pages_per_step // KMEAN_GROUP)]


def _merge_mlp_kernel(*refs, ff_step, stream):
    if stream is None:
        (x_ref, oa_ref, ob_ref, ga_ref, gb_ref, wpa_ref, wpb_ref, wo_ref, n2_ref, wup_ref, wdn_ref,
         nf_ref, y_ref) = refs
        groups = []
    else:
        (pt_ref, x_ref, oa_ref, ob_ref, ga_ref, gb_ref, wpa_ref, wpb_ref, wo_ref, n2_ref, wup_ref,
         wdn_ref, nf_ref, cache_ref, y_ref, km_ref, page_buf, page_sems) = refs
        groups = _key_mean_groups(pt_ref, cache_ref, km_ref, page_buf, page_sems, stream)
    d_ff = wup_ref.shape[1]
    n_stages = 1 + d_ff // ff_step

    def run_groups(stage):
        for g in groups[stage * len(groups) // n_stages:(stage + 1) * len(groups) // n_stages]:
            g()

    y_a = _mm(oa_ref[...], wpa_ref[...])
    y_b = _mm(ob_ref[...], wpb_ref[...])
    mixed = _sigmoid(ga_ref[...]) * y_a + _sigmoid(gb_ref[...]) * y_b
    h = x_ref[...] + _mm(_bf(mixed), wo_ref[...])
    hn = _bf(_rms(h, n2_ref[...]))
    run_groups(0)
    out = h
    for c in range(d_ff // ff_step):
        u = jnp.maximum(_mm(hn, wup_ref[:, c * ff_step:(c + 1) * ff_step]), 0.0)
        out = out + _mm(_bf(u * u), wdn_ref[c * ff_step:(c + 1) * ff_step, :])
        run_groups(1 + c)
    y_ref[...] = _rms(out, nf_ref[...])


def _merge_mlp(x2d, oa, ob, ga, gb, wpa, wpb, wo, n2, wup, wdn, nf, *, tm, key_stream=None):
    rows, d_model = x2d.shape
    d_ff = wup.shape[1]
    n_tiles = rows // tm

    def row_spec(n):
        return pl.BlockSpec((tm, n), lambda i, *_: (i, 0))

    in_specs = [row_spec(d_model), row_spec(W_V_A), row_spec(W_B), row_spec(d_model),
                row_spec(d_model),
                _const_spec(wpa.shape), _const_spec(wpb.shape), _const_spec(wo.shape),
                _const_spec((1, d_model)), _const_spec(wup.shape), _const_spec(wdn.shape),
                _const_spec((1, d_model))]
    args = [x2d, oa, ob, ga, gb, wpa, wpb, wo, n2.reshape(1, d_model), wup, wdn,
            nf.reshape(1, d_model)]
    y_shape = jax.ShapeDtypeStruct((rows, d_model), F32)
    params = dict(dimension_semantics=("arbitrary",), vmem_limit_bytes=VMEM_LIMIT)
    ff_step = min(d_ff, 1024)
    if key_stream is None:
        return pl.pallas_call(
            functools.partial(_merge_mlp_kernel, ff_step=ff_step, stream=None),
            grid=(n_tiles,), in_specs=in_specs, out_specs=row_spec(d_model), out_shape=y_shape,
            compiler_params=pltpu.CompilerParams(**params),
        )(*args)

    page_ids, cache_t, ppb, pages_per_seq = key_stream
    total = page_ids.shape[0]
    _, _, _, page_size = cache_t.shape
    pages_per_step = total // n_tiles
    assert (pages_per_step * n_tiles == total and pages_per_step % KMEAN_SLOTS == 0
            and pages_per_seq % pages_per_step == 0 and KMEAN_GROUP % ppb == 0
            and KMEAN_SLOTS % KMEAN_GROUP == 0 and pages_per_seq // ppb <= LANES)
    km_shape = jax.ShapeDtypeStruct((total // pages_per_seq, W_B, LANES), F32)
    km_spec = pl.BlockSpec((1, W_B, LANES),
                           lambda i, *_: (i * pages_per_step // pages_per_seq, 0, 0))
    return pl.pallas_call(
        functools.partial(_merge_mlp_kernel, ff_step=ff_step,
                          stream=(pages_per_step, pages_per_seq, total, ppb, page_size)),
        grid_spec=pltpu.PrefetchScalarGridSpec(
            num_scalar_prefetch=1, grid=(n_tiles,),
            in_specs=in_specs + [pl.BlockSpec(memory_space=pl.ANY)],
            out_specs=[row_spec(d_model), km_spec],
            scratch_shapes=[pltpu.VMEM((KMEAN_SLOTS, H_B, HD_B, page_size), F32),
                            pltpu.SemaphoreType.DMA((KMEAN_SLOTS,))]),
        out_shape=[y_shape, km_shape],
        compiler_params=pltpu.CompilerParams(disable_bounds_checks=True, **params),
    )(page_ids, *args, cache_t)


def _delta_step_kernel(alog_ref, dtb_ref, x_ref, z_ref, lg_ref, sc_ref, cw_ref, sd_ref, nw_ref,
                       o_ref, snew_ref, cnew_ref):
    x = x_ref[0]
    hist = sc_ref[0]
    w = cw_ref[...]
    y = x * w[CONV_W - 1:CONV_W]
    for i in range(CONV_W - 1):
        y = y + hist[i:i + 1] * w[i:i + 1]
    y = _silu(y)
    cnew_ref[0, 0:CONV_W - 2, :] = hist[1:CONV_W - 1]
    cnew_ref[0, CONV_W - 2:CONV_W - 1, :] = x
    lg = lg_ref[0]
    zz = z_ref[0]
    nw = nw_ref[...]
    sq = (DK_A, DK_A)
    eye = _iota(sq, 0) == _iota(sq, 1)

    def l2n(t):
        return t * lax.rsqrt(jnp.sum(t * t, axis=-1, keepdims=True) + EPS)

    heads = range(H_A)
    q = [l2n(y[:, h * DK_A:(h + 1) * DK_A]) * (DK_A ** -0.5) for h in heads]
    k = [l2n(y[:, W_QK_A + h * DK_A:W_QK_A + (h + 1) * DK_A]) for h in heads]
    v = [y[:, 2 * W_QK_A + h * DV_A:2 * W_QK_A + (h + 1) * DV_A] for h in heads]
    beta = [_sigmoid(lg[:, h:h + 1]) for h in heads]
    g = [-jnp.exp(jnp.full((1, 1), alog_ref[h], F32))
         * _softplus(lg[:, H_A + h:H_A + h + 1] + dtb_ref[h]) for h in heads]
    state = [sd_ref[0, h] * jnp.exp(g[h]) for h in heads]
    kv = [_mm(jnp.broadcast_to(k[h], (SUBLANES, DK_A)), state[h], HIGHEST)[0:1] for h in heads]
    dv = [(v[h] - kv[h]) * beta[h] for h in heads]
    k_diag = [jnp.where(eye, jnp.broadcast_to(k[h], sq), 0.0) for h in heads]
    state = [state[h] + _mm(k_diag[h], jnp.broadcast_to(dv[h], (DK_A, DV_A)), HIGHEST)
             for h in heads]
    o = [_mm(jnp.broadcast_to(q[h], (SUBLANES, DK_A)), state[h], HIGHEST)[0:1] for h in heads]
    for h in heads:
        snew_ref[0, h] = state[h]
        gate = _silu(zz[:, h * DV_A:(h + 1) * DV_A])
        o_ref[0, :, h * DV_A:(h + 1) * DV_A] = _bf(_rms(o[h], nw) * gate)


def _delta_step(qkv, z, lg, state_conv, conv_w, state_delta, a_log, dt_bias, norm_w):
    nb = qkv.shape[0]
    smem = pl.BlockSpec(memory_space=pltpu.SMEM)

    def per_b(shape):
        nd = len(shape)
        return pl.BlockSpec((1,) + shape, lambda b: (b,) + (0,) * nd)

    return pl.pallas_call(
        _delta_step_kernel,
        grid=(nb,),
        in_specs=[smem, smem, per_b((1, C_CONV)), per_b((1, W_V_A)), per_b((1, LOGIT_PAD)),
                  per_b((CONV_W - 1, C_CONV)), _const_spec((CONV_W, C_CONV)),
                  per_b((H_A, DK_A, DV_A)), _const_spec((1, DV_A))],
        out_specs=[per_b((1, W_V_A)), per_b((H_A, DK_A, DV_A)), per_b((CONV_W - 1, C_CONV))],
        out_shape=[jax.ShapeDtypeStruct((nb, 1, W_V_A), BF16),
                   jax.ShapeDtypeStruct((nb, H_A, DK_A, DV_A), F32),
                   jax.ShapeDtypeStruct((nb, CONV_W - 1, C_CONV), F32)],
        compiler_params=pltpu.CompilerParams(dimension_semantics=("arbitrary",),
                                             vmem_limit_bytes=VMEM_LIMIT),
    )(a_log, dt_bias, qkv.reshape(nb, 1, C_CONV), z.reshape(nb, 1, W_V_A),
      lg.reshape(nb, 1, LOGIT_PAD), state_conv, conv_w, state_delta, norm_w.reshape(1, DV_A))


def _topk_kernel(qt_ref, km_ref, idx_ref, *, n_full, kk):
    n_seq = km_ref.shape[0]
    qt = qt_ref[...]
    per_seq = []
    for b in range(n_seq):
        prod = km_ref[b] * qt[:, b:b + 1]
        per_seq.append(jnp.concatenate(
            [jnp.sum(prod[h * HD_B:(h + 1) * HD_B], axis=0, keepdims=True) for h in range(H_B)],
            axis=0))
    sc = jnp.concatenate(per_seq, axis=0)
    blk = _iota(sc.shape, 1)
    sc = jnp.where(blk < n_full, sc, NEG_INF)
    rank = _block_rank(sc, n_full, 1)
    blk_f = blk.astype(F32)
    out = jnp.zeros(sc.shape, F32)
    for r in range(kk):
        pick = jnp.sum(jnp.where(rank == float(r), blk_f, 0.0), axis=-1, keepdims=True)
        out = jnp.where(blk == r, pick, out)
    out = out.astype(jnp.int32)
    for b in range(n_seq):
        idx_ref[b] = out[b * H_B:(b + 1) * H_B]


def _topk(q_t, km, *, n_full, kk):
    n_seq = km.shape[0]
    return pl.pallas_call(
        functools.partial(_topk_kernel, n_full=n_full, kk=kk),
        grid=(1,),
        in_specs=[_const_spec(q_t.shape), _const_spec(km.shape)],
        out_specs=pl.BlockSpec((n_seq, H_B, LANES), lambda i: (0, 0, 0)),
        out_shape=jax.ShapeDtypeStruct((n_seq, H_B, LANES), jnp.int32),
        compiler_params=pltpu.CompilerParams(dimension_semantics=("arbitrary",),
                                             vmem_limit_bytes=VMEM_LIMIT),
    )(q_t, km)


_PAGE_DMA_PARAMS = pltpu.CompilerParams(dimension_semantics=("arbitrary",),
                                        vmem_limit_bytes=VMEM_LIMIT, disable_bounds_checks=True)


def _moba_sample_kernel(top_ref, pt_ref, q_ref, kn_ref, vn_ref, ck_ref, cv_ref, o_ref,
                        kbuf, vbuf, sems, *, kk, ppb, n_pages, page_size, n_seq):
    b = pl.program_id(0)
    pages_per_head = kk * ppb

    def copies(seq, h, i, page=None):
        slot = seq % 2
        if page is None:
            blk = top_ref[(seq * H_B + h) * kk + i // ppb]
            page = pt_ref[seq * n_pages + blk * ppb + i % ppb]
        keys = pl.ds(i * page_size, page_size)
        return (pltpu.make_async_copy(ck_ref.at[page, h], kbuf.at[slot, h, :, keys],
                                      sems.at[slot, 0, h]),
                pltpu.make_async_copy(cv_ref.at[page, h], vbuf.at[slot, h, :, keys],
                                      sems.at[slot, 1, h]))

    def start_fetch(seq):
        for h in range(H_B):
            for i in range(pages_per_head):
                for c in copies(seq, h, i):
                    c.start()

    @pl.when(b == 0)
    def _():
        start_fetch(b)

    @pl.when(b + 1 < n_seq)
    def _():
        start_fetch(b + 1)

    slot = b % 2
    q = q_ref[0]
    kn = kn_ref[0]
    vn = vn_ref[0]
    heads = range(H_B)
    for h in heads:
        for i in range(pages_per_head):
            for c in copies(b, h, i, page=0):
                c.wait()
    sl = [slice(h * HD_B, (h + 1) * HD_B) for h in heads]
    qh = [q[:, sl[h]] * (HD_B ** -0.5) for h in heads]
    s = [_mm(_bf(jnp.broadcast_to(qh[h], (SUBLANES, HD_B))), _bf(kbuf[slot, h]))[0:1]
         for h in heads]
    s_own = [jnp.sum(qh[h] * kn[:, sl[h]], axis=-1, keepdims=True) for h in heads]
    m = [jnp.maximum(jnp.max(s[h], axis=-1, keepdims=True), s_own[h]) for h in heads]
    p = [jnp.exp(s[h] - m[h]) for h in heads]
    p_own = [jnp.exp(s_own[h] - m[h]) for h in heads]
    l = [jnp.sum(p[h], axis=-1, keepdims=True) + p_own[h] for h in heads]
    pv = [_mm_nt(_bf(jnp.broadcast_to(p[h], (SUBLANES, p[h].shape[1]))), _bf(vbuf[slot, h]))[0:1]
          for h in heads]
    for h in heads:
        o_ref[0, :, sl[h]] = _bf((pv[h] + p_own[h] * vn[:, sl[h]]) / l[h])


def _moba_sample(top_flat, pt_flat, q3, kn3, vn3, cache_kt, cache_vt, *, kk, ppb):
    nb = q3.shape[0]
    _, _, _, page_size = cache_kt.shape
    keys = kk * ppb * page_size
    row = pl.BlockSpec((1, 1, W_B), lambda b, *_: (b, 0, 0))
    any_spec = pl.BlockSpec(memory_space=pl.ANY)
    return pl.pallas_call(
        functools.partial(_moba_sample_kernel, kk=kk, ppb=ppb,
                          n_pages=pt_flat.shape[0] // nb, page_size=page_size, n_seq=nb),
        grid_spec=pltpu.PrefetchScalarGridSpec(
            num_scalar_prefetch=2,
            grid=(nb,),
            in_specs=[row, row, row, any_spec, any_spec],
            out_specs=row,
            scratch_shapes=[pltpu.VMEM((2, H_B, HD_B, keys), F32),
                            pltpu.VMEM((2, H_B, HD_B, keys), F32),
                            pltpu.SemaphoreType.DMA((2, 2, H_B))]),
        out_shape=jax.ShapeDtypeStruct((nb, 1, W_B), BF16),
        compiler_params=_PAGE_DMA_PARAMS,
    )(top_flat, pt_flat, q3, kn3, vn3, cache_kt, cache_vt)


def _rearranged_w_in(w_in, d_model):
    o_z = C_CONV
    o_lg = o_z + W_V_A
    o_b = o_lg + 2 * H_A
    o_g = o_b + 3 * W_B
    pad = jnp.zeros((d_model, LOGIT_PAD - 2 * H_A), w_in.dtype)
    return _bf(jnp.concatenate(
        [w_in[:, :o_lg], w_in[:, o_b:o_g + 2 * d_model], w_in[:, o_lg:o_b], pad], axis=1))


def kernel(x_prompt, x_sample, cache_k, cache_v, page_table, state_delta, state_conv, norm1_w,
           w_in, conv_w, a_log, dt_bias, delta_norm_w, w_proj_a, w_proj_b, w_out, norm2_w, w_up,
           w_down, norm_f_w):
    depth = w_in.shape[0]
    assert depth == 1, "single-layer trunk"
    bp, sp, d_model = x_prompt.shape
    bs, ss, _ = x_sample.shape
    assert ss == 1, "one new token per sample sequence"
    _, n_pool, page_size, _, _ = cache_k.shape
    n_pages = page_table.shape[1]
    past_len = n_pages * page_size
    ppb = MOBA_BLOCK // page_size
    n_full = past_len // MOBA_BLOCK
    assert n_full * MOBA_BLOCK == past_len, "the sample token starts a fresh MoBA block"
    kk = min(MOBA_TOPK, n_full)
    assert kk >= 1

    w_r = _rearranged_w_in(w_in[0], d_model)
    o_qb = C_CONV + W_V_A + 2 * H_A
    wqkv_t = _bf(w_in[0][:, o_qb:o_qb + 3 * W_B].T)
    wpa, wpb, wo = _bf(w_proj_a[0]), _bf(w_proj_b[0]), _bf(w_out[0])
    wup, wdn = _bf(w_up[0]), _bf(w_down[0])

    tm = 256
    pos_p = jnp.arange(sp, dtype=jnp.int32)
    (qkv, z, qt, kt, vt, ga, gb, lg, conv_p) = _in_proj(
        x_prompt.reshape(bp * sp, d_model), norm1_w[0], w_r, _rope_tables(pos_p),
        tm=tm, rows_per_seq=sp, wqkv_t=wqkv_t, tabs_t=_rope_tables_t(pos_p), conv_w=conv_w[0])
    o_a, s_p = _delta_prompt(qkv.reshape(bp, sp, C_CONV), z.reshape(bp, sp, W_V_A),
                             lg.reshape(bp, sp, LOGIT_PAD), a_log[0], dt_bias[0],
                             delta_norm_w[0])
    o_b = _moba_prompt(qt, kt, vt)
    cache_kt = cache_k.reshape(n_pool, page_size, H_B, HD_B).transpose(0, 2, 3, 1)
    cache_vt = cache_v.reshape(n_pool, page_size, H_B, HD_B).transpose(0, 2, 3, 1)
    past_pages = page_table[:, :n_full * ppb].reshape(-1)
    y_p, km = _merge_mlp(x_prompt.reshape(bp * sp, d_model), o_a.reshape(bp * sp, W_V_A),
                         o_b.reshape(bp * sp, W_B), ga, gb, wpa, wpb, wo, norm2_w[0], wup, wdn,
                         norm_f_w, tm=tm, key_stream=(past_pages, cache_kt, ppb, n_full * ppb))

    pos_s = jnp.full((bs,), past_len, dtype=jnp.int32)
    (qkv_s, z_s, qr_s, kr_s, vr_s, ga_s, gb_s, lg_s) = _in_proj(
        x_sample.reshape(bs, d_model), norm1_w[0], w_r, _rope_tables(pos_s),
        tm=bs, rows_per_seq=bs)
    o_a_s, s_s, conv_s = _delta_step(qkv_s, z_s, lg_s, state_conv[0], conv_w[0], state_delta[0],
                                     a_log[0], dt_bias[0], delta_norm_w[0])
    top = _topk(qr_s.T, km, n_full=n_full, kk=kk)
    top_flat = top[:, :, :kk].reshape(-1)
    o_b_s = _moba_sample(top_flat, page_table.reshape(-1), qr_s.reshape(bs, 1, W_B),
                         kr_s.reshape(bs, 1, W_B), vr_s.reshape(bs, 1, W_B), cache_kt, cache_vt,
                         kk=kk, ppb=ppb)
    y_s = _merge_mlp(x_sample.reshape(bs, d_model), o_a_s.reshape(bs, W_V_A),
                     o_b_s.reshape(bs, W_B), ga_s, gb_s, wpa, wpb, wo, norm2_w[0], wup, wdn,
                     norm_f_w, tm=bs)

    def kv_out(t):
        return t.reshape(1, bp, H_B, HD_B, sp).transpose(0, 1, 4, 2, 3)

    return (y_p.reshape(bp, sp, d_model), y_s.reshape(bs, ss, d_model),
            kv_out(kt), kv_out(vt),
            s_p.reshape(1, bp, H_A, DK_A, DV_A), conv_p.reshape(1, bp, CONV_W - 1, C_CONV),
            kr_s.reshape(1, bs, ss, H_B, HD_B), vr_s.reshape(1, bs, ss, H_B, HD_B),
            s_s.reshape(1, bs, H_A, DK_A, DV_A), conv_s.reshape(1, bs, CONV_W - 1, C_CONV))
```

```python
import functools
import math

import jax
import jax.numpy as jnp
from jax import lax
from jax.experimental import pallas as pl
from jax.experimental.pallas import tpu as pltpu

F32 = jnp.float32
BF16 = jnp.bfloat16
HIGHEST = lax.Precision.HIGHEST

H_A = 4
DK_A = 128
DV_A = 128
W_QK_A = H_A * DK_A
W_V_A = H_A * DV_A
C_CONV = 2 * W_QK_A + W_V_A
CONV_W = 4
H_B = 8
HD_B = 64
W_B = H_B * HD_B
MOBA_BLOCK = 256
MOBA_TOPK = 3
ROPE_THETA = 500000.0
ROT_DIM = HD_B // 4
EPS = 1e-6

LANES = 128
SUBLANES = 8
BF16_SUBLANES = 16
MOBA_HEADS_PER_STEP = 8
KMEAN_SLOTS = 64
KMEAN_GROUP = 16
CONV_ROWS = 64
HEADS_PER_LANE_TILE = LANES // HD_B
LOGIT_PAD = LANES
DELTA_CHUNK = LANES
DELTA_TILE = 4 * DELTA_CHUNK
PREP_CHUNKS = 4
DELTA_STEP_SEQS = 2
VMEM_LIMIT = 56 * 1024 * 1024
NEG_INF = float("-inf")


def _dot(a, b, dims, precision=None):
    return lax.dot_general(a, b, (dims, ((), ())), precision=precision,
                           preferred_element_type=F32)


def _mm(a, b, precision=None):
    return _dot(a, b, ((1,), (0,)), precision)


def _mm_nt(a, b, precision=None):
    return _dot(a, b, ((1,), (1,)), precision)


def _bf(x):
    return x.astype(BF16)


def _sigmoid(x):
    return 1.0 / (1.0 + jnp.exp(-x))


def _silu(x):
    return x * _sigmoid(x)


def _softplus(x):
    return jnp.maximum(x, 0.0) + jnp.log(1.0 + jnp.exp(-jnp.abs(x)))


def _rms(x, w):
    return x * lax.rsqrt(jnp.mean(x * x, axis=-1, keepdims=True) + EPS) * w


def _iota(shape, dim):
    return lax.broadcasted_iota(jnp.int32, shape, dim)


def _const_spec(shape):
    nd = len(shape)
    return pl.BlockSpec(shape, lambda *_: (0,) * nd, pipeline_mode=pl.Buffered(1))


def _block_rank(sc, n_blocks, axis):
    blk = _iota(sc.shape, axis)
    rank = jnp.zeros(sc.shape, F32)
    for j in range(n_blocks):
        cj = sc[j:j + 1, :] if axis == 0 else sc[:, j:j + 1]
        first_on_tie = jnp.where(blk > j, 1.0, 0.0)
        rank = rank + jnp.where(cj > sc, 1.0, jnp.where(cj == sc, first_on_tie, 0.0))
    return rank


def _split_bf16(x):
    hi = _bf(x)
    return hi, _bf(x - hi.astype(F32))


def _mm_3pass(a, b):
    a_hi, a_lo = _split_bf16(a)
    b_hi, b_lo = _split_bf16(b)
    return _mm(jnp.concatenate([a_hi, a_lo, a_hi], axis=1),
               jnp.concatenate([b_hi, b_hi, b_lo], axis=0))


def _in_proj_kernel(x_ref, nw_ref, w_ref, cos_ref, slo_ref, shi_ref, *rest,
                    d_model, tiles_per_seq, kv_transposed):
    if kv_transposed:
        (wqkv_t_ref, cos_t_ref, sin_t_ref, cw_ref,
         qkv_ref, z_ref, qb_ref, kb_ref, vb_ref, ga_ref, gb_ref, lg_ref, conv_ref, hist_s) = rest
    else:
        qkv_ref, z_ref, qb_ref, kb_ref, vb_ref, ga_ref, gb_ref, lg_ref = rest
    step_i = pl.program_id(0)
    xb = _bf(_rms(x_ref[...], nw_ref[...]))
    tm = xb.shape[0]
    half = ROT_DIM // 2

    def conv_act_in_place():
        last = qkv_ref[tm - SUBLANES:tm, :]
        conv_ref[0] = last[SUBLANES - (CONV_W - 1):, :]
        for j in range(C_CONV // LANES):
            lanes = slice(j * LANES, (j + 1) * LANES)
            w = cw_ref[:, lanes]
            for rb in reversed(range(tm // CONV_ROWS)):
                r0 = rb * CONV_ROWS
                y = qkv_ref[r0:r0 + CONV_ROWS, lanes]
                prev = hist_s[:, lanes] if rb == 0 else qkv_ref[r0 - SUBLANES:r0, lanes]
                ext = jnp.concatenate([prev, y], axis=0)
                acc = y * w[CONV_W - 1:CONV_W]
                for i in range(CONV_W - 1):
                    lo = SUBLANES - (CONV_W - 1) + i
                    acc = acc + ext[lo:lo + CONV_ROWS] * w[i:i + 1]
                act = _silu(acc)
                if lanes.start < 2 * W_QK_A:
                    act = act * lax.rsqrt(jnp.sum(act * act, axis=-1, keepdims=True) + EPS)
                if lanes.start < W_QK_A:
                    act = act * (DK_A ** -0.5)
                qkv_ref[r0:r0 + CONV_ROWS, lanes] = act
        hist_s[...] = last

    def cols(c0, n):
        return _mm(xb, w_ref[:, c0:c0 + n])

    def rope_store(dst_ref, c0):
        cos, slo, shi = cos_ref[...], slo_ref[...], shi_ref[...]
        for c in range(W_B // LANES):
            y = cols(c0 + c * LANES, LANES)
            up = pltpu.roll(y, LANES - half, 1)
            dn = pltpu.roll(y, half, 1)
            dst_ref[:, c * LANES:(c + 1) * LANES] = y * cos + up * slo + dn * shi

    def rope_store_t(dst_ref, r0):
        cos_t, sin_t = cos_t_ref[...], sin_t_ref[...]
        y_t = _mm_nt(wqkv_t_ref[r0:r0 + W_B, :], xb)
        for h in range(H_B):
            r = h * HD_B
            x1, x2 = y_t[r:r + half], y_t[r + half:r + ROT_DIM]
            dst_ref[0, r:r + half, :] = x1 * cos_t - x2 * sin_t
            dst_ref[0, r + half:r + ROT_DIM, :] = x2 * cos_t + x1 * sin_t
            dst_ref[0, r + ROT_DIM:r + HD_B, :] = y_t[r + ROT_DIM:r + HD_B]

    if kv_transposed:
        @pl.when(step_i % tiles_per_seq == 0)
        def _():
            hist_s[...] = jnp.zeros_like(hist_s)

    step = 512
    c0 = 0
    for c in range(C_CONV // step):
        qkv_ref[:, c0:c0 + step] = cols(c0, step)
        c0 += step
    if kv_transposed:
        conv_act_in_place()
    z_ref[...] = cols(c0, W_V_A)
    c0 += W_V_A
    if kv_transposed:
        rope_store_t(qb_ref, 0)
        rope_store_t(kb_ref, W_B)
        vb_ref[0] = _mm_nt(wqkv_t_ref[2 * W_B:3 * W_B, :], xb)
    else:
        rope_store(qb_ref, c0)
        rope_store(kb_ref, c0 + W_B)
        vb_ref[...] = cols(c0 + 2 * W_B, W_B)
    c0 += 3 * W_B
    for g_ref in (ga_ref, gb_ref):
        for c in range(d_model // step):
            g_ref[:, c * step:(c + 1) * step] = cols(c0, step)
            c0 += step
    lg_ref[...] = cols(c0, LOGIT_PAD)


def _in_proj(x2d, norm_w, w_r, tabs, *, tm, rows_per_seq, wqkv_t=None, tabs_t=None, conv_w=None):
    rows, d_model = x2d.shape
    n_tiles = rows // tm
    tiles_per_seq = rows_per_seq // tm
    n_seq = rows // rows_per_seq
    kv_transposed = wqkv_t is not None

    def row_spec(n):
        return pl.BlockSpec((tm, n), lambda i: (i, 0))

    def sds(*shape):
        return jax.ShapeDtypeStruct(shape, F32)

    kv_t_spec = pl.BlockSpec((1, W_B, tm), lambda i: (i // tiles_per_seq, 0, i % tiles_per_seq))
    kv_shape = sds(n_seq, W_B, rows_per_seq) if kv_transposed else sds(rows, W_B)
    kv_spec = kv_t_spec if kv_transposed else row_spec(W_B)
    tab_spec = pl.BlockSpec((tm, LANES), lambda i: (i % tiles_per_seq, 0))
    in_specs = [row_spec(d_model), _const_spec((1, d_model)), _const_spec(w_r.shape),
                tab_spec, tab_spec, tab_spec]
    args = [x2d, norm_w.reshape(1, d_model), w_r, *tabs]
    out_shape = [sds(rows, C_CONV), sds(rows, W_V_A), kv_shape, kv_shape, kv_shape,
                 sds(rows, d_model), sds(rows, d_model), sds(rows, LOGIT_PAD)]
    out_specs = [row_spec(C_CONV), row_spec(W_V_A), kv_spec, kv_spec, kv_spec,
                 row_spec(d_model), row_spec(d_model), row_spec(LOGIT_PAD)]
    if kv_transposed:
        tab_t_spec = pl.BlockSpec((ROT_DIM // 2, tm), lambda i: (0, i % tiles_per_seq))
        in_specs += [_const_spec(wqkv_t.shape), tab_t_spec, tab_t_spec, _const_spec(conv_w.shape)]
        args += [wqkv_t, *tabs_t, conv_w]
        out_shape.append(sds(n_seq, CONV_W - 1, C_CONV))
        out_specs.append(pl.BlockSpec((1, CONV_W - 1, C_CONV),
                                      lambda i: (i // tiles_per_seq, 0, 0)))
        assert tm >= SUBLANES
        scratch = [pltpu.VMEM((SUBLANES, C_CONV), F32)]
    else:
        scratch = []
    return pl.pallas_call(
        functools.partial(_in_proj_kernel, d_model=d_model, tiles_per_seq=tiles_per_seq,
                          kv_transposed=kv_transposed),
        grid=(n_tiles,),
        in_specs=in_specs,
        out_specs=out_specs,
        out_shape=out_shape,
        scratch_shapes=scratch,
        compiler_params=pltpu.CompilerParams(dimension_semantics=("arbitrary",),
                                             vmem_limit_bytes=VMEM_LIMIT),
    )(*args)


def _rope_angles(pos):
    half = ROT_DIM // 2
    inv_freq = jnp.exp(jnp.arange(half, dtype=F32) * (-2.0 * math.log(ROPE_THETA) / ROT_DIM))
    ang = pos.astype(F32)[:, None] * inv_freq[None, :]
    return jnp.cos(ang), jnp.sin(ang)


def _rope_tables_t(pos):
    cos, sin = _rope_angles(pos)
    return cos.T, sin.T


def _rope_tables(pos):
    half = ROT_DIM // 2
    cos, sin = _rope_angles(pos)
    n = pos.shape[0]
    ones = jnp.ones((n, HD_B - ROT_DIM), F32)
    zeros_h = jnp.zeros((n, half), F32)
    zeros_r = jnp.zeros((n, HD_B - ROT_DIM), F32)
    cos_h = jnp.concatenate([cos, cos, ones], axis=1)
    slo_h = jnp.concatenate([-sin, zeros_h, zeros_r], axis=1)
    shi_h = jnp.concatenate([zeros_h, sin, zeros_r], axis=1)
    tile = lambda t: jnp.tile(t, (1, HEADS_PER_LANE_TILE))
    return tile(cos_h), tile(slo_h), tile(shi_h)


def _delta_prompt_kernel(alog_ref, dtb_ref, x_ref, z_ref, lg_ref, nw_ref,
                         o_ref, s_ref,
                         state_s, u_s, w_s, qg_s, qk_s, kdt_s, gl_s):
    t = pl.program_id(1)
    cs = DELTA_CHUNK
    n_chunks = x_ref.shape[1] // cs
    heads = range(H_A)

    @pl.when(t == 0)
    def _():
        state_s[...] = jnp.zeros_like(state_s)

    sq = (cs, cs)
    row = _iota(sq, 0)
    col = _iota(sq, 1)
    causal = col <= row
    strict = col < row
    tril_b = _bf(jnp.where(causal, 1.0, 0.0))
    eye_f = jnp.where(col == row, 1.0, 0.0)
    lane_row = _iota((1, LANES), 1)
    alog_row = jnp.zeros((1, LANES), F32)
    dtb_row = jnp.zeros((1, LANES), F32)
    for h in heads:
        alog_row = jnp.where(lane_row == H_A + h, alog_ref[h], alog_row)
        dtb_row = jnp.where(lane_row == H_A + h, dtb_ref[h], dtb_row)
    neg_a_row = -jnp.exp(alog_row)

    def prep(ci, carry):
        cidx = [ci * PREP_CHUNKS + d for d in range(PREP_CHUNKS)]
        rows = [pl.ds(pl.multiple_of(c * cs, cs), cs) for c in cidx]
        units = [(d, h) for d in range(PREP_CHUNKS) for h in heads]
        un = range(len(units))
        q = [x_ref[0, rows[d], h * DK_A:(h + 1) * DK_A] for d, h in units]
        k = [x_ref[0, rows[d], W_QK_A + h * DK_A:W_QK_A + (h + 1) * DK_A] for d, h in units]
        v = [x_ref[0, rows[d], 2 * W_QK_A + h * DV_A:2 * W_QK_A + (h + 1) * DV_A] for d, h in units]
        sig, gc_all = [], []
        for d in range(PREP_CHUNKS):
            lg = lg_ref[0, rows[d], :]
            sig.append(_sigmoid(lg))
            rem = neg_a_row * _softplus(lg + dtb_row)
            pieces = []
            for _ in range(3):
                piece = _bf(rem)
                pieces.append(piece)
                rem = rem - piece.astype(F32)
            csum = _mm(tril_b, jnp.concatenate(pieces, axis=1))
            gc_all.append(csum[:, :LANES] + csum[:, LANES:2 * LANES] + csum[:, 2 * LANES:])
        beta = [sig[d][:, h:h + 1] for d, h in units]
        gc = [jnp.broadcast_to(gc_all[d][:, H_A + h:H_A + h + 1], (cs, LANES)) for d, h in units]
        decay = [jnp.exp(jnp.where(causal, gc[u] - gc[u].T, NEG_INF)) for u in un]
        kb = [k[u] * beta[u] for u in un]
        a = [jnp.where(strict, _mm_nt(_bf(kb[u]), _bf(k[u])) * decay[u], 0.0) for u in un]
        t_inv = [eye_f - a[u] for u in un]
        a_pow = [_mm_3pass(a[u], a[u]) for u in un]
        n_levels = int(math.log2(cs)) - 1
        for lvl in range(n_levels):
            if lvl < n_levels - 1:
                prod = [_mm_3pass(a_pow[u], jnp.concatenate([a_pow[u], t_inv[u]], axis=1))
                        for u in un]
                a_pow = [prod[u][:, :cs] for u in un]
                t_inv = [t_inv[u] + prod[u][:, cs:] for u in un]
            else:
                t_inv = [t_inv[u] + _mm_3pass(a_pow[u], t_inv[u]) for u in un]
        e_gc = [jnp.exp(gc[u]) for u in un]
        uw = [_mm(_bf(t_inv[u]),
                  jnp.concatenate([_bf(v[u] * beta[u]), _bf(kb[u] * e_gc[u])], axis=1))
              for u in un]
        qk = [jnp.where(causal, _mm_nt(_bf(q[u]), _bf(k[u])) * decay[u], 0.0) for u in un]
        for u, (d, h) in enumerate(units):
            g_last = gc[u][cs - 1:cs, :]
            u_s[h, rows[d], :] = uw[u][:, :DV_A]
            w_s[h, rows[d], :] = _bf(uw[u][:, DV_A:])
            qk_s[h, rows[d], :] = _bf(qk[u])
            qg_s[h, rows[d], :] = _bf(q[u] * e_gc[u])
            kdt_s[h, cidx[d]] = _bf((k[u] * jnp.exp(g_last - gc[u])).T)
            gl_s[h, cidx[d]] = jnp.broadcast_to(jnp.exp(g_last), (SUBLANES, LANES))
        return carry

    lax.fori_loop(0, n_chunks // PREP_CHUNKS, prep, 0)

    nw = nw_ref[...]

    def scan(c, carry):
        rows = pl.ds(pl.multiple_of(c * cs, cs), cs)
        state = [state_s[h] for h in heads]
        sb = [_bf(state[h]) for h in heads]
        ws = [_mm(jnp.concatenate([w_s[h, rows, :], qg_s[h, rows, :]], axis=0), sb[h])
              for h in heads]
        vb = [_bf(u_s[h, rows, :] - ws[h][:cs]) for h in heads]
        o = [ws[h][cs:] + _mm(qk_s[h, rows, :], vb[h]) for h in heads]
        new = [state[h] * gl_s[h, c][0:1] + _mm(kdt_s[h, c], vb[h]) for h in heads]
        for h in heads:
            state_s[h] = new[h]
            lanes = slice(h * DV_A, (h + 1) * DV_A)
            o_ref[0, rows, lanes] = _bf(_rms(o[h], nw) * _silu(z_ref[0, rows, lanes]))
        return carry

    lax.fori_loop(0, n_chunks, scan, 0)

    @pl.when(t == pl.num_programs(1) - 1)
    def _():
        s_ref[0] = state_s[...]


def _delta_prompt(qkv, z, lg, a_log, dt_bias, norm_w):
    bsz, seq, _ = qkv.shape
    ts = min(DELTA_TILE, seq)
    n_chunks = ts // DELTA_CHUNK
    assert seq % ts == 0 and ts % (DELTA_CHUNK * PREP_CHUNKS) == 0
    assert DK_A == LANES and DV_A == LANES

    def tile_spec(n):
        return pl.BlockSpec((1, ts, n), lambda b, t: (b, t, 0))

    def per_head(shape, dtype):
        return pltpu.VMEM((H_A,) + shape, dtype)

    smem = pl.BlockSpec(memory_space=pltpu.SMEM)
    return pl.pallas_call(
        _delta_prompt_kernel,
        grid=(bsz, seq // ts),
        in_specs=[smem, smem, tile_spec(C_CONV), tile_spec(W_V_A), tile_spec(LOGIT_PAD),
                  _const_spec((1, DV_A))],
        out_specs=[tile_spec(W_V_A),
                   pl.BlockSpec((1, H_A, DK_A, DV_A), lambda b, t: (b, 0, 0, 0))],
        out_shape=[jax.ShapeDtypeStruct((bsz, seq, W_V_A), BF16),
                   jax.ShapeDtypeStruct((bsz, H_A, DK_A, DV_A), F32)],
        scratch_shapes=[per_head((DK_A, DV_A), F32),
                        per_head((ts, DV_A), F32),
                        per_head((ts, DK_A), BF16),
                        per_head((ts, DK_A), BF16),
                        per_head((ts, DELTA_CHUNK), BF16),
                        per_head((n_chunks, DK_A, DELTA_CHUNK), BF16),
                        per_head((n_chunks, SUBLANES, LANES), F32)],
        compiler_params=pltpu.CompilerParams(dimension_semantics=("arbitrary", "arbitrary"),
                                             vmem_limit_bytes=VMEM_LIMIT),
    )(a_log, dt_bias, qkv, z, lg, norm_w.reshape(1, DV_A))


def _moba_prompt_kernel(qt_ref, kt_ref, vt_ref, o_ref, kn_s, vt_s, km_s, ch_s, *, n_blocks):
    qi = pl.program_id(2)
    bs = MOBA_BLOCK
    n_pairs = qt_ref.shape[1] // LANES
    heads = range(n_pairs * HEADS_PER_LANE_TILE)
    pair_of = [h // HEADS_PER_LANE_TILE for h in heads]

    @pl.when(qi == 0)
    def _():
        km_s[...] = jnp.zeros_like(km_s)
        ones = jnp.ones((BF16_SUBLANES, bs), BF16)
        for j in range(n_blocks):
            keys = slice(j * bs, (j + 1) * bs)
            for pr in range(n_pairs):
                kj = kt_ref[0, pr * LANES:(pr + 1) * LANES, keys].T
                kn_s[j, pr] = _bf(kj)
                km_s[pr, j:j + 1, :] = jnp.mean(kj, axis=0, keepdims=True)
            for h in heads:
                vt_s[j, h, 0:HD_B, :] = _bf(vt_ref[0, h * HD_B:(h + 1) * HD_B, keys])
                vt_s[j, h, HD_B:, :] = ones

    d_row = _iota((LANES, bs), 0)
    blk = _iota((km_s.shape[1], bs), 0)
    key_le_query = _iota((bs, bs), 0) <= _iota((bs, bs), 1)
    qh = []
    for h in heads:
        hh = h % HEADS_PER_LANE_TILE
        qt = qt_ref[0, pair_of[h] * LANES:(pair_of[h] + 1) * LANES, :]
        qh.append(jnp.where((d_row >= hh * HD_B) & (d_row < (hh + 1) * HD_B), qt, 0.0))
    sc = [jnp.where(blk < qi, _mm_3pass(km_s[pair_of[h]], qh[h]), NEG_INF) for h in heads]
    rank = [_block_rank(sc[h], n_blocks, 0) for h in heads]
    for h in heads:
        ch_s[h] = jnp.where(blk < qi, jnp.where(rank[h] < MOBA_TOPK, 1.0, 0.0), 0.0)

    qs = [_bf(qh[h] * (HD_B ** -0.5 * math.log2(math.e))) for h in heads]
    s = [jnp.where(key_le_query, _mm(kn_s[qi, pair_of[h]], qs[h]), NEG_INF)
         for h in heads]
    m = [jnp.max(s[h], axis=0, keepdims=True) for h in heads]
    acc = [_mm(vt_s[qi, h], _bf(jnp.exp2(s[h] - m[h]))) for h in heads]

    def two_past_blocks(jj, carry):
        m, acc = carry
        js = (2 * jj, 2 * jj + 1)
        s = [[jnp.where(ch_s[h, pl.ds(j, 1), :] > 0.0, _mm(kn_s[j, pair_of[h]], qs[h]), NEG_INF)
              for j in js] for h in heads]
        m_new = [jnp.maximum(m[h], jnp.maximum(jnp.max(s[h][0], axis=0, keepdims=True),
                                               jnp.max(s[h][1], axis=0, keepdims=True)))
                 for h in heads]
        acc = [jnp.exp2(m[h] - m_new[h]) * acc[h]
               + _mm(vt_s[js[0], h], _bf(jnp.exp2(s[h][0] - m_new[h])))
               + _mm(vt_s[js[1], h], _bf(jnp.exp2(s[h][1] - m_new[h]))) for h in heads]
        return m_new, acc

    m, acc = lax.fori_loop(0, (qi + 1) // 2, two_past_blocks, (m, acc))
    o_t = jnp.concatenate([acc[h][:HD_B] / acc[h][HD_B:HD_B + 1] for h in heads], axis=0)
    o_ref[0] = _bf(o_t.T)


def _moba_prompt(qt, kt, vt):
    bsz, _, seq = qt.shape
    n_blocks = seq // MOBA_BLOCK
    assert seq % MOBA_BLOCK == 0
    blocks_pad = -(-n_blocks // SUBLANES) * SUBLANES
    n_pairs = MOBA_HEADS_PER_STEP // HEADS_PER_LANE_TILE
    rows = n_pairs * LANES
    q_spec = pl.BlockSpec((1, rows, MOBA_BLOCK), lambda b, p, i: (b, p, i))
    kv_spec = pl.BlockSpec((1, rows, seq), lambda b, p, i: (b, p, 0))
    return pl.pallas_call(
        functools.partial(_moba_prompt_kernel, n_blocks=n_blocks),
        grid=(bsz, H_B // MOBA_HEADS_PER_STEP, n_blocks),
        in_specs=[q_spec, kv_spec, kv_spec],
        out_specs=pl.BlockSpec((1, MOBA_BLOCK, rows), lambda b, p, i: (b, i, p)),
        out_shape=jax.ShapeDtypeStruct((bsz, seq, W_B), BF16),
        scratch_shapes=[pltpu.VMEM((n_blocks, n_pairs, MOBA_BLOCK, LANES), BF16),
                        pltpu.VMEM((n_blocks, MOBA_HEADS_PER_STEP, HD_B + BF16_SUBLANES,
                                    MOBA_BLOCK), BF16),
                        pltpu.VMEM((n_pairs, blocks_pad, LANES), F32),
                        pltpu.VMEM((MOBA_HEADS_PER_STEP, blocks_pad, MOBA_BLOCK), F32)],
        compiler_params=pltpu.CompilerParams(
            dimension_semantics=("arbitrary", "arbitrary", "arbitrary"),
            vmem_limit_bytes=VMEM_LIMIT),
    )(qt, kt, vt)


def _key_mean_groups(pt_ref, cache_ref, km_ref, page_buf, page_sems, stream):
    pages_per_step, pages_per_seq, total, ppb, page_size = stream
    step_i = pl.program_id(0)
    base = step_i * pages_per_step
    blocks_per_group = KMEAN_GROUP // ppb
    lane = _iota((HD_B, LANES), 1)

    def page_copy(p, slot):
        return pltpu.make_async_copy(cache_ref.at[pt_ref[p]], page_buf.at[slot], page_sems.at[slot])

    @pl.when(step_i == 0)
    def _():
        for s in range(KMEAN_SLOTS):
            page_copy(s, s).start()

    @pl.when(base % pages_per_seq == 0)
    def _():
        km_ref[...] = jnp.zeros_like(km_ref)

    def consume_group(g):
        p0 = base + g * KMEAN_GROUP
        slot0 = (g * KMEAN_GROUP) % KMEAN_SLOTS
        for j in range(KMEAN_GROUP):
            page_copy(p0 + j, slot0 + j).wait()
        blk0 = (base % pages_per_seq) // ppb + g * blocks_per_group
        for h in range(H_B):
            acc = jnp.zeros((HD_B, LANES), F32)
            for k in range(blocks_per_group):
                x = page_buf[slot0 + k * ppb, h]
                for r in range(1, ppb):
                    x = x + page_buf[slot0 + k * ppb + r, h]
                mean = jnp.sum(x, axis=-1, keepdims=True) * (1.0 / (ppb * page_size))
                acc = jnp.where(lane == blk0 + k, mean, acc)
            rows = slice(h * HD_B, (h + 1) * HD_B)
            km_ref[0, rows, :] = km_ref[0, rows, :] + acc
        for j in range(KMEAN_GROUP):
            @pl.when(p0 + j + KMEAN_SLOTS < total)
            def _():
                page_copy(p0 + j + KMEAN_SLOTS, slot0 + j).start()

    return [functools.partial(consume_group, g) for g in range(pages_per_step // KMEAN_GROUP)]


def _merge_mlp_kernel(*refs, ff_step, stream):
    if stream is None:
        (x_ref, oa_ref, ob_ref, ga_ref, gb_ref, wpa_ref, wpb_ref, wo_ref, n2_ref, wup_ref, wdn_ref,
         nf_ref, y_ref) = refs
        groups = []
    else:
        (pt_ref, x_ref, oa_ref, ob_ref, ga_ref, gb_ref, wpa_ref, wpb_ref, wo_ref, n2_ref, wup_ref,
         wdn_ref, nf_ref, cache_ref, y_ref, km_ref, page_buf, page_sems) = refs
        groups = _key_mean_groups(pt_ref, cache_ref, km_ref, page_buf, page_sems, stream)
    d_ff = wup_ref.shape[1]
    n_ff = d_ff // ff_step

    y_a = _mm(oa_ref[...], wpa_ref[...])
    y_b = _mm(ob_ref[...], wpb_ref[...])
    mixed = _sigmoid(ga_ref[...]) * y_a + _sigmoid(gb_ref[...]) * y_b
    h = x_ref[...] + _mm(_bf(mixed), wo_ref[...])
    hn = _bf(_rms(h, n2_ref[...]))
    out = h
    for c in range(n_ff):
        for g in groups[c * len(groups) // n_ff:(c + 1) * len(groups) // n_ff]:
            g()
        u = jnp.maximum(_mm(hn, wup_ref[:, c * ff_step:(c + 1) * ff_step]), 0.0)
        out = out + _mm(_bf(u * u), wdn_ref[c * ff_step:(c + 1) * ff_step, :])
    y_ref[...] = _rms(out, nf_ref[...])


def _merge_mlp(x2d, oa, ob, ga, gb, wpa, wpb, wo, n2, wup, wdn, nf, *, tm, key_stream=None):
    rows, d_model = x2d.shape
    d_ff = wup.shape[1]
    n_tiles = rows // tm

    def row_spec(n):
        return pl.BlockSpec((tm, n), lambda i, *_: (i, 0))

    in_specs = [row_spec(d_model), row_spec(W_V_A), row_spec(W_B), row_spec(d_model),
                row_spec(d_model),
                _const_spec(wpa.shape), _const_spec(wpb.shape), _const_spec(wo.shape),
                _const_spec((1, d_model)), _const_spec(wup.shape), _const_spec(wdn.shape),
                _const_spec((1, d_model))]
    args = [x2d, oa, ob, ga, gb, wpa, wpb, wo, n2.reshape(1, d_model), wup, wdn,
            nf.reshape(1, d_model)]
    y_shape = jax.ShapeDtypeStruct((rows, d_model), F32)
    params = dict(dimension_semantics=("arbitrary",), vmem_limit_bytes=VMEM_LIMIT)
    ff_step = min(d_ff, 1024)
    if key_stream is None:
        return pl.pallas_call(
            functools.partial(_merge_mlp_kernel, ff_step=ff_step, stream=None),
            grid=(n_tiles,), in_specs=in_specs, out_specs=row_spec(d_model), out_shape=y_shape,
            compiler_params=pltpu.CompilerParams(**params),
        )(*args)

    page_ids, cache_t, ppb, pages_per_seq = key_stream
    total = page_ids.shape[0]
    _, _, _, page_size = cache_t.shape
    pages_per_step = total // n_tiles
    assert (pages_per_step * n_tiles == total and pages_per_step % KMEAN_SLOTS == 0
            and pages_per_seq % pages_per_step == 0 and KMEAN_GROUP % ppb == 0
            and KMEAN_SLOTS % KMEAN_GROUP == 0 and pages_per_seq // ppb <= LANES)
    km_shape = jax.ShapeDtypeStruct((total // pages_per_seq, W_B, LANES), F32)
    km_spec = pl.BlockSpec((1, W_B, LANES),
                           lambda i, *_: (i * pages_per_step // pages_per_seq, 0, 0))
    return pl.pallas_call(
        functools.partial(_merge_mlp_kernel, ff_step=ff_step,
                          stream=(pages_per_step, pages_per_seq, total, ppb, page_size)),
        grid_spec=pltpu.PrefetchScalarGridSpec(
            num_scalar_prefetch=1, grid=(n_tiles,),
            in_specs=in_specs + [pl.BlockSpec(memory_space=pl.ANY)],
            out_specs=[row_spec(d_model), km_spec],
            scratch_shapes=[pltpu.VMEM((KMEAN_SLOTS, H_B, HD_B, page_size), F32),
                            pltpu.SemaphoreType.DMA((KMEAN_SLOTS,))]),
        out_shape=[y_shape, km_shape],
        compiler_params=pltpu.CompilerParams(disable_bounds_checks=True, **params),
    )(page_ids, *args, cache_t)


def _delta_step_kernel(alog_ref, dtb_ref, x_ref, z_ref, lg_ref, sc_ref, cw_ref, sd_ref, nw_ref,
                       o_ref, snew_ref, cnew_ref):
    n_seq = x_ref.shape[0]
    w = cw_ref[...]
    y = []
    for s in range(n_seq):
        x = x_ref[s]
        hist = sc_ref[s]
        acc = x * w[CONV_W - 1:CONV_W]
        for i in range(CONV_W - 1):
            acc = acc + hist[i:i + 1] * w[i:i + 1]
        y.append(_silu(acc))
        cnew_ref[s, 0:CONV_W - 2, :] = hist[1:CONV_W - 1]
        cnew_ref[s, CONV_W - 2:CONV_W - 1, :] = x
    nw = nw_ref[...]
    sq = (DK_A, DK_A)
    eye = _iota(sq, 0) == _iota(sq, 1)

    def l2n(t):
        return t * lax.rsqrt(jnp.sum(t * t, axis=-1, keepdims=True) + EPS)

    units = [(s, h) for s in range(n_seq) for h in range(H_A)]
    un = range(len(units))
    q = [l2n(y[s][:, h * DK_A:(h + 1) * DK_A]) * (DK_A ** -0.5) for s, h in units]
    k = [l2n(y[s][:, W_QK_A + h * DK_A:W_QK_A + (h + 1) * DK_A]) for s, h in units]
    v = [y[s][:, 2 * W_QK_A + h * DV_A:2 * W_QK_A + (h + 1) * DV_A] for s, h in units]
    beta = [_sigmoid(lg_ref[s][:, h:h + 1]) for s, h in units]
    g = [-jnp.exp(jnp.full((1, 1), alog_ref[h], F32))
         * _softplus(lg_ref[s][:, H_A + h:H_A + h + 1] + dtb_ref[h]) for s, h in units]
    state = [sd_ref[s, h] * jnp.exp(g[u]) for u, (s, h) in enumerate(units)]
    kv = [_mm(jnp.broadcast_to(k[u], (SUBLANES, DK_A)), state[u], HIGHEST)[0:1] for u in un]
    dv = [(v[u] - kv[u]) * beta[u] for u in un]
    k_diag = [jnp.where(eye, jnp.broadcast_to(k[u], sq), 0.0) for u in un]
    state = [state[u] + _mm(k_diag[u], jnp.broadcast_to(dv[u], (DK_A, DV_A)), HIGHEST)
             for u in un]
    o = [_mm(jnp.broadcast_to(q[u], (SUBLANES, DK_A)), state[u], HIGHEST)[0:1] for u in un]
    for u, (s, h) in enumerate(units):
        snew_ref[s, h] = state[u]
        gate = _silu(z_ref[s][:, h * DV_A:(h + 1) * DV_A])
        o_ref[s, :, h * DV_A:(h + 1) * DV_A] = _bf(_rms(o[u], nw) * gate)


def _delta_step(qkv, z, lg, state_conv, conv_w, state_delta, a_log, dt_bias, norm_w):
    nb = qkv.shape[0]
    smem = pl.BlockSpec(memory_space=pltpu.SMEM)

    per_step = math.gcd(nb, DELTA_STEP_SEQS)

    def per_b(shape):
        nd = len(shape)
        return pl.BlockSpec((per_step,) + shape, lambda b: (b,) + (0,) * nd)

    return pl.pallas_call(
        _delta_step_kernel,
        grid=(nb // per_step,),
        in_specs=[smem, smem, per_b((1, C_CONV)), per_b((1, W_V_A)), per_b((1, LOGIT_PAD)),
                  per_b((CONV_W - 1, C_CONV)), _const_spec((CONV_W, C_CONV)),
                  per_b((H_A, DK_A, DV_A)), _const_spec((1, DV_A))],
        out_specs=[per_b((1, W_V_A)), per_b((H_A, DK_A, DV_A)), per_b((CONV_W - 1, C_CONV))],
        out_shape=[jax.ShapeDtypeStruct((nb, 1, W_V_A), BF16),
                   jax.ShapeDtypeStruct((nb, H_A, DK_A, DV_A), F32),
                   jax.ShapeDtypeStruct((nb, CONV_W - 1, C_CONV), F32)],
        compiler_params=pltpu.CompilerParams(dimension_semantics=("arbitrary",),
                                             vmem_limit_bytes=VMEM_LIMIT),
    )(a_log, dt_bias, qkv.reshape(nb, 1, C_CONV), z.reshape(nb, 1, W_V_A),
      lg.reshape(nb, 1, LOGIT_PAD), state_conv, conv_w, state_delta, norm_w.reshape(1, DV_A))


def _topk_kernel(qt_ref, km_ref, idx_ref, *, n_full, kk):
    n_seq = km_ref.shape[0]
    qt = qt_ref[...]
    per_seq = []
    for b in range(n_seq):
        prod = km_ref[b] * qt[:, b:b + 1]
        per_seq.append(jnp.concatenate(
            [jnp.sum(prod[h * HD_B:(h + 1) * HD_B], axis=0, keepdims=True) for h in range(H_B)],
            axis=0))
    sc = jnp.concatenate(per_seq, axis=0)
    blk = _iota(sc.shape, 1)
    sc = jnp.where(blk < n_full, sc, NEG_INF)
    rank = _block_rank(sc, n_full, 1)
    blk_f = blk.astype(F32)
    out = jnp.zeros(sc.shape, F32)
    for r in range(kk):
        pick = jnp.sum(jnp.where(rank == float(r), blk_f, 0.0), axis=-1, keepdims=True)
        out = jnp.where(blk == r, pick, out)
    out = out.astype(jnp.int32)
    for b in range(n_seq):
        idx_ref[b] = out[b * H_B:(b + 1) * H_B]


def _topk(q_t, km, *, n_full, kk):
    n_seq = km.shape[0]
    return pl.pallas_call(
        functools.partial(_topk_kernel, n_full=n_full, kk=kk),
        grid=(1,),
        in_specs=[_const_spec(q_t.shape), _const_spec(km.shape)],
        out_specs=pl.BlockSpec((n_seq, H_B, LANES), lambda i: (0, 0, 0)),
        out_shape=jax.ShapeDtypeStruct((n_seq, H_B, LANES), jnp.int32),
        compiler_params=pltpu.CompilerParams(dimension_semantics=("arbitrary",),
                                             vmem_limit_bytes=VMEM_LIMIT),
    )(q_t, km)


_PAGE_DMA_PARAMS = pltpu.CompilerParams(dimension_semantics=("arbitrary",),
                                        vmem_limit_bytes=VMEM_LIMIT, disable_bounds_checks=True)


def _moba_sample_kernel(top_ref, pt_ref, q_ref, kn_ref, vn_ref, ck_ref, cv_ref, o_ref,
                        kbuf, vbuf, sems, *, kk, ppb, n_pages, page_size, n_seq):
    b = pl.program_id(0)
    pages_per_head = kk * ppb

    def copies(seq, h, i, page=None):
        slot = seq % 2
        if page is None:
            blk = top_ref[(seq * H_B + h) * kk + i // ppb]
            page = pt_ref[seq * n_pages + blk * ppb + i % ppb]
        keys = pl.ds(i * page_size, page_size)
        return (pltpu.make_async_copy(ck_ref.at[page, h], kbuf.at[slot, h, :, keys],
                                      sems.at[slot, 0, h]),
                pltpu.make_async_copy(cv_ref.at[page, h], vbuf.at[slot, h, :, keys],
                                      sems.at[slot, 1, h]))

    def start_fetch(seq):
        for h in range(H_B):
            for i in range(pages_per_head):
                for c in copies(seq, h, i):
                    c.start()

    @pl.when(b == 0)
    def _():
        start_fetch(b)

    @pl.when(b + 1 < n_seq)
    def _():
        start_fetch(b + 1)

    slot = b % 2
    q = q_ref[0]
    kn = kn_ref[0]
    vn = vn_ref[0]
    heads = range(H_B)
    for h in heads:
        for i in range(pages_per_head):
            for c in copies(b, h, i, page=0):
                c.wait()
    sl = [slice(h * HD_B, (h + 1) * HD_B) for h in heads]
    qh = [q[:, sl[h]] * (HD_B ** -0.5) for h in heads]
    s = [_mm(_bf(jnp.broadcast_to(qh[h], (SUBLANES, HD_B))), _bf(kbuf[slot, h]))[0:1]
         for h in heads]
    s_own = [jnp.sum(qh[h] * kn[:, sl[h]], axis=-1, keepdims=True) for h in heads]
    m = [jnp.maximum(jnp.max(s[h], axis=-1, keepdims=True), s_own[h]) for h in heads]
    p = [jnp.exp(s[h] - m[h]) for h in heads]
    p_own = [jnp.exp(s_own[h] - m[h]) for h in heads]
    l = [jnp.sum(p[h], axis=-1, keepdims=True) + p_own[h] for h in heads]
    pv = [_mm_nt(_bf(jnp.broadcast_to(p[h], (SUBLANES, p[h].shape[1]))), _bf(vbuf[slot, h]))[0:1]
          for h in heads]
    for h in heads:
        o_ref[0, :, sl[h]] = _bf((pv[h] + p_own[h] * vn[:, sl[h]]) / l[h])


def _moba_sample(top_flat, pt_flat, q3, kn3, vn3, cache_kt, cache_vt, *, kk, ppb):
    nb = q3.shape[0]
    _, _, _, page_size = cache_kt.shape
    keys = kk * ppb * page_size
    row = pl.BlockSpec((1, 1, W_B), lambda b, *_: (b, 0, 0))
    any_spec = pl.BlockSpec(memory_space=pl.ANY)
    return pl.pallas_call(
        functools.partial(_moba_sample_kernel, kk=kk, ppb=ppb,
                          n_pages=pt_flat.shape[0] // nb, page_size=page_size, n_seq=nb),
        grid_spec=pltpu.PrefetchScalarGridSpec(
            num_scalar_prefetch=2,
            grid=(nb,),
            in_specs=[row, row, row, any_spec, any_spec],
            out_specs=row,
            scratch_shapes=[pltpu.VMEM((2, H_B, HD_B, keys), F32),
                            pltpu.VMEM((2, H_B, HD_B, keys), F32),
                            pltpu.SemaphoreType.DMA((2, 2, H_B))]),
        out_shape=jax.ShapeDtypeStruct((nb, 1, W_B), BF16),
        compiler_params=_PAGE_DMA_PARAMS,
    )(top_flat, pt_flat, q3, kn3, vn3, cache_kt, cache_vt)


def _rearranged_w_in(w_in, d_model):
    o_z = C_CONV
    o_lg = o_z + W_V_A
    o_b = o_lg + 2 * H_A
    o_g = o_b + 3 * W_B
    pad = jnp.zeros((d_model, LOGIT_PAD - 2 * H_A), w_in.dtype)
    return _bf(jnp.concatenate(
        [w_in[:, :o_lg], w_in[:, o_b:o_g + 2 * d_model], w_in[:, o_lg:o_b], pad], axis=1))


def kernel(x_prompt, x_sample, cache_k, cache_v, page_table, state_delta, state_conv, norm1_w,
           w_in, conv_w, a_log, dt_bias, delta_norm_w, w_proj_a, w_proj_b, w_out, norm2_w, w_up,
           w_down, norm_f_w):
    depth = w_in.shape[0]
    assert depth == 1, "single-layer trunk"
    bp, sp, d_model = x_prompt.shape
    bs, ss, _ = x_sample.shape
    assert ss == 1, "one new token per sample sequence"
    _, n_pool, page_size, _, _ = cache_k.shape
    n_pages = page_table.shape[1]
    past_len = n_pages * page_size
    ppb = MOBA_BLOCK // page_size
    n_full = past_len // MOBA_BLOCK
    assert n_full * MOBA_BLOCK == past_len, "the sample token starts a fresh MoBA block"
    kk = min(MOBA_TOPK, n_full)
    assert kk >= 1

    w_r = _rearranged_w_in(w_in[0], d_model)
    o_qb = C_CONV + W_V_A + 2 * H_A
    wqkv_t = _bf(w_in[0][:, o_qb:o_qb + 3 * W_B].T)
    wpa, wpb, wo = _bf(w_proj_a[0]), _bf(w_proj_b[0]), _bf(w_out[0])
    wup, wdn = _bf(w_up[0]), _bf(w_down[0])

    tm = 256
    pos_p = jnp.arange(sp, dtype=jnp.int32)
    (qkv, z, qt, kt, vt, ga, gb, lg, conv_p) = _in_proj(
        x_prompt.reshape(bp * sp, d_model), norm1_w[0], w_r, _rope_tables(pos_p),
        tm=tm, rows_per_seq=sp, wqkv_t=wqkv_t, tabs_t=_rope_tables_t(pos_p), conv_w=conv_w[0])
    o_a, s_p = _delta_prompt(qkv.reshape(bp, sp, C_CONV), z.reshape(bp, sp, W_V_A),
                             lg.reshape(bp, sp, LOGIT_PAD), a_log[0], dt_bias[0],
                             delta_norm_w[0])
    o_b = _moba_prompt(qt, kt, vt)
    cache_kt = cache_k.reshape(n_pool, page_size, H_B, HD_B).transpose(0, 2, 3, 1)
    cache_vt = cache_v.reshape(n_pool, page_size, H_B, HD_B).transpose(0, 2, 3, 1)
    past_pages = page_table[:, :n_full * ppb].reshape(-1)
    y_p, km = _merge_mlp(x_prompt.reshape(bp * sp, d_model), o_a.reshape(bp * sp, W_V_A),
                         o_b.reshape(bp * sp, W_B), ga, gb, wpa, wpb, wo, norm2_w[0], wup, wdn,
                         norm_f_w, tm=tm, key_stream=(past_pages, cache_kt, ppb, n_full * ppb))

    pos_s = jnp.full((bs,), past_len, dtype=jnp.int32)
    (qkv_s, z_s, qr_s, kr_s, vr_s, ga_s, gb_s, lg_s) = _in_proj(
        x_sample.reshape(bs, d_model), norm1_w[0], w_r, _rope_tables(pos_s),
        tm=bs, rows_per_seq=bs)
    o_a_s, s_s, conv_s = _delta_step(qkv_s, z_s, lg_s, state_conv[0], conv_w[0], state_delta[0],
                                     a_log[0], dt_bias[0], delta_norm_w[0])
    top = _topk(qr_s.T, km, n_full=n_full, kk=kk)
    top_flat = top[:, :, :kk].reshape(-1)
    o_b_s = _moba_sample(top_flat, page_table.reshape(-1), qr_s.reshape(bs, 1, W_B),
                         kr_s.reshape(bs, 1, W_B), vr_s.reshape(bs, 1, W_B), cache_kt, cache_vt,
                         kk=kk, ppb=ppb)
    y_s = _merge_mlp(x_sample.reshape(bs, d_model), o_a_s.reshape(bs, W_V_A),
                     o_b_s.reshape(bs, W_B), ga_s, gb_s, wpa, wpb, wo, norm2_w[0], wup, wdn,
                     norm_f_w, tm=bs)

    def kv_out(t):
        return t.reshape(1, bp, H_B, HD_B, sp).transpose(0, 1, 4, 2, 3)

    return (y_p.reshape(bp, sp, d_model), y_s.reshape(bs, ss, d_model),
            kv_out(kt), kv_out(vt),
            s_p.reshape(1, bp, H_A, DK_A, DV_A), conv_p.reshape(1, bp, CONV_W - 1, C_CONV),
            kr_s.reshape(1, bs, ss, H_B, HD_B), vr_s.reshape(1, bs, ss, H_B, HD_B),
            s_s.reshape(1, bs, H_A, DK_A, DV_A), conv_s.reshape(1, bs, CONV_W - 1, C_CONV))
```

```python
import functools
import math

import jax
import jax.numpy as jnp
from jax import lax
from jax.experimental import pallas as pl
from jax.experimental.pallas import tpu as pltpu

F32 = jnp.float32
BF16 = jnp.bfloat16
HIGHEST = lax.Precision.HIGHEST

H_A = 4
DK_A = 128
DV_A = 128
W_QK_A = H_A * DK_A
W_V_A = H_A * DV_A
C_CONV = 2 * W_QK_A + W_V_A
CONV_W = 4
H_B = 8
HD_B = 64
W_B = H_B * HD_B
MOBA_BLOCK = 256
MOBA_TOPK = 3
ROPE_THETA = 500000.0
ROT_DIM = HD_B // 4
EPS = 1e-6

LANES = 128
SUBLANES = 8
BF16_SUBLANES = 16
MOBA_HEADS_PER_STEP = 8
KMEAN_SLOTS = 64
KMEAN_GROUP = 16
CONV_ROWS = 64
HEADS_PER_LANE_TILE = LANES // HD_B
LOGIT_PAD = LANES
DELTA_CHUNK = LANES
DELTA_TILE = 4 * DELTA_CHUNK
PREP_CHUNKS = 4
DELTA_STEP_SEQS = 2
VMEM_LIMIT = 56 * 1024 * 1024
NEG_INF = float("-inf")
UNCHOSEN_OFFSET = 1e30


def _dot(a, b, dims, precision=None):
    return lax.dot_general(a, b, (dims, ((), ())), precision=precision,
                           preferred_element_type=F32)


def _mm(a, b, precision=None):
    return _dot(a, b, ((1,), (0,)), precision)


def _mm_nt(a, b, precision=None):
    return _dot(a, b, ((1,), (1,)), precision)


def _bf(x):
    return x.astype(BF16)


def _sigmoid(x):
    return 1.0 / (1.0 + jnp.exp(-x))


def _silu(x):
    return x * _sigmoid(x)


def _softplus(x):
    return jnp.maximum(x, 0.0) + jnp.log(1.0 + jnp.exp(-jnp.abs(x)))


def _rms(x, w):
    return x * lax.rsqrt(jnp.mean(x * x, axis=-1, keepdims=True) + EPS) * w


def _iota(shape, dim):
    return lax.broadcasted_iota(jnp.int32, shape, dim)


def _const_spec(shape):
    nd = len(shape)
    return pl.BlockSpec(shape, lambda *_: (0,) * nd, pipeline_mode=pl.Buffered(1))


def _block_rank(sc, n_blocks, axis):
    blk = _iota(sc.shape, axis)
    rank = jnp.zeros(sc.shape, F32)
    for j in range(n_blocks):
        cj = sc[j:j + 1, :] if axis == 0 else sc[:, j:j + 1]
        first_on_tie = jnp.where(blk > j, 1.0, 0.0)
        rank = rank + jnp.where(cj > sc, 1.0, jnp.where(cj == sc, first_on_tie, 0.0))
    return rank


def _split_bf16(x):
    hi = _bf(x)
    return hi, _bf(x - hi.astype(F32))


def _mm_3pass(a, b):
    a_hi, a_lo = _split_bf16(a)
    b_hi, b_lo = _split_bf16(b)
    return _mm(jnp.concatenate([a_hi, a_lo, a_hi], axis=1),
               jnp.concatenate([b_hi, b_hi, b_lo], axis=0))


def _in_proj_kernel(x_ref, nw_ref, w_ref, cos_ref, slo_ref, shi_ref, *rest,
                    d_model, tiles_per_seq, kv_transposed):
    if kv_transposed:
        (wqkv_t_ref, cos_t_ref, sin_t_ref, cw_ref,
         qkv_ref, z_ref, qb_ref, kb_ref, vb_ref, ga_ref, gb_ref, lg_ref, conv_ref, hist_s) = rest
    else:
        qkv_ref, z_ref, qb_ref, kb_ref, vb_ref, ga_ref, gb_ref, lg_ref = rest
    step_i = pl.program_id(0)
    xb = _bf(_rms(x_ref[...], nw_ref[...]))
    tm = xb.shape[0]
    half = ROT_DIM // 2

    def conv_act_in_place():
        last = qkv_ref[tm - SUBLANES:tm, :]
        conv_ref[0] = last[SUBLANES - (CONV_W - 1):, :]
        for j in range(C_CONV // LANES):
            lanes = slice(j * LANES, (j + 1) * LANES)
            w = cw_ref[:, lanes]
            for rb in reversed(range(tm // CONV_ROWS)):
                r0 = rb * CONV_ROWS
                y = qkv_ref[r0:r0 + CONV_ROWS, lanes]
                prev = hist_s[:, lanes] if rb == 0 else qkv_ref[r0 - SUBLANES:r0, lanes]
                ext = jnp.concatenate([prev, y], axis=0)
                acc = y * w[CONV_W - 1:CONV_W]
                for i in range(CONV_W - 1):
                    lo = SUBLANES - (CONV_W - 1) + i
                    acc = acc + ext[lo:lo + CONV_ROWS] * w[i:i + 1]
                act = _silu(acc)
                if lanes.start < 2 * W_QK_A:
                    act = act * lax.rsqrt(jnp.sum(act * act, axis=-1, keepdims=True) + EPS)
                if lanes.start < W_QK_A:
                    act = act * (DK_A ** -0.5)
                qkv_ref[r0:r0 + CONV_ROWS, lanes] = act
        hist_s[...] = last

    def cols(c0, n):
        return _mm(xb, w_ref[:, c0:c0 + n])

    def rope_store(dst_ref, c0):
        cos, slo, shi = cos_ref[...], slo_ref[...], shi_ref[...]
        for c in range(W_B // LANES):
            y = cols(c0 + c * LANES, LANES)
            up = pltpu.roll(y, LANES - half, 1)
            dn = pltpu.roll(y, half, 1)
            dst_ref[:, c * LANES:(c + 1) * LANES] = y * cos + up * slo + dn * shi

    def rope_store_t(dst_ref, r0):
        cos_t, sin_t = cos_t_ref[...], sin_t_ref[...]
        y_t = _mm_nt(wqkv_t_ref[r0:r0 + W_B, :], xb)
        for h in range(H_B):
            r = h * HD_B
            x1, x2 = y_t[r:r + half], y_t[r + half:r + ROT_DIM]
            dst_ref[0, r:r + half, :] = x1 * cos_t - x2 * sin_t
            dst_ref[0, r + half:r + ROT_DIM, :] = x2 * cos_t + x1 * sin_t
            dst_ref[0, r + ROT_DIM:r + HD_B, :] = y_t[r + ROT_DIM:r + HD_B]

    if kv_transposed:
        @pl.when(step_i % tiles_per_seq == 0)
        def _():
            hist_s[...] = jnp.zeros_like(hist_s)

    step = 512
    c0 = 0
    for c in range(C_CONV // step):
        qkv_ref[:, c0:c0 + step] = cols(c0, step)
        c0 += step
    if kv_transposed:
        conv_act_in_place()
    z_ref[...] = cols(c0, W_V_A)
    c0 += W_V_A
    if kv_transposed:
        rope_store_t(qb_ref, 0)
        rope_store_t(kb_ref, W_B)
        vb_ref[0] = _mm_nt(wqkv_t_ref[2 * W_B:3 * W_B, :], xb)
    else:
        rope_store(qb_ref, c0)
        rope_store(kb_ref, c0 + W_B)
        vb_ref[...] = cols(c0 + 2 * W_B, W_B)
    c0 += 3 * W_B
    for g_ref in (ga_ref, gb_ref):
        for c in range(d_model // step):
            g_ref[:, c * step:(c + 1) * step] = cols(c0, step)
            c0 += step
    lg_ref[...] = cols(c0, LOGIT_PAD)


def _in_proj(x2d, norm_w, w_r, tabs, *, tm, rows_per_seq, wqkv_t=None, tabs_t=None, conv_w=None):
    rows, d_model = x2d.shape
    n_tiles = rows // tm
    tiles_per_seq = rows_per_seq // tm
    n_seq = rows // rows_per_seq
    kv_transposed = wqkv_t is not None

    def row_spec(n):
        return pl.BlockSpec((tm, n), lambda i: (i, 0))

    def sds(*shape):
        return jax.ShapeDtypeStruct(shape, F32)

    kv_t_spec = pl.BlockSpec((1, W_B, tm), lambda i: (i // tiles_per_seq, 0, i % tiles_per_seq))
    kv_shape = sds(n_seq, W_B, rows_per_seq) if kv_transposed else sds(rows, W_B)
    kv_spec = kv_t_spec if kv_transposed else row_spec(W_B)
    tab_spec = pl.BlockSpec((tm, LANES), lambda i: (i % tiles_per_seq, 0))
    in_specs = [row_spec(d_model), _const_spec((1, d_model)), _const_spec(w_r.shape),
                tab_spec, tab_spec, tab_spec]
    args = [x2d, norm_w.reshape(1, d_model), w_r, *tabs]
    out_shape = [sds(rows, C_CONV), sds(rows, W_V_A), kv_shape, kv_shape, kv_shape,
                 sds(rows, d_model), sds(rows, d_model), sds(rows, LOGIT_PAD)]
    out_specs = [row_spec(C_CONV), row_spec(W_V_A), kv_spec, kv_spec, kv_spec,
                 row_spec(d_model), row_spec(d_model), row_spec(LOGIT_PAD)]
    if kv_transposed:
        tab_t_spec = pl.BlockSpec((ROT_DIM // 2, tm), lambda i: (0, i % tiles_per_seq))
        in_specs += [_const_spec(wqkv_t.shape), tab_t_spec, tab_t_spec, _const_spec(conv_w.shape)]
        args += [wqkv_t, *tabs_t, conv_w]
        out_shape.append(sds(n_seq, CONV_W - 1, C_CONV))
        out_specs.append(pl.BlockSpec((1, CONV_W - 1, C_CONV),
                                      lambda i: (i // tiles_per_seq, 0, 0)))
        assert tm >= SUBLANES
        scratch = [pltpu.VMEM((SUBLANES, C_CONV), F32)]
    else:
        scratch = []
    return pl.pallas_call(
        functools.partial(_in_proj_kernel, d_model=d_model, tiles_per_seq=tiles_per_seq,
                          kv_transposed=kv_transposed),
        grid=(n_tiles,),
        in_specs=in_specs,
        out_specs=out_specs,
        out_shape=out_shape,
        scratch_shapes=scratch,
        compiler_params=pltpu.CompilerParams(dimension_semantics=("arbitrary",),
                                             vmem_limit_bytes=VMEM_LIMIT),
    )(*args)


def _rope_angles(pos):
    half = ROT_DIM // 2
    inv_freq = jnp.exp(jnp.arange(half, dtype=F32) * (-2.0 * math.log(ROPE_THETA) / ROT_DIM))
    ang = pos.astype(F32)[:, None] * inv_freq[None, :]
    return jnp.cos(ang), jnp.sin(ang)


def _rope_tables_t(pos):
    cos, sin = _rope_angles(pos)
    return cos.T, sin.T


def _rope_tables(pos):
    half = ROT_DIM // 2
    cos, sin = _rope_angles(pos)
    n = pos.shape[0]
    ones = jnp.ones((n, HD_B - ROT_DIM), F32)
    zeros_h = jnp.zeros((n, half), F32)
    zeros_r = jnp.zeros((n, HD_B - ROT_DIM), F32)
    cos_h = jnp.concatenate([cos, cos, ones], axis=1)
    slo_h = jnp.concatenate([-sin, zeros_h, zeros_r], axis=1)
    shi_h = jnp.concatenate([zeros_h, sin, zeros_r], axis=1)
    tile = lambda t: jnp.tile(t, (1, HEADS_PER_LANE_TILE))
    return tile(cos_h), tile(slo_h), tile(shi_h)


def _delta_prompt_kernel(alog_ref, dtb_ref, x_ref, z_ref, lg_ref, nw_ref,
                         o_ref, s_ref,
                         state_s, u_s, w_s, qg_s, qk_s, kdt_s, gl_s):
    t = pl.program_id(1)
    cs = DELTA_CHUNK
    n_chunks = x_ref.shape[1] // cs
    heads = range(H_A)

    @pl.when(t == 0)
    def _():
        state_s[...] = jnp.zeros_like(state_s)

    sq = (cs, cs)
    row = _iota(sq, 0)
    col = _iota(sq, 1)
    causal = col <= row
    strict = col < row
    tril_b = _bf(jnp.where(causal, 1.0, 0.0))
    eye_f = jnp.where(col == row, 1.0, 0.0)
    lane_row = _iota((1, LANES), 1)
    alog_row = jnp.zeros((1, LANES), F32)
    dtb_row = jnp.zeros((1, LANES), F32)
    for h in heads:
        alog_row = jnp.where(lane_row == H_A + h, alog_ref[h], alog_row)
        dtb_row = jnp.where(lane_row == H_A + h, dtb_ref[h], dtb_row)
    neg_a_row = -jnp.exp(alog_row)

    def prep(ci, carry):
        cidx = [ci * PREP_CHUNKS + d for d in range(PREP_CHUNKS)]
        rows = [pl.ds(pl.multiple_of(c * cs, cs), cs) for c in cidx]
        units = [(d, h) for d in range(PREP_CHUNKS) for h in heads]
        un = range(len(units))
        q = [x_ref[0, rows[d], h * DK_A:(h + 1) * DK_A] for d, h in units]
        k = [x_ref[0, rows[d], W_QK_A + h * DK_A:W_QK_A + (h + 1) * DK_A] for d, h in units]
        v = [x_ref[0, rows[d], 2 * W_QK_A + h * DV_A:2 * W_QK_A + (h + 1) * DV_A] for d, h in units]
        sig, gc_all = [], []
        for d in range(PREP_CHUNKS):
            lg = lg_ref[0, rows[d], :]
            sig.append(_sigmoid(lg))
            rem = neg_a_row * _softplus(lg + dtb_row)
            pieces = []
            for _ in range(3):
                piece = _bf(rem)
                pieces.append(piece)
                rem = rem - piece.astype(F32)
            csum = _mm(tril_b, jnp.concatenate(pieces, axis=1))
            gc_all.append(csum[:, :LANES] + csum[:, LANES:2 * LANES] + csum[:, 2 * LANES:])
        beta = [sig[d][:, h:h + 1] for d, h in units]
        gc = [jnp.broadcast_to(gc_all[d][:, H_A + h:H_A + h + 1], (cs, LANES)) for d, h in units]
        decay = [jnp.exp(jnp.where(causal, gc[u] - gc[u].T, NEG_INF)) for u in un]
        kb = [k[u] * beta[u] for u in un]
        a = [jnp.where(strict, _mm_nt(_bf(kb[u]), _bf(k[u])) * decay[u], 0.0) for u in un]
        t_inv = [eye_f - a[u] for u in un]
        a_pow = [_mm_3pass(a[u], a[u]) for u in un]
        n_levels = int(math.log2(cs)) - 1
        for lvl in range(n_levels):
            if lvl < n_levels - 1:
                prod = [_mm_3pass(a_pow[u], jnp.concatenate([a_pow[u], t_inv[u]], axis=1))
                        for u in un]
                a_pow = [prod[u][:, :cs] for u in un]
                t_inv = [t_inv[u] + prod[u][:, cs:] for u in un]
            else:
                t_inv = [t_inv[u] + _mm_3pass(a_pow[u], t_inv[u]) for u in un]
        e_gc = [jnp.exp(gc[u]) for u in un]
        uw = [_mm(_bf(t_inv[u]),
                  jnp.concatenate([_bf(v[u] * beta[u]), _bf(kb[u] * e_gc[u])], axis=1))
              for u in un]
        qk = [jnp.where(causal, _mm_nt(_bf(q[u]), _bf(k[u])) * decay[u], 0.0) for u in un]
        for u, (d, h) in enumerate(units):
            g_last = gc[u][cs - 1:cs, :]
            u_s[h, rows[d], :] = uw[u][:, :DV_A]
            w_s[h, rows[d], :] = _bf(uw[u][:, DV_A:])
            qk_s[h, rows[d], :] = _bf(qk[u])
            qg_s[h, rows[d], :] = _bf(q[u] * e_gc[u])
            kdt_s[h, cidx[d]] = _bf((k[u] * jnp.exp(g_last - gc[u])).T)
            gl_s[h, cidx[d]] = jnp.broadcast_to(jnp.exp(g_last), (SUBLANES, LANES))
        return carry

    lax.fori_loop(0, n_chunks // PREP_CHUNKS, prep, 0)

    nw = nw_ref[...]

    def scan(c, carry):
        rows = pl.ds(pl.multiple_of(c * cs, cs), cs)
        state = [state_s[h] for h in heads]
        sb = [_bf(state[h]) for h in heads]
        ws = [_mm(jnp.concatenate([w_s[h, rows, :], qg_s[h, rows, :]], axis=0), sb[h])
              for h in heads]
        vb = [_bf(u_s[h, rows, :] - ws[h][:cs]) for h in heads]
        o = [ws[h][cs:] + _mm(qk_s[h, rows, :], vb[h]) for h in heads]
        new = [state[h] * gl_s[h, c][0:1] + _mm(kdt_s[h, c], vb[h]) for h in heads]
        for h in heads:
            state_s[h] = new[h]
            lanes = slice(h * DV_A, (h + 1) * DV_A)
            o_ref[0, rows, lanes] = _bf(_rms(o[h], nw) * _silu(z_ref[0, rows, lanes]))
        return carry

    lax.fori_loop(0, n_chunks, scan, 0)

    @pl.when(t == pl.num_programs(1) - 1)
    def _():
        s_ref[0] = state_s[...]


def _delta_prompt(qkv, z, lg, a_log, dt_bias, norm_w):
    bsz, seq, _ = qkv.shape
    ts = min(DELTA_TILE, seq)
    n_chunks = ts // DELTA_CHUNK
    assert seq % ts == 0 and ts % (DELTA_CHUNK * PREP_CHUNKS) == 0
    assert DK_A == LANES and DV_A == LANES

    def tile_spec(n):
        return pl.BlockSpec((1, ts, n), lambda b, t: (b, t, 0))

    def per_head(shape, dtype):
        return pltpu.VMEM((H_A,) + shape, dtype)

    smem = pl.BlockSpec(memory_space=pltpu.SMEM)
    return pl.pallas_call(
        _delta_prompt_kernel,
        grid=(bsz, seq // ts),
        in_specs=[smem, smem, tile_spec(C_CONV), tile_spec(W_V_A), tile_spec(LOGIT_PAD),
                  _const_spec((1, DV_A))],
        out_specs=[tile_spec(W_V_A),
                   pl.BlockSpec((1, H_A, DK_A, DV_A), lambda b, t: (b, 0, 0, 0))],
        out_shape=[jax.ShapeDtypeStruct((bsz, seq, W_V_A), BF16),
                   jax.ShapeDtypeStruct((bsz, H_A, DK_A, DV_A), F32)],
        scratch_shapes=[per_head((DK_A, DV_A), F32),
                        per_head((ts, DV_A), F32),
                        per_head((ts, DK_A), BF16),
                        per_head((ts, DK_A), BF16),
                        per_head((ts, DELTA_CHUNK), BF16),
                        per_head((n_chunks, DK_A, DELTA_CHUNK), BF16),
                        per_head((n_chunks, SUBLANES, LANES), F32)],
        compiler_params=pltpu.CompilerParams(dimension_semantics=("arbitrary", "arbitrary"),
                                             vmem_limit_bytes=VMEM_LIMIT),
    )(a_log, dt_bias, qkv, z, lg, norm_w.reshape(1, DV_A))


def _moba_prompt_kernel(qt_ref, kt_ref, vt_ref, o_ref, kn_s, vt_s, km_s, ch_s, *, n_blocks):
    qi = pl.program_id(2)
    bs = MOBA_BLOCK
    n_pairs = qt_ref.shape[1] // LANES
    heads = range(n_pairs * HEADS_PER_LANE_TILE)
    pair_of = [h // HEADS_PER_LANE_TILE for h in heads]

    @pl.when(qi == 0)
    def _():
        km_s[...] = jnp.zeros_like(km_s)
        ones = jnp.ones((BF16_SUBLANES, bs), BF16)
        for j in range(n_blocks):
            keys = slice(j * bs, (j + 1) * bs)
            for pr in range(n_pairs):
                kj = kt_ref[0, pr * LANES:(pr + 1) * LANES, keys].T
                kn_s[j, pr] = _bf(kj)
                km_s[pr, j:j + 1, :] = jnp.mean(kj, axis=0, keepdims=True)
            for h in heads:
                vt_s[j, h, 0:HD_B, :] = _bf(vt_ref[0, h * HD_B:(h + 1) * HD_B, keys])
                vt_s[j, h, HD_B:, :] = ones

    d_row = _iota((LANES, bs), 0)
    blk = _iota((km_s.shape[1], bs), 0)
    key_le_query = _iota((bs, bs), 0) <= _iota((bs, bs), 1)
    qh = []
    for h in heads:
        hh = h % HEADS_PER_LANE_TILE
        qt = qt_ref[0, pair_of[h] * LANES:(pair_of[h] + 1) * LANES, :]
        qh.append(jnp.where((d_row >= hh * HD_B) & (d_row < (hh + 1) * HD_B), qt, 0.0))
    sc = [jnp.where(blk < qi, _mm_3pass(km_s[pair_of[h]], qh[h]), NEG_INF) for h in heads]
    rank = [_block_rank(sc[h], n_blocks, 0) for h in heads]
    for h in heads:
        ch_s[h] = jnp.where(blk < qi, jnp.where(rank[h] < MOBA_TOPK, 1.0, 0.0), 0.0)

    qs = [_bf(qh[h] * (HD_B ** -0.5 * math.log2(math.e))) for h in heads]
    s = [jnp.where(key_le_query, _mm(kn_s[qi, pair_of[h]], qs[h]), NEG_INF)
         for h in heads]
    m = [jnp.max(s[h], axis=0, keepdims=True) for h in heads]
    acc = [_mm(vt_s[qi, h], _bf(jnp.exp2(s[h] - m[h]))) for h in heads]

    def two_past_blocks(jj, carry):
        m, acc = carry
        js = (2 * jj, 2 * jj + 1)
        pick = [[ch_s[h, pl.ds(j, 1), :] > 0.0 for j in js] for h in heads]
        s = [[_mm(kn_s[j, pair_of[h]], qs[h]) for j in js] for h in heads]
        col_max = [[jnp.where(pick[h][i], jnp.max(s[h][i], axis=0, keepdims=True), NEG_INF)
                    for i in range(2)] for h in heads]
        m_new = [jnp.maximum(m[h], jnp.maximum(col_max[h][0], col_max[h][1])) for h in heads]
        off = [[jnp.where(pick[h][i], m_new[h], UNCHOSEN_OFFSET) for i in range(2)] for h in heads]
        acc = [jnp.exp2(m[h] - m_new[h]) * acc[h]
               + _mm(vt_s[js[0], h], _bf(jnp.exp2(s[h][0] - off[h][0])))
               + _mm(vt_s[js[1], h], _bf(jnp.exp2(s[h][1] - off[h][1]))) for h in heads]
        return m_new, acc

    m, acc = lax.fori_loop(0, (qi + 1) // 2, two_past_blocks, (m, acc))
    o_t = jnp.concatenate([acc[h][:HD_B] / acc[h][HD_B:HD_B + 1] for h in heads], axis=0)
    o_ref[0] = _bf(o_t.T)


def _moba_prompt(qt, kt, vt):
    bsz, _, seq = qt.shape
    n_blocks = seq // MOBA_BLOCK
    assert seq % MOBA_BLOCK == 0
    blocks_pad = -(-n_blocks // SUBLANES) * SUBLANES
    n_pairs = MOBA_HEADS_PER_STEP // HEADS_PER_LANE_TILE
    rows = n_pairs * LANES
    q_spec = pl.BlockSpec((1, rows, MOBA_BLOCK), lambda b, p, i: (b, p, i))
    kv_spec = pl.BlockSpec((1, rows, seq), lambda b, p, i: (b, p, 0))
    return pl.pallas_call(
        functools.partial(_moba_prompt_kernel, n_blocks=n_blocks),
        grid=(bsz, H_B // MOBA_HEADS_PER_STEP, n_blocks),
        in_specs=[q_spec, kv_spec, kv_spec],
        out_specs=pl.BlockSpec((1, MOBA_BLOCK, rows), lambda b, p, i: (b, i, p)),
        out_shape=jax.ShapeDtypeStruct((bsz, seq, W_B), BF16),
        scratch_shapes=[pltpu.VMEM((n_blocks, n_pairs, MOBA_BLOCK, LANES), BF16),
                        pltpu.VMEM((n_blocks, MOBA_HEADS_PER_STEP, HD_B + BF16_SUBLANES,
                                    MOBA_BLOCK), BF16),
                        pltpu.VMEM((n_pairs, blocks_pad, LANES), F32),
                        pltpu.VMEM((MOBA_HEADS_PER_STEP, blocks_pad, MOBA_BLOCK), F32)],
        compiler_params=pltpu.CompilerParams(
            dimension_semantics=("arbitrary", "arbitrary", "arbitrary"),
            vmem_limit_bytes=VMEM_LIMIT),
    )(qt, kt, vt)


def _key_mean_groups(pt_ref, cache_ref, km_ref, page_buf, page_sems, stream):
    pages_per_step, pages_per_seq, total, ppb, page_size = stream
    step_i = pl.program_id(0)
    base = step_i * pages_per_step
    blocks_per_group = KMEAN_GROUP // ppb
    lane = _iota((HD_B, LANES), 1)

    def page_copy(p, slot):
        return pltpu.make_async_copy(cache_ref.at[pt_ref[p]], page_buf.at[slot], page_sems.at[slot])

    @pl.when(step_i == 0)
    def _():
        for s in range(KMEAN_SLOTS):
            page_copy(s, s).start()

    @pl.when(base % pages_per_seq == 0)
    def _():
        km_ref[...] = jnp.zeros_like(km_ref)

    def consume_group(g):
        p0 = base + g * KMEAN_GROUP
        slot0 = (g * KMEAN_GROUP) % KMEAN_SLOTS
        for j in range(KMEAN_GROUP):
            page_copy(p0 + j, slot0 + j).wait()
        blk0 = (base % pages_per_seq) // ppb + g * blocks_per_group
        for h in range(H_B):
            acc = jnp.zeros((HD_B, LANES), F32)
            for k in range(blocks_per_group):
                x = page_buf[slot0 + k * ppb, h]
                for r in range(1, ppb):
                    x = x + page_buf[slot0 + k * ppb + r, h]
                mean = jnp.sum(x, axis=-1, keepdims=True) * (1.0 / (ppb * page_size))
                acc = jnp.where(lane == blk0 + k, mean, acc)
            rows = slice(h * HD_B, (h + 1) * HD_B)
            km_ref[0, rows, :] = km_ref[0, rows, :] + acc
        for j in range(KMEAN_GROUP):
            @pl.when(p0 + j + KMEAN_SLOTS < total)
            def _():
                page_copy(p0 + j + KMEAN_SLOTS, slot0 + j).start()

    return [functools.partial(consume_group, g) for g in range(pages_per_step // KMEAN_GROUP)]


def _merge_mlp_kernel(*refs, ff_step, stream):
    if stream is None:
        (x_ref, oa_ref, ob_ref, ga_ref, gb_ref, wpa_ref, wpb_ref, wo_ref, n2_ref, wup_ref, wdn_ref,
         nf_ref, y_ref) = refs
        groups = []
    else:
        (pt_ref, x_ref, oa_ref, ob_ref, ga_ref, gb_ref, wpa_ref, wpb_ref, wo_ref, n2_ref, wup_ref,
         wdn_ref, nf_ref, cache_ref, y_ref, km_ref, page_buf, page_sems) = refs
        groups = _key_mean_groups(pt_ref, cache_ref, km_ref, page_buf, page_sems, stream)
    d_ff = wup_ref.shape[1]
    n_ff = d_ff // ff_step

    y_a = _mm(oa_ref[...], wpa_ref[...])
    y_b = _mm(ob_ref[...], wpb_ref[...])
    mixed = _sigmoid(ga_ref[...]) * y_a + _sigmoid(gb_ref[...]) * y_b
    h = x_ref[...] + _mm(_bf(mixed), wo_ref[...])
    hn = _bf(_rms(h, n2_ref[...]))
    out = h
    for c in range(n_ff):
        for g in groups[c * len(groups) // n_ff:(c + 1) * len(groups) // n_ff]:
            g()
        u = jnp.maximum(_mm(hn, wup_ref[:, c * ff_step:(c + 1) * ff_step]), 0.0)
        out = out + _mm(_bf(u * u), wdn_ref[c * ff_step:(c + 1) * ff_step, :])
    y_ref[...] = _rms(out, nf_ref[...])


def _merge_mlp(x2d, oa, ob, ga, gb, wpa, wpb, wo, n2, wup, wdn, nf, *, tm, key_stream=None):
    rows, d_model = x2d.shape
    d_ff = wup.shape[1]
    n_tiles = rows // tm

    def row_spec(n):
        return pl.BlockSpec((tm, n), lambda i, *_: (i, 0))

    in_specs = [row_spec(d_model), row_spec(W_V_A), row_spec(W_B), row_spec(d_model),
                row_spec(d_model),
                _const_spec(wpa.shape), _const_spec(wpb.shape), _const_spec(wo.shape),
                _const_spec((1, d_model)), _const_spec(wup.shape), _const_spec(wdn.shape),
                _const_spec((1, d_model))]
    args = [x2d, oa, ob, ga, gb, wpa, wpb, wo, n2.reshape(1, d_model), wup, wdn,
            nf.reshape(1, d_model)]
    y_shape = jax.ShapeDtypeStruct((rows, d_model), F32)
    params = dict(dimension_semantics=("arbitrary",), vmem_limit_bytes=VMEM_LIMIT)
    ff_step = min(d_ff, 1024)
    if key_stream is None:
        return pl.pallas_call(
            functools.partial(_merge_mlp_kernel, ff_step=ff_step, stream=None),
            grid=(n_tiles,), in_specs=in_specs, out_specs=row_spec(d_model), out_shape=y_shape,
            compiler_params=pltpu.CompilerParams(**params),
        )(*args)

    page_ids, cache_t, ppb, pages_per_seq = key_stream
    total = page_ids.shape[0]
    _, _, _, page_size = cache_t.shape
    pages_per_step = total // n_tiles
    assert (pages_per_step * n_tiles == total and pages_per_step % KMEAN_SLOTS == 0
            and pages_per_seq % pages_per_step == 0 and KMEAN_GROUP % ppb == 0
            and KMEAN_SLOTS % KMEAN_GROUP == 0 and pages_per_seq // ppb <= LANES)
    km_shape = jax.ShapeDtypeStruct((total // pages_per_seq, W_B, LANES), F32)
    km_spec = pl.BlockSpec((1, W_B, LANES),
                           lambda i, *_: (i * pages_per_step // pages_per_seq, 0, 0))
    return pl.pallas_call(
        functools.partial(_merge_mlp_kernel, ff_step=ff_step,
                          stream=(pages_per_step, pages_per_seq, total, ppb, page_size)),
        grid_spec=pltpu.PrefetchScalarGridSpec(
            num_scalar_prefetch=1, grid=(n_tiles,),
            in_specs=in_specs + [pl.BlockSpec(memory_space=pl.ANY)],
            out_specs=[row_spec(d_model), km_spec],
            scratch_shapes=[pltpu.VMEM((KMEAN_SLOTS, H_B, HD_B, page_size), F32),
                            pltpu.SemaphoreType.DMA((KMEAN_SLOTS,))]),
        out_shape=[y_shape, km_shape],
        compiler_params=pltpu.CompilerParams(disable_bounds_checks=True, **params),
    )(page_ids, *args, cache_t)


def _delta_step_kernel(alog_ref, dtb_ref, x_ref, z_ref, lg_ref, sc_ref, cw_ref, sd_ref, nw_ref,
                       o_ref, snew_ref, cnew_ref):
    n_seq = x_ref.shape[0]
    w = cw_ref[...]
    y = []
    for s in range(n_seq):
        x = x_ref[s]
        hist = sc_ref[s]
        acc = x * w[CONV_W - 1:CONV_W]
        for i in range(CONV_W - 1):
            acc = acc + hist[i:i + 1] * w[i:i + 1]
        y.append(_silu(acc))
        cnew_ref[s, 0:CONV_W - 2, :] = hist[1:CONV_W - 1]
        cnew_ref[s, CONV_W - 2:CONV_W - 1, :] = x
    nw = nw_ref[...]
    sq = (DK_A, DK_A)
    eye = _iota(sq, 0) == _iota(sq, 1)

    def l2n(t):
        return t * lax.rsqrt(jnp.sum(t * t, axis=-1, keepdims=True) + EPS)

    units = [(s, h) for s in range(n_seq) for h in range(H_A)]
    un = range(len(units))
    q = [l2n(y[s][:, h * DK_A:(h + 1) * DK_A]) * (DK_A ** -0.5) for s, h in units]
    k = [l2n(y[s][:, W_QK_A + h * DK_A:W_QK_A + (h + 1) * DK_A]) for s, h in units]
    v = [y[s][:, 2 * W_QK_A + h * DV_A:2 * W_QK_A + (h + 1) * DV_A] for s, h in units]
    beta = [_sigmoid(lg_ref[s][:, h:h + 1]) for s, h in units]
    g = [-jnp.exp(jnp.full((1, 1), alog_ref[h], F32))
         * _softplus(lg_ref[s][:, H_A + h:H_A + h + 1] + dtb_ref[h]) for s, h in units]
    state = [sd_ref[s, h] * jnp.exp(g[u]) for u, (s, h) in enumerate(units)]
    kv = [_mm(jnp.broadcast_to(k[u], (SUBLANES, DK_A)), state[u], HIGHEST)[0:1] for u in un]
    dv = [(v[u] - kv[u]) * beta[u] for u in un]
    k_diag = [jnp.where(eye, jnp.broadcast_to(k[u], sq), 0.0) for u in un]
    state = [state[u] + _mm(k_diag[u], jnp.broadcast_to(dv[u], (DK_A, DV_A)), HIGHEST)
             for u in un]
    o = [_mm(jnp.broadcast_to(q[u], (SUBLANES, DK_A)), state[u], HIGHEST)[0:1] for u in un]
    for u, (s, h) in enumerate(units):
        snew_ref[s, h] = state[u]
        gate = _silu(z_ref[s][:, h * DV_A:(h + 1) * DV_A])
        o_ref[s, :, h * DV_A:(h + 1) * DV_A] = _bf(_rms(o[u], nw) * gate)


def _delta_step(qkv, z, lg, state_conv, conv_w, state_delta, a_log, dt_bias, norm_w):
    nb = qkv.shape[0]
    smem = pl.BlockSpec(memory_space=pltpu.SMEM)

    per_step = math.gcd(nb, DELTA_STEP_SEQS)

    def per_b(shape):
        nd = len(shape)
        return pl.BlockSpec((per_step,) + shape, lambda b: (b,) + (0,) * nd)

    return pl.pallas_call(
        _delta_step_kernel,
        grid=(nb // per_step,),
        in_specs=[smem, smem, per_b((1, C_CONV)), per_b((1, W_V_A)), per_b((1, LOGIT_PAD)),
                  per_b((CONV_W - 1, C_CONV)), _const_spec((CONV_W, C_CONV)),
                  per_b((H_A, DK_A, DV_A)), _const_spec((1, DV_A))],
        out_specs=[per_b((1, W_V_A)), per_b((H_A, DK_A, DV_A)), per_b((CONV_W - 1, C_CONV))],
        out_shape=[jax.ShapeDtypeStruct((nb, 1, W_V_A), BF16),
                   jax.ShapeDtypeStruct((nb, H_A, DK_A, DV_A), F32),
                   jax.ShapeDtypeStruct((nb, CONV_W - 1, C_CONV), F32)],
        compiler_params=pltpu.CompilerParams(dimension_semantics=("arbitrary",),
                                             vmem_limit_bytes=VMEM_LIMIT),
    )(a_log, dt_bias, qkv.reshape(nb, 1, C_CONV), z.reshape(nb, 1, W_V_A),
      lg.reshape(nb, 1, LOGIT_PAD), state_conv, conv_w, state_delta, norm_w.reshape(1, DV_A))


def _topk_kernel(qt_ref, km_ref, idx_ref, *, n_full, kk):
    n_seq = km_ref.shape[0]
    qt = qt_ref[...]
    per_seq = []
    for b in range(n_seq):
        prod = km_ref[b] * qt[:, b:b + 1]
        per_seq.append(jnp.concatenate(
            [jnp.sum(prod[h * HD_B:(h + 1) * HD_B], axis=0, keepdims=True) for h in range(H_B)],
            axis=0))
    sc = jnp.concatenate(per_seq, axis=0)
    blk = _iota(sc.shape, 1)
    sc = jnp.where(blk < n_full, sc, NEG_INF)
    rank = _block_rank(sc, n_full, 1)
    blk_f = blk.astype(F32)
    out = jnp.zeros(sc.shape, F32)
    for r in range(kk):
        pick = jnp.sum(jnp.where(rank == float(r), blk_f, 0.0), axis=-1, keepdims=True)
        out = jnp.where(blk == r, pick, out)
    out = out.astype(jnp.int32)
    for b in range(n_seq):
        idx_ref[b] = out[b * H_B:(b + 1) * H_B]


def _topk(q_t, km, *, n_full, kk):
    n_seq = km.shape[0]
    return pl.pallas_call(
        functools.partial(_topk_kernel, n_full=n_full, kk=kk),
        grid=(1,),
        in_specs=[_const_spec(q_t.shape), _const_spec(km.shape)],
        out_specs=pl.BlockSpec((n_seq, H_B, LANES), lambda i: (0, 0, 0)),
        out_shape=jax.ShapeDtypeStruct((n_seq, H_B, LANES), jnp.int32),
        compiler_params=pltpu.CompilerParams(dimension_semantics=("arbitrary",),
                                             vmem_limit_bytes=VMEM_LIMIT),
    )(q_t, km)


_PAGE_DMA_PARAMS = pltpu.CompilerParams(dimension_semantics=("arbitrary",),
                                        vmem_limit_bytes=VMEM_LIMIT, disable_bounds_checks=True)


def _moba_sample_kernel(top_ref, pt_ref, q_ref, kn_ref, vn_ref, ck_ref, cv_ref, o_ref,
                        kbuf, vbuf, sems, *, kk, ppb, n_pages, page_size, n_seq):
    b = pl.program_id(0)
    pages_per_head = kk * ppb

    def copies(seq, h, i, page=None):
        slot = seq % 2
        if page is None:
            blk = top_ref[(seq * H_B + h) * kk + i // ppb]
            page = pt_ref[seq * n_pages + blk * ppb + i % ppb]
        keys = pl.ds(i * page_size, page_size)
        return (pltpu.make_async_copy(ck_ref.at[page, h], kbuf.at[slot, h, :, keys],
                                      sems.at[slot, 0, h]),
                pltpu.make_async_copy(cv_ref.at[page, h], vbuf.at[slot, h, :, keys],
                                      sems.at[slot, 1, h]))

    def start_fetch(seq):
        for h in range(H_B):
            for i in range(pages_per_head):
                for c in copies(seq, h, i):
                    c.start()

    @pl.when(b == 0)
    def _():
        start_fetch(b)

    @pl.when(b + 1 < n_seq)
    def _():
        start_fetch(b + 1)

    slot = b % 2
    q = q_ref[0]
    kn = kn_ref[0]
    vn = vn_ref[0]
    heads = range(H_B)
    for h in heads:
        for i in range(pages_per_head):
            for c in copies(b, h, i, page=0):
                c.wait()
    sl = [slice(h * HD_B, (h + 1) * HD_B) for h in heads]
    qh = [q[:, sl[h]] * (HD_B ** -0.5) for h in heads]
    s = [_mm(_bf(jnp.broadcast_to(qh[h], (SUBLANES, HD_B))), _bf(kbuf[slot, h]))[0:1]
         for h in heads]
    s_own = [jnp.sum(qh[h] * kn[:, sl[h]], axis=-1, keepdims=True) for h in heads]
    m = [jnp.maximum(jnp.max(s[h], axis=-1, keepdims=True), s_own[h]) for h in heads]
    p = [jnp.exp(s[h] - m[h]) for h in heads]
    p_own = [jnp.exp(s_own[h] - m[h]) for h in heads]
    l = [jnp.sum(p[h], axis=-1, keepdims=True) + p_own[h] for h in heads]
    pv = [_mm_nt(_bf(jnp.broadcast_to(p[h], (SUBLANES, p[h].shape[1]))), _bf(vbuf[slot, h]))[0:1]
          for h in heads]
    for h in heads:
        o_ref[0, :, sl[h]] = _bf((pv[h] + p_own[h] * vn[:, sl[h]]) / l[h])


def _moba_sample(top_flat, pt_flat, q3, kn3, vn3, cache_kt, cache_vt, *, kk, ppb):
    nb = q3.shape[0]
    _, _, _, page_size = cache_kt.shape
    keys = kk * ppb * page_size
    row = pl.BlockSpec((1, 1, W_B), lambda b, *_: (b, 0, 0))
    any_spec = pl.BlockSpec(memory_space=pl.ANY)
    return pl.pallas_call(
        functools.partial(_moba_sample_kernel, kk=kk, ppb=ppb,
                          n_pages=pt_flat.shape[0] // nb, page_size=page_size, n_seq=nb),
        grid_spec=pltpu.PrefetchScalarGridSpec(
            num_scalar_prefetch=2,
            grid=(nb,),
            in_specs=[row, row, row, any_spec, any_spec],
            out_specs=row,
            scratch_shapes=[pltpu.VMEM((2, H_B, HD_B, keys), F32),
                            pltpu.VMEM((2, H_B, HD_B, keys), F32),
                            pltpu.SemaphoreType.DMA((2, 2, H_B))]),
        out_shape=jax.ShapeDtypeStruct((nb, 1, W_B), BF16),
        compiler_params=_PAGE_DMA_PARAMS,
    )(top_flat, pt_flat, q3, kn3, vn3, cache_kt, cache_vt)


def _rearranged_w_in(w_in, d_model):
    o_z = C_CONV
    o_lg = o_z + W_V_A
    o_b = o_lg + 2 * H_A
    o_g = o_b + 3 * W_B
    pad = jnp.zeros((d_model, LOGIT_PAD - 2 * H_A), w_in.dtype)
    return _bf(jnp.concatenate(
        [w_in[:, :o_lg], w_in[:, o_b:o_g + 2 * d_model], w_in[:, o_lg:o_b], pad], axis=1))


def kernel(x_prompt, x_sample, cache_k, cache_v, page_table, state_delta, state_conv, norm1_w,
           w_in, conv_w, a_log, dt_bias, delta_norm_w, w_proj_a, w_proj_b, w_out, norm2_w, w_up,
           w_down, norm_f_w):
    depth = w_in.shape[0]
    assert depth == 1, "single-layer trunk"
    bp, sp, d_model = x_prompt.shape
    bs, ss, _ = x_sample.shape
    assert ss == 1, "one new token per sample sequence"
    _, n_pool, page_size, _, _ = cache_k.shape
    n_pages = page_table.shape[1]
    past_len = n_pages * page_size
    ppb = MOBA_BLOCK // page_size
    n_full = past_len // MOBA_BLOCK
    assert n_full * MOBA_BLOCK == past_len, "the sample token starts a fresh MoBA block"
    kk = min(MOBA_TOPK, n_full)
    assert kk >= 1

    w_r = _rearranged_w_in(w_in[0], d_model)
    o_qb = C_CONV + W_V_A + 2 * H_A
    wqkv_t = _bf(w_in[0][:, o_qb:o_qb + 3 * W_B].T)
    wpa, wpb, wo = _bf(w_proj_a[0]), _bf(w_proj_b[0]), _bf(w_out[0])
    wup, wdn = _bf(w_up[0]), _bf(w_down[0])

    tm = 256
    pos_p = jnp.arange(sp, dtype=jnp.int32)
    (qkv, z, qt, kt, vt, ga, gb, lg, conv_p) = _in_proj(
        x_prompt.reshape(bp * sp, d_model), norm1_w[0], w_r, _rope_tables(pos_p),
        tm=tm, rows_per_seq=sp, wqkv_t=wqkv_t, tabs_t=_rope_tables_t(pos_p), conv_w=conv_w[0])
    o_a, s_p = _delta_prompt(qkv.reshape(bp, sp, C_CONV), z.reshape(bp, sp, W_V_A),
                             lg.reshape(bp, sp, LOGIT_PAD), a_log[0], dt_bias[0],
                             delta_norm_w[0])
    o_b = _moba_prompt(qt, kt, vt)
    cache_kt = cache_k.reshape(n_pool, page_size, H_B, HD_B).transpose(0, 2, 3, 1)
    cache_vt = cache_v.reshape(n_pool, page_size, H_B, HD_B).transpose(0, 2, 3, 1)
    past_pages = page_table[:, :n_full * ppb].reshape(-1)
    y_p, km = _merge_mlp(x_prompt.reshape(bp * sp, d_model), o_a.reshape(bp * sp, W_V_A),
                         o_b.reshape(bp * sp, W_B), ga, gb, wpa, wpb, wo, norm2_w[0], wup, wdn,
                         norm_f_w, tm=tm, key_stream=(past_pages, cache_kt, ppb, n_full * ppb))

    pos_s = jnp.full((bs,), past_len, dtype=jnp.int32)
    (qkv_s, z_s, qr_s, kr_s, vr_s, ga_s, gb_s, lg_s) = _in_proj(
        x_sample.reshape(bs, d_model), norm1_w[0], w_r, _rope_tables(pos_s),
        tm=bs, rows_per_seq=bs)
    o_a_s, s_s, conv_s = _delta_step(qkv_s, z_s, lg_s, state_conv[0], conv_w[0], state_delta[0],
                                     a_log[0], dt_bias[0], delta_norm_w[0])
    top = _topk(qr_s.T, km, n_full=n_full, kk=kk)
    top_flat = top[:, :, :kk].reshape(-1)
    o_b_s = _moba_sample(top_flat, page_table.reshape(-1), qr_s.reshape(bs, 1, W_B),
                         kr_s.reshape(bs, 1, W_B), vr_s.reshape(bs, 1, W_B), cache_kt, cache_vt,
                         kk=kk, ppb=ppb)
    y_s = _merge_mlp(x_sample.reshape(bs, d_model), o_a_s.reshape(bs, W_V_A),
                     o_b_s.reshape(bs, W_B), ga_s, gb_s, wpa, wpb, wo, norm2_w[0], wup, wdn,
                     norm_f_w, tm=bs)

    def kv_out(t):
        return t.reshape(1, bp, H_B, HD_B, sp).transpose(0, 1, 4, 2, 3)

    return (y_p.reshape(bp, sp, d_model), y_s.reshape(bs, ss, d_model),
            kv_out(kt), kv_out(vt),
            s_p.reshape(1, bp, H_A, DK_A, DV_A), conv_p.reshape(1, bp, CONV_W - 1, C_CONV),
            kr_s.reshape(1, bs, ss, H_B, HD_B), vr_s.reshape(1, bs, ss, H_B, HD_B),
            s_s.reshape(1, bs, H_A, DK_A, DV_A), conv_s.reshape(1, bs, CONV_W - 1, C_CONV))
```

```python
import functools
import math

import jax
import jax.numpy as jnp
from jax import lax
from jax.experimental import pallas as pl
from jax.experimental.pallas import tpu as pltpu

F32 = jnp.float32
BF16 = jnp.bfloat16
HIGHEST = lax.Precision.HIGHEST

H_A = 4
DK_A = 128
DV_A = 128
W_QK_A = H_A * DK_A
W_V_A = H_A * DV_A
C_CONV = 2 * W_QK_A + W_V_A
CONV_W = 4
H_B = 8
HD_B = 64
W_B = H_B * HD_B
MOBA_BLOCK = 256
MOBA_TOPK = 3
ROPE_THETA = 500000.0
ROT_DIM = HD_B // 4
EPS = 1e-6

LANES = 128
SUBLANES = 8
BF16_SUBLANES = 16
MOBA_HEADS_PER_STEP = 8
KMEAN_SLOTS = 64
KMEAN_GROUP = 16
CONV_ROWS = 64
HEADS_PER_LANE_TILE = LANES // HD_B
LOGIT_PAD = LANES
DELTA_CHUNK = LANES
DELTA_TILE = 4 * DELTA_CHUNK
PREP_CHUNKS = 4
DELTA_STEP_SEQS = 2
VMEM_LIMIT = 56 * 1024 * 1024
NEG_INF = float("-inf")
UNCHOSEN_OFFSET = 1e30


def _dot(a, b, dims, precision=None):
    return lax.dot_general(a, b, (dims, ((), ())), precision=precision,
                           preferred_element_type=F32)


def _mm(a, b, precision=None):
    return _dot(a, b, ((1,), (0,)), precision)


def _mm_nt(a, b, precision=None):
    return _dot(a, b, ((1,), (1,)), precision)


def _bf(x):
    return x.astype(BF16)


def _sigmoid(x):
    return 1.0 / (1.0 + jnp.exp(-x))


def _silu(x):
    return x * _sigmoid(x)


def _softplus(x):
    return jnp.maximum(x, 0.0) + jnp.log(1.0 + jnp.exp(-jnp.abs(x)))


def _rms(x, w):
    return x * lax.rsqrt(jnp.mean(x * x, axis=-1, keepdims=True) + EPS) * w


def _iota(shape, dim):
    return lax.broadcasted_iota(jnp.int32, shape, dim)


def _const_spec(shape):
    nd = len(shape)
    return pl.BlockSpec(shape, lambda *_: (0,) * nd, pipeline_mode=pl.Buffered(1))


def _block_rank(sc, n_blocks, axis):
    blk = _iota(sc.shape, axis)
    rank = jnp.zeros(sc.shape, F32)
    for j in range(n_blocks):
        cj = sc[j:j + 1, :] if axis == 0 else sc[:, j:j + 1]
        first_on_tie = jnp.where(blk > j, 1.0, 0.0)
        rank = rank + jnp.where(cj > sc, 1.0, jnp.where(cj == sc, first_on_tie, 0.0))
    return rank


def _split_bf16(x):
    hi = _bf(x)
    return hi, _bf(x - hi.astype(F32))


def _mm_3pass(a, b):
    a_hi, a_lo = _split_bf16(a)
    b_hi, b_lo = _split_bf16(b)
    return _mm(jnp.concatenate([a_hi, a_lo, a_hi], axis=1),
               jnp.concatenate([b_hi, b_hi, b_lo], axis=0))


def _in_proj_kernel(x_ref, nw_ref, w_ref, cos_ref, slo_ref, shi_ref, *rest,
                    d_model, tiles_per_seq, kv_transposed):
    if kv_transposed:
        (wqkv_t_ref, cos_t_ref, sin_t_ref, cw_ref,
         qkv_ref, z_ref, qb_ref, kb_ref, vb_ref, ga_ref, gb_ref, lg_ref, conv_ref, hist_s) = rest
    else:
        qkv_ref, z_ref, qb_ref, kb_ref, vb_ref, ga_ref, gb_ref, lg_ref = rest
    step_i = pl.program_id(0)
    xb = _bf(_rms(x_ref[...], nw_ref[...]))
    tm = xb.shape[0]
    half = ROT_DIM // 2

    def conv_act_in_place():
        last = qkv_ref[tm - SUBLANES:tm, :]
        conv_ref[0] = last[SUBLANES - (CONV_W - 1):, :]
        for j in range(C_CONV // LANES):
            lanes = slice(j * LANES, (j + 1) * LANES)
            w = cw_ref[:, lanes]
            for rb in reversed(range(tm // CONV_ROWS)):
                r0 = rb * CONV_ROWS
                y = qkv_ref[r0:r0 + CONV_ROWS, lanes]
                prev = hist_s[:, lanes] if rb == 0 else qkv_ref[r0 - SUBLANES:r0, lanes]
                ext = jnp.concatenate([prev, y], axis=0)
                acc = y * w[CONV_W - 1:CONV_W]
                for i in range(CONV_W - 1):
                    lo = SUBLANES - (CONV_W - 1) + i
                    acc = acc + ext[lo:lo + CONV_ROWS] * w[i:i + 1]
                act = _silu(acc)
                if lanes.start < 2 * W_QK_A:
                    act = act * lax.rsqrt(jnp.sum(act * act, axis=-1, keepdims=True) + EPS)
                if lanes.start < W_QK_A:
                    act = act * (DK_A ** -0.5)
                qkv_ref[r0:r0 + CONV_ROWS, lanes] = act
        hist_s[...] = last

    def cols(c0, n):
        return _mm(xb, w_ref[:, c0:c0 + n])

    def rope_store(dst_ref, c0):
        cos, slo, shi = cos_ref[...], slo_ref[...], shi_ref[...]
        for c in range(W_B // LANES):
            y = cols(c0 + c * LANES, LANES)
            up = pltpu.roll(y, LANES - half, 1)
            dn = pltpu.roll(y, half, 1)
            dst_ref[:, c * LANES:(c + 1) * LANES] = y * cos + up * slo + dn * shi

    def rope_store_t(dst_ref, r0):
        cos_t, sin_t = cos_t_ref[...], sin_t_ref[...]
        y_t = _mm_nt(wqkv_t_ref[r0:r0 + W_B, :], xb)
        for h in range(H_B):
            r = h * HD_B
            x1, x2 = y_t[r:r + half], y_t[r + half:r + ROT_DIM]
            dst_ref[0, r:r + half, :] = x1 * cos_t - x2 * sin_t
            dst_ref[0, r + half:r + ROT_DIM, :] = x2 * cos_t + x1 * sin_t
            dst_ref[0, r + ROT_DIM:r + HD_B, :] = y_t[r + ROT_DIM:r + HD_B]

    if kv_transposed:
        @pl.when(step_i % tiles_per_seq == 0)
        def _():
            hist_s[...] = jnp.zeros_like(hist_s)

    step = 512
    c0 = 0
    for c in range(C_CONV // step):
        qkv_ref[:, c0:c0 + step] = cols(c0, step)
        c0 += step
    if kv_transposed:
        conv_act_in_place()
    z_ref[...] = cols(c0, W_V_A)
    c0 += W_V_A
    if kv_transposed:
        rope_store_t(qb_ref, 0)
        rope_store_t(kb_ref, W_B)
        vb_ref[0] = _mm_nt(wqkv_t_ref[2 * W_B:3 * W_B, :], xb)
    else:
        rope_store(qb_ref, c0)
        rope_store(kb_ref, c0 + W_B)
        vb_ref[...] = cols(c0 + 2 * W_B, W_B)
    c0 += 3 * W_B
    for g_ref in (ga_ref, gb_ref):
        for c in range(d_model // step):
            g_ref[:, c * step:(c + 1) * step] = _bf(cols(c0, step))
            c0 += step
    lg_ref[...] = cols(c0, LOGIT_PAD)


def _in_proj(x2d, norm_w, w_r, tabs, *, tm, rows_per_seq, wqkv_t=None, tabs_t=None, conv_w=None):
    rows, d_model = x2d.shape
    n_tiles = rows // tm
    tiles_per_seq = rows_per_seq // tm
    n_seq = rows // rows_per_seq
    kv_transposed = wqkv_t is not None

    def row_spec(n):
        return pl.BlockSpec((tm, n), lambda i: (i, 0))

    def sds(*shape):
        return jax.ShapeDtypeStruct(shape, F32)

    kv_t_spec = pl.BlockSpec((1, W_B, tm), lambda i: (i // tiles_per_seq, 0, i % tiles_per_seq))
    kv_shape = sds(n_seq, W_B, rows_per_seq) if kv_transposed else sds(rows, W_B)
    kv_spec = kv_t_spec if kv_transposed else row_spec(W_B)
    tab_spec = pl.BlockSpec((tm, LANES), lambda i: (i % tiles_per_seq, 0))
    in_specs = [row_spec(d_model), _const_spec((1, d_model)), _const_spec(w_r.shape),
                tab_spec, tab_spec, tab_spec]
    args = [x2d, norm_w.reshape(1, d_model), w_r, *tabs]
    gate_shape = jax.ShapeDtypeStruct((rows, d_model), BF16)
    out_shape = [sds(rows, C_CONV), sds(rows, W_V_A), kv_shape, kv_shape, kv_shape,
                 gate_shape, gate_shape, sds(rows, LOGIT_PAD)]
    out_specs = [row_spec(C_CONV), row_spec(W_V_A), kv_spec, kv_spec, kv_spec,
                 row_spec(d_model), row_spec(d_model), row_spec(LOGIT_PAD)]
    if kv_transposed:
        tab_t_spec = pl.BlockSpec((ROT_DIM // 2, tm), lambda i: (0, i % tiles_per_seq))
        in_specs += [_const_spec(wqkv_t.shape), tab_t_spec, tab_t_spec, _const_spec(conv_w.shape)]
        args += [wqkv_t, *tabs_t, conv_w]
        out_shape.append(sds(n_seq, CONV_W - 1, C_CONV))
        out_specs.append(pl.BlockSpec((1, CONV_W - 1, C_CONV),
                                      lambda i: (i // tiles_per_seq, 0, 0)))
        assert tm >= SUBLANES
        scratch = [pltpu.VMEM((SUBLANES, C_CONV), F32)]
    else:
        scratch = []
    return pl.pallas_call(
        functools.partial(_in_proj_kernel, d_model=d_model, tiles_per_seq=tiles_per_seq,
                          kv_transposed=kv_transposed),
        grid=(n_tiles,),
        in_specs=in_specs,
        out_specs=out_specs,
        out_shape=out_shape,
        scratch_shapes=scratch,
        compiler_params=pltpu.CompilerParams(dimension_semantics=("arbitrary",),
                                             vmem_limit_bytes=VMEM_LIMIT),
    )(*args)


def _rope_angles(pos):
    half = ROT_DIM // 2
    inv_freq = jnp.exp(jnp.arange(half, dtype=F32) * (-2.0 * math.log(ROPE_THETA) / ROT_DIM))
    ang = pos.astype(F32)[:, None] * inv_freq[None, :]
    return jnp.cos(ang), jnp.sin(ang)


def _rope_tables_t(pos):
    cos, sin = _rope_angles(pos)
    return cos.T, sin.T


def _rope_tables(pos):
    half = ROT_DIM // 2
    cos, sin = _rope_angles(pos)
    n = pos.shape[0]
    ones = jnp.ones((n, HD_B - ROT_DIM), F32)
    zeros_h = jnp.zeros((n, half), F32)
    zeros_r = jnp.zeros((n, HD_B - ROT_DIM), F32)
    cos_h = jnp.concatenate([cos, cos, ones], axis=1)
    slo_h = jnp.concatenate([-sin, zeros_h, zeros_r], axis=1)
    shi_h = jnp.concatenate([zeros_h, sin, zeros_r], axis=1)
    tile = lambda t: jnp.tile(t, (1, HEADS_PER_LANE_TILE))
    return tile(cos_h), tile(slo_h), tile(shi_h)


def _delta_prompt_kernel(alog_ref, dtb_ref, x_ref, z_ref, lg_ref, nw_ref,
                         o_ref, s_ref,
                         state_s, u_s, w_s, qg_s, qk_s, kdt_s, gl_s):
    t = pl.program_id(1)
    cs = DELTA_CHUNK
    n_chunks = x_ref.shape[1] // cs
    heads = range(H_A)

    @pl.when(t == 0)
    def _():
        state_s[...] = jnp.zeros_like(state_s)

    sq = (cs, cs)
    row = _iota(sq, 0)
    col = _iota(sq, 1)
    causal = col <= row
    strict = col < row
    tril_b = _bf(jnp.where(causal, 1.0, 0.0))
    eye_f = jnp.where(col == row, 1.0, 0.0)
    lane_row = _iota((1, LANES), 1)
    alog_row = jnp.zeros((1, LANES), F32)
    dtb_row = jnp.zeros((1, LANES), F32)
    for h in heads:
        alog_row = jnp.where(lane_row == H_A + h, alog_ref[h], alog_row)
        dtb_row = jnp.where(lane_row == H_A + h, dtb_ref[h], dtb_row)
    neg_a_row = -jnp.exp(alog_row)

    def prep(ci, carry):
        cidx = [ci * PREP_CHUNKS + d for d in range(PREP_CHUNKS)]
        rows = [pl.ds(pl.multiple_of(c * cs, cs), cs) for c in cidx]
        units = [(d, h) for d in range(PREP_CHUNKS) for h in heads]
        un = range(len(units))
        q = [x_ref[0, rows[d], h * DK_A:(h + 1) * DK_A] for d, h in units]
        k = [x_ref[0, rows[d], W_QK_A + h * DK_A:W_QK_A + (h + 1) * DK_A] for d, h in units]
        v = [x_ref[0, rows[d], 2 * W_QK_A + h * DV_A:2 * W_QK_A + (h + 1) * DV_A] for d, h in units]
        sig, gc_all = [], []
        for d in range(PREP_CHUNKS):
            lg = lg_ref[0, rows[d], :]
            sig.append(_sigmoid(lg))
            rem = neg_a_row * _softplus(lg + dtb_row)
            pieces = []
            for _ in range(3):
                piece = _bf(rem)
                pieces.append(piece)
                rem = rem - piece.astype(F32)
            csum = _mm(tril_b, jnp.concatenate(pieces, axis=1))
            gc_all.append(csum[:, :LANES] + csum[:, LANES:2 * LANES] + csum[:, 2 * LANES:])
        beta = [sig[d][:, h:h + 1] for d, h in units]
        gc = [jnp.broadcast_to(gc_all[d][:, H_A + h:H_A + h + 1], (cs, LANES)) for d, h in units]
        decay = [jnp.exp(jnp.where(causal, gc[u] - gc[u].T, NEG_INF)) for u in un]
        kb = [k[u] * beta[u] for u in un]
        a = [jnp.where(strict, _mm_nt(_bf(kb[u]), _bf(k[u])) * decay[u], 0.0) for u in un]
        t_inv = [eye_f - a[u] for u in un]
        a_pow = [_mm_3pass(a[u], a[u]) for u in un]
        n_levels = int(math.log2(cs)) - 1
        for lvl in range(n_levels):
            if lvl < n_levels - 1:
                prod = [_mm_3pass(a_pow[u], jnp.concatenate([a_pow[u], t_inv[u]], axis=1))
                        for u in un]
                a_pow = [prod[u][:, :cs] for u in un]
                t_inv = [t_inv[u] + prod[u][:, cs:] for u in un]
            else:
                t_inv = [t_inv[u] + _mm_3pass(a_pow[u], t_inv[u]) for u in un]
        e_gc = [jnp.exp(gc[u]) for u in un]
        uw = [_mm(_bf(t_inv[u]),
                  jnp.concatenate([_bf(v[u] * beta[u]), _bf(kb[u] * e_gc[u])], axis=1))
              for u in un]
        qk = [jnp.where(causal, _mm_nt(_bf(q[u]), _bf(k[u])) * decay[u], 0.0) for u in un]
        for u, (d, h) in enumerate(units):
            g_last = gc[u][cs - 1:cs, :]
            u_s[h, rows[d], :] = uw[u][:, :DV_A]
            w_s[h, rows[d], :] = _bf(uw[u][:, DV_A:])
            qk_s[h, rows[d], :] = _bf(qk[u])
            qg_s[h, rows[d], :] = _bf(q[u] * e_gc[u])
            kdt_s[h, cidx[d]] = _bf((k[u] * jnp.exp(g_last - gc[u])).T)
            gl_s[h, cidx[d]] = jnp.broadcast_to(jnp.exp(g_last), (SUBLANES, LANES))
        return carry

    lax.fori_loop(0, n_chunks // PREP_CHUNKS, prep, 0)

    nw = nw_ref[...]

    def scan(c, carry):
        rows = pl.ds(pl.multiple_of(c * cs, cs), cs)
        state = [state_s[h] for h in heads]
        sb = [_bf(state[h]) for h in heads]
        ws = [_mm(jnp.concatenate([w_s[h, rows, :], qg_s[h, rows, :]], axis=0), sb[h])
              for h in heads]
        vb = [_bf(u_s[h, rows, :] - ws[h][:cs]) for h in heads]
        o = [ws[h][cs:] + _mm(qk_s[h, rows, :], vb[h]) for h in heads]
        new = [state[h] * gl_s[h, c][0:1] + _mm(kdt_s[h, c], vb[h]) for h in heads]
        for h in heads:
            state_s[h] = new[h]
            lanes = slice(h * DV_A, (h + 1) * DV_A)
            o_ref[0, rows, lanes] = _bf(_rms(o[h], nw) * _silu(z_ref[0, rows, lanes]))
        return carry

    lax.fori_loop(0, n_chunks, scan, 0)

    @pl.when(t == pl.num_programs(1) - 1)
    def _():
        s_ref[0] = state_s[...]


def _delta_prompt(qkv, z, lg, a_log, dt_bias, norm_w):
    bsz, seq, _ = qkv.shape
    ts = min(DELTA_TILE, seq)
    n_chunks = ts // DELTA_CHUNK
    assert seq % ts == 0 and ts % (DELTA_CHUNK * PREP_CHUNKS) == 0
    assert DK_A == LANES and DV_A == LANES

    def tile_spec(n):
        return pl.BlockSpec((1, ts, n), lambda b, t: (b, t, 0))

    def per_head(shape, dtype):
        return pltpu.VMEM((H_A,) + shape, dtype)

    smem = pl.BlockSpec(memory_space=pltpu.SMEM)
    return pl.pallas_call(
        _delta_prompt_kernel,
        grid=(bsz, seq // ts),
        in_specs=[smem, smem, tile_spec(C_CONV), tile_spec(W_V_A), tile_spec(LOGIT_PAD),
                  _const_spec((1, DV_A))],
        out_specs=[tile_spec(W_V_A),
                   pl.BlockSpec((1, H_A, DK_A, DV_A), lambda b, t: (b, 0, 0, 0))],
        out_shape=[jax.ShapeDtypeStruct((bsz, seq, W_V_A), BF16),
                   jax.ShapeDtypeStruct((bsz, H_A, DK_A, DV_A), F32)],
        scratch_shapes=[per_head((DK_A, DV_A), F32),
                        per_head((ts, DV_A), F32),
                        per_head((ts, DK_A), BF16),
                        per_head((ts, DK_A), BF16),
                        per_head((ts, DELTA_CHUNK), BF16),
                        per_head((n_chunks, DK_A, DELTA_CHUNK), BF16),
                        per_head((n_chunks, SUBLANES, LANES), F32)],
        compiler_params=pltpu.CompilerParams(dimension_semantics=("arbitrary", "arbitrary"),
                                             vmem_limit_bytes=VMEM_LIMIT),
    )(a_log, dt_bias, qkv, z, lg, norm_w.reshape(1, DV_A))


def _moba_prompt_kernel(qt_ref, kt_ref, vt_ref, o_ref, kn_s, vt_s, km_s, ch_s, *, n_blocks):
    qi = pl.program_id(2)
    bs = MOBA_BLOCK
    n_pairs = qt_ref.shape[1] // LANES
    heads = range(n_pairs * HEADS_PER_LANE_TILE)
    pair_of = [h // HEADS_PER_LANE_TILE for h in heads]

    @pl.when(qi == 0)
    def _():
        km_s[...] = jnp.zeros_like(km_s)
        ones = jnp.ones((BF16_SUBLANES, bs), BF16)
        for j in range(n_blocks):
            keys = slice(j * bs, (j + 1) * bs)
            for pr in range(n_pairs):
                kj = kt_ref[0, pr * LANES:(pr + 1) * LANES, keys].T
                kn_s[j, pr] = _bf(kj)
                km_s[pr, j:j + 1, :] = jnp.mean(kj, axis=0, keepdims=True)
            for h in heads:
                vt_s[j, h, 0:HD_B, :] = _bf(vt_ref[0, h * HD_B:(h + 1) * HD_B, keys])
                vt_s[j, h, HD_B:, :] = ones

    d_row = _iota((LANES, bs), 0)
    blk = _iota((km_s.shape[1], bs), 0)
    key_le_query = _iota((bs, bs), 0) <= _iota((bs, bs), 1)
    qh = []
    for h in heads:
        hh = h % HEADS_PER_LANE_TILE
        qt = qt_ref[0, pair_of[h] * LANES:(pair_of[h] + 1) * LANES, :]
        qh.append(jnp.where((d_row >= hh * HD_B) & (d_row < (hh + 1) * HD_B), qt, 0.0))
    sc = [jnp.where(blk < qi, _mm_3pass(km_s[pair_of[h]], qh[h]), NEG_INF) for h in heads]
    rank = [_block_rank(sc[h], n_blocks, 0) for h in heads]
    for h in heads:
        ch_s[h] = jnp.where(blk < qi, jnp.where(rank[h] < MOBA_TOPK, 1.0, 0.0), 0.0)

    qs = [_bf(qh[h] * (HD_B ** -0.5 * math.log2(math.e))) for h in heads]
    s = [jnp.where(key_le_query, _mm(kn_s[qi, pair_of[h]], qs[h]), NEG_INF)
         for h in heads]
    m = [jnp.max(s[h], axis=0, keepdims=True) for h in heads]
    acc = [_mm(vt_s[qi, h], _bf(jnp.exp2(s[h] - m[h]))) for h in heads]

    def two_past_blocks(jj, carry):
        m, acc = carry
        js = (2 * jj, 2 * jj + 1)
        pick = [[ch_s[h, pl.ds(j, 1), :] > 0.0 for j in js] for h in heads]
        s = [[_mm(kn_s[j, pair_of[h]], qs[h]) for j in js] for h in heads]
        col_max = [[jnp.where(pick[h][i], jnp.max(s[h][i], axis=0, keepdims=True), NEG_INF)
                    for i in range(2)] for h in heads]
        m_new = [jnp.maximum(m[h], jnp.maximum(col_max[h][0], col_max[h][1])) for h in heads]
        off = [[jnp.where(pick[h][i], m_new[h], UNCHOSEN_OFFSET) for i in range(2)] for h in heads]
        acc = [jnp.exp2(m[h] - m_new[h]) * acc[h]
               + _mm(vt_s[js[0], h], _bf(jnp.exp2(s[h][0] - off[h][0])))
               + _mm(vt_s[js[1], h], _bf(jnp.exp2(s[h][1] - off[h][1]))) for h in heads]
        return m_new, acc

    m, acc = lax.fori_loop(0, (qi + 1) // 2, two_past_blocks, (m, acc))
    o_t = jnp.concatenate([acc[h][:HD_B] / acc[h][HD_B:HD_B + 1] for h in heads], axis=0)
    o_ref[0] = _bf(o_t.T)


def _moba_prompt(qt, kt, vt):
    bsz, _, seq = qt.shape
    n_blocks = seq // MOBA_BLOCK
    assert seq % MOBA_BLOCK == 0
    blocks_pad = -(-n_blocks // SUBLANES) * SUBLANES
    n_pairs = MOBA_HEADS_PER_STEP // HEADS_PER_LANE_TILE
    rows = n_pairs * LANES
    q_spec = pl.BlockSpec((1, rows, MOBA_BLOCK), lambda b, p, i: (b, p, i))
    kv_spec = pl.BlockSpec((1, rows, seq), lambda b, p, i: (b, p, 0))
    return pl.pallas_call(
        functools.partial(_moba_prompt_kernel, n_blocks=n_blocks),
        grid=(bsz, H_B // MOBA_HEADS_PER_STEP, n_blocks),
        in_specs=[q_spec, kv_spec, kv_spec],
        out_specs=pl.BlockSpec((1, MOBA_BLOCK, rows), lambda b, p, i: (b, i, p)),
        out_shape=jax.ShapeDtypeStruct((bsz, seq, W_B), BF16),
        scratch_shapes=[pltpu.VMEM((n_blocks, n_pairs, MOBA_BLOCK, LANES), BF16),
                        pltpu.VMEM((n_blocks, MOBA_HEADS_PER_STEP, HD_B + BF16_SUBLANES,
                                    MOBA_BLOCK), BF16),
                        pltpu.VMEM((n_pairs, blocks_pad, LANES), F32),
                        pltpu.VMEM((MOBA_HEADS_PER_STEP, blocks_pad, MOBA_BLOCK), F32)],
        compiler_params=pltpu.CompilerParams(
            dimension_semantics=("arbitrary", "arbitrary", "arbitrary"),
            vmem_limit_bytes=VMEM_LIMIT),
    )(qt, kt, vt)


def _key_mean_groups(pt_ref, cache_ref, km_ref, page_buf, page_sems, stream):
    pages_per_step, pages_per_seq, total, ppb, page_size = stream
    step_i = pl.program_id(0)
    base = step_i * pages_per_step
    blocks_per_group = KMEAN_GROUP // ppb
    lane = _iota((HD_B, LANES), 1)

    def page_copy(p, slot):
        return pltpu.make_async_copy(cache_ref.at[pt_ref[p]], page_buf.at[slot], page_sems.at[slot])

    @pl.when(step_i == 0)
    def _():
        for s in range(KMEAN_SLOTS):
            page_copy(s, s).start()

    @pl.when(base % pages_per_seq == 0)
    def _():
        km_ref[...] = jnp.zeros_like(km_ref)

    def consume_group(g):
        p0 = base + g * KMEAN_GROUP
        slot0 = (g * KMEAN_GROUP) % KMEAN_SLOTS
        for j in range(KMEAN_GROUP):
            page_copy(p0 + j, slot0 + j).wait()
        blk0 = (base % pages_per_seq) // ppb + g * blocks_per_group
        for h in range(H_B):
            acc = jnp.zeros((HD_B, LANES), F32)
            for k in range(blocks_per_group):
                x = page_buf[slot0 + k * ppb, h]
                for r in range(1, ppb):
                    x = x + page_buf[slot0 + k * ppb + r, h]
                mean = jnp.sum(x, axis=-1, keepdims=True) * (1.0 / (ppb * page_size))
                acc = jnp.where(lane == blk0 + k, mean, acc)
            rows = slice(h * HD_B, (h + 1) * HD_B)
            km_ref[0, rows, :] = km_ref[0, rows, :] + acc
        for j in range(KMEAN_GROUP):
            @pl.when(p0 + j + KMEAN_SLOTS < total)
            def _():
                page_copy(p0 + j + KMEAN_SLOTS, slot0 + j).start()

    return [functools.partial(consume_group, g) for g in range(pages_per_step // KMEAN_GROUP)]


def _merge_mlp_kernel(*refs, ff_step, stream):
    if stream is None:
        (x_ref, oa_ref, ob_ref, ga_ref, gb_ref, wpa_ref, wpb_ref, wo_ref, n2_ref, wup_ref, wdn_ref,
         nf_ref, y_ref) = refs
        groups = []
    else:
        (pt_ref, x_ref, oa_ref, ob_ref, ga_ref, gb_ref, wpa_ref, wpb_ref, wo_ref, n2_ref, wup_ref,
         wdn_ref, nf_ref, cache_ref, y_ref, km_ref, page_buf, page_sems) = refs
        groups = _key_mean_groups(pt_ref, cache_ref, km_ref, page_buf, page_sems, stream)
    d_ff = wup_ref.shape[1]
    n_ff = d_ff // ff_step

    y_a = _mm(oa_ref[...], wpa_ref[...])
    y_b = _mm(ob_ref[...], wpb_ref[...])
    mixed = (_sigmoid(ga_ref[...].astype(F32)) * y_a
             + _sigmoid(gb_ref[...].astype(F32)) * y_b)
    h = x_ref[...] + _mm(_bf(mixed), wo_ref[...])
    hn = _bf(_rms(h, n2_ref[...]))
    out = h
    for c in range(n_ff):
        for g in groups[c * len(groups) // n_ff:(c + 1) * len(groups) // n_ff]:
            g()
        u = jnp.maximum(_mm(hn, wup_ref[:, c * ff_step:(c + 1) * ff_step]), 0.0)
        out = out + _mm(_bf(u * u), wdn_ref[c * ff_step:(c + 1) * ff_step, :])
    y_ref[...] = _rms(out, nf_ref[...])


def _merge_mlp(x2d, oa, ob, ga, gb, wpa, wpb, wo, n2, wup, wdn, nf, *, tm, key_stream=None):
    rows, d_model = x2d.shape
    d_ff = wup.shape[1]
    n_tiles = rows // tm

    def row_spec(n):
        return pl.BlockSpec((tm, n), lambda i, *_: (i, 0))

    in_specs = [row_spec(d_model), row_spec(W_V_A), row_spec(W_B), row_spec(d_model),
                row_spec(d_model),
                _const_spec(wpa.shape), _const_spec(wpb.shape), _const_spec(wo.shape),
                _const_spec((1, d_model)), _const_spec(wup.shape), _const_spec(wdn.shape),
                _const_spec((1, d_model))]
    args = [x2d, oa, ob, ga, gb, wpa, wpb, wo, n2.reshape(1, d_model), wup, wdn,
            nf.reshape(1, d_model)]
    y_shape = jax.ShapeDtypeStruct((rows, d_model), F32)
    params = dict(dimension_semantics=("arbitrary",), vmem_limit_bytes=VMEM_LIMIT)
    ff_step = min(d_ff, 1024)
    if key_stream is None:
        return pl.pallas_call(
            functools.partial(_merge_mlp_kernel, ff_step=ff_step, stream=None),
            grid=(n_tiles,), in_specs=in_specs, out_specs=row_spec(d_model), out_shape=y_shape,
            compiler_params=pltpu.CompilerParams(**params),
        )(*args)

    page_ids, cache_t, ppb, pages_per_seq = key_stream
    total = page_ids.shape[0]
    _, _, _, page_size = cache_t.shape
    pages_per_step = total // n_tiles
    assert (pages_per_step * n_tiles == total and pages_per_step % KMEAN_SLOTS == 0
            and pages_per_seq % pages_per_step == 0 and KMEAN_GROUP % ppb == 0
            and KMEAN_SLOTS % KMEAN_GROUP == 0 and pages_per_seq // ppb <= LANES)
    km_shape = jax.ShapeDtypeStruct((total // pages_per_seq, W_B, LANES), F32)
    km_spec = pl.BlockSpec((1, W_B, LANES),
                           lambda i, *_: (i * pages_per_step // pages_per_seq, 0, 0))
    return pl.pallas_call(
        functools.partial(_merge_mlp_kernel, ff_step=ff_step,
                          stream=(pages_per_step, pages_per_seq, total, ppb, page_size)),
        grid_spec=pltpu.PrefetchScalarGridSpec(
            num_scalar_prefetch=1, grid=(n_tiles,),
            in_specs=in_specs + [pl.BlockSpec(memory_space=pl.ANY)],
            out_specs=[row_spec(d_model), km_spec],
            scratch_shapes=[pltpu.VMEM((KMEAN_SLOTS, H_B, HD_B, page_size), F32),
                            pltpu.SemaphoreType.DMA((KMEAN_SLOTS,))]),
        out_shape=[y_shape, km_shape],
        compiler_params=pltpu.CompilerParams(disable_bounds_checks=True, **params),
    )(page_ids, *args, cache_t)


def _delta_step_kernel(alog_ref, dtb_ref, x_ref, z_ref, lg_ref, sc_ref, cw_ref, sd_ref, nw_ref,
                       o_ref, snew_ref, cnew_ref):
    n_seq = x_ref.shape[0]
    w = cw_ref[...]
    y = []
    for s in range(n_seq):
        x = x_ref[s]
        hist = sc_ref[s]
        acc = x * w[CONV_W - 1:CONV_W]
        for i in range(CONV_W - 1):
            acc = acc + hist[i:i + 1] * w[i:i + 1]
        y.append(_silu(acc))
        cnew_ref[s, 0:CONV_W - 2, :] = hist[1:CONV_W - 1]
        cnew_ref[s, CONV_W - 2:CONV_W - 1, :] = x
    nw = nw_ref[...]
    sq = (DK_A, DK_A)
    eye = _iota(sq, 0) == _iota(sq, 1)

    def l2n(t):
        return t * lax.rsqrt(jnp.sum(t * t, axis=-1, keepdims=True) + EPS)

    units = [(s, h) for s in range(n_seq) for h in range(H_A)]
    un = range(len(units))
    q = [l2n(y[s][:, h * DK_A:(h + 1) * DK_A]) * (DK_A ** -0.5) for s, h in units]
    k = [l2n(y[s][:, W_QK_A + h * DK_A:W_QK_A + (h + 1) * DK_A]) for s, h in units]
    v = [y[s][:, 2 * W_QK_A + h * DV_A:2 * W_QK_A + (h + 1) * DV_A] for s, h in units]
    beta = [_sigmoid(lg_ref[s][:, h:h + 1]) for s, h in units]
    g = [-jnp.exp(jnp.full((1, 1), alog_ref[h], F32))
         * _softplus(lg_ref[s][:, H_A + h:H_A + h + 1] + dtb_ref[h]) for s, h in units]
    state = [sd_ref[s, h] * jnp.exp(g[u]) for u, (s, h) in enumerate(units)]
    kv = [_mm(jnp.broadcast_to(k[u], (SUBLANES, DK_A)), state[u], HIGHEST)[0:1] for u in un]
    dv = [(v[u] - kv[u]) * beta[u] for u in un]
    k_diag = [jnp.where(eye, jnp.broadcast_to(k[u], sq), 0.0) for u in un]
    state = [state[u] + _mm(k_diag[u], jnp.broadcast_to(dv[u], (DK_A, DV_A)), HIGHEST)
             for u in un]
    o = [_mm(jnp.broadcast_to(q[u], (SUBLANES, DK_A)), state[u], HIGHEST)[0:1] for u in un]
    for u, (s, h) in enumerate(units):
        snew_ref[s, h] = state[u]
        gate = _silu(z_ref[s][:, h * DV_A:(h + 1) * DV_A])
        o_ref[s, :, h * DV_A:(h + 1) * DV_A] = _bf(_rms(o[u], nw) * gate)


def _delta_step(qkv, z, lg, state_conv, conv_w, state_delta, a_log, dt_bias, norm_w):
    nb = qkv.shape[0]
    smem = pl.BlockSpec(memory_space=pltpu.SMEM)

    per_step = math.gcd(nb, DELTA_STEP_SEQS)

    def per_b(shape):
        nd = len(shape)
        return pl.BlockSpec((per_step,) + shape, lambda b: (b,) + (0,) * nd)

    return pl.pallas_call(
        _delta_step_kernel,
        grid=(nb // per_step,),
        in_specs=[smem, smem, per_b((1, C_CONV)), per_b((1, W_V_A)), per_b((1, LOGIT_PAD)),
                  per_b((CONV_W - 1, C_CONV)), _const_spec((CONV_W, C_CONV)),
                  per_b((H_A, DK_A, DV_A)), _const_spec((1, DV_A))],
        out_specs=[per_b((1, W_V_A)), per_b((H_A, DK_A, DV_A)), per_b((CONV_W - 1, C_CONV))],
        out_shape=[jax.ShapeDtypeStruct((nb, 1, W_V_A), BF16),
                   jax.ShapeDtypeStruct((nb, H_A, DK_A, DV_A), F32),
                   jax.ShapeDtypeStruct((nb, CONV_W - 1, C_CONV), F32)],
        compiler_params=pltpu.CompilerParams(dimension_semantics=("arbitrary",),
                                             vmem_limit_bytes=VMEM_LIMIT),
    )(a_log, dt_bias, qkv.reshape(nb, 1, C_CONV), z.reshape(nb, 1, W_V_A),
      lg.reshape(nb, 1, LOGIT_PAD), state_conv, conv_w, state_delta, norm_w.reshape(1, DV_A))


def _topk_kernel(qt_ref, km_ref, idx_ref, *, n_full, kk):
    n_seq = km_ref.shape[0]
    qt = qt_ref[...]
    per_seq = []
    for b in range(n_seq):
        prod = km_ref[b] * qt[:, b:b + 1]
        per_seq.append(jnp.concatenate(
            [jnp.sum(prod[h * HD_B:(h + 1) * HD_B], axis=0, keepdims=True) for h in range(H_B)],
            axis=0))
    sc = jnp.concatenate(per_seq, axis=0)
    blk = _iota(sc.shape, 1)
    sc = jnp.where(blk < n_full, sc, NEG_INF)
    rank = _block_rank(sc, n_full, 1)
    blk_f = blk.astype(F32)
    out = jnp.zeros(sc.shape, F32)
    for r in range(kk):
        pick = jnp.sum(jnp.where(rank == float(r), blk_f, 0.0), axis=-1, keepdims=True)
        out = jnp.where(blk == r, pick, out)
    out = out.astype(jnp.int32)
    for b in range(n_seq):
        idx_ref[b] = out[b * H_B:(b + 1) * H_B]


def _topk(q_t, km, *, n_full, kk):
    n_seq = km.shape[0]
    return pl.pallas_call(
        functools.partial(_topk_kernel, n_full=n_full, kk=kk),
        grid=(1,),
        in_specs=[_const_spec(q_t.shape), _const_spec(km.shape)],
        out_specs=pl.BlockSpec((n_seq, H_B, LANES), lambda i: (0, 0, 0)),
        out_shape=jax.ShapeDtypeStruct((n_seq, H_B, LANES), jnp.int32),
        compiler_params=pltpu.CompilerParams(dimension_semantics=("arbitrary",),
                                             vmem_limit_bytes=VMEM_LIMIT),
    )(q_t, km)


_PAGE_DMA_PARAMS = pltpu.CompilerParams(dimension_semantics=("arbitrary",),
                                        vmem_limit_bytes=VMEM_LIMIT, disable_bounds_checks=True)


def _moba_sample_kernel(top_ref, pt_ref, q_ref, kn_ref, vn_ref, ck_ref, cv_ref, o_ref,
                        kbuf, vbuf, sems, *, kk, ppb, n_pages, page_size, n_seq):
    b = pl.program_id(0)
    pages_per_head = kk * ppb

    def copies(seq, h, i, page=None):
        slot = seq % 2
        if page is None:
            blk = top_ref[(seq * H_B + h) * kk + i // ppb]
            page = pt_ref[seq * n_pages + blk * ppb + i % ppb]
        keys = pl.ds(i * page_size, page_size)
        return (pltpu.make_async_copy(ck_ref.at[page, h], kbuf.at[slot, h, :, keys],
                                      sems.at[slot, 0, h]),
                pltpu.make_async_copy(cv_ref.at[page, h], vbuf.at[slot, h, :, keys],
                                      sems.at[slot, 1, h]))

    def start_fetch(seq):
        for h in range(H_B):
            for i in range(pages_per_head):
                for c in copies(seq, h, i):
                    c.start()

    @pl.when(b == 0)
    def _():
        start_fetch(b)

    @pl.when(b + 1 < n_seq)
    def _():
        start_fetch(b + 1)

    slot = b % 2
    q = q_ref[0]
    kn = kn_ref[0]
    vn = vn_ref[0]
    heads = range(H_B)
    for h in heads:
        for i in range(pages_per_head):
            for c in copies(b, h, i, page=0):
                c.wait()
    sl = [slice(h * HD_B, (h + 1) * HD_B) for h in heads]
    qh = [q[:, sl[h]] * (HD_B ** -0.5) for h in heads]
    s = [_mm(_bf(jnp.broadcast_to(qh[h], (SUBLANES, HD_B))), _bf(kbuf[slot, h]))[0:1]
         for h in heads]
    s_own = [jnp.sum(qh[h] * kn[:, sl[h]], axis=-1, keepdims=True) for h in heads]
    m = [jnp.maximum(jnp.max(s[h], axis=-1, keepdims=True), s_own[h]) for h in heads]
    p = [jnp.exp(s[h] - m[h]) for h in heads]
    p_own = [jnp.exp(s_own[h] - m[h]) for h in heads]
    l = [jnp.sum(p[h], axis=-1, keepdims=True) + p_own[h] for h in heads]
    pv = [_mm_nt(_bf(jnp.broadcast_to(p[h], (SUBLANES, p[h].shape[1]))), _bf(vbuf[slot, h]))[0:1]
          for h in heads]
    for h in heads:
        o_ref[0, :, sl[h]] = _bf((pv[h] + p_own[h] * vn[:, sl[h]]) / l[h])


def _moba_sample(top_flat, pt_flat, q3, kn3, vn3, cache_kt, cache_vt, *, kk, ppb):
    nb = q3.shape[0]
    _, _, _, page_size = cache_kt.shape
    keys = kk * ppb * page_size
    row = pl.BlockSpec((1, 1, W_B), lambda b, *_: (b, 0, 0))
    any_spec = pl.BlockSpec(memory_space=pl.ANY)
    return pl.pallas_call(
        functools.partial(_moba_sample_kernel, kk=kk, ppb=ppb,
                          n_pages=pt_flat.shape[0] // nb, page_size=page_size, n_seq=nb),
        grid_spec=pltpu.PrefetchScalarGridSpec(
            num_scalar_prefetch=2,
            grid=(nb,),
            in_specs=[row, row, row, any_spec, any_spec],
            out_specs=row,
            scratch_shapes=[pltpu.VMEM((2, H_B, HD_B, keys), F32),
                            pltpu.VMEM((2, H_B, HD_B, keys), F32),
                            pltpu.SemaphoreType.DMA((2, 2, H_B))]),
        out_shape=jax.ShapeDtypeStruct((nb, 1, W_B), BF16),
        compiler_params=_PAGE_DMA_PARAMS,
    )(top_flat, pt_flat, q3, kn3, vn3, cache_kt, cache_vt)


def _rearranged_w_in(w_in, d_model):
    o_z = C_CONV
    o_lg = o_z + W_V_A
    o_b = o_lg + 2 * H_A
    o_g = o_b + 3 * W_B
    pad = jnp.zeros((d_model, LOGIT_PAD - 2 * H_A), w_in.dtype)
    return _bf(jnp.concatenate(
        [w_in[:, :o_lg], w_in[:, o_b:o_g + 2 * d_model], w_in[:, o_lg:o_b], pad], axis=1))


def kernel(x_prompt, x_sample, cache_k, cache_v, page_table, state_delta, state_conv, norm1_w,
           w_in, conv_w, a_log, dt_bias, delta_norm_w, w_proj_a, w_proj_b, w_out, norm2_w, w_up,
           w_down, norm_f_w):
    depth = w_in.shape[0]
    assert depth == 1, "single-layer trunk"
    bp, sp, d_model = x_prompt.shape
    bs, ss, _ = x_sample.shape
    assert ss == 1, "one new token per sample sequence"
    _, n_pool, page_size, _, _ = cache_k.shape
    n_pages = page_table.shape[1]
    past_len = n_pages * page_size
    ppb = MOBA_BLOCK // page_size
    n_full = past_len // MOBA_BLOCK
    assert n_full * MOBA_BLOCK == past_len, "the sample token starts a fresh MoBA block"
    kk = min(MOBA_TOPK, n_full)
    assert kk >= 1

    w_r = _rearranged_w_in(w_in[0], d_model)
    o_qb = C_CONV + W_V_A + 2 * H_A
    wqkv_t = _bf(w_in[0][:, o_qb:o_qb + 3 * W_B].T)
    wpa, wpb, wo = _bf(w_proj_a[0]), _bf(w_proj_b[0]), _bf(w_out[0])
    wup, wdn = _bf(w_up[0]), _bf(w_down[0])

    tm = 256
    pos_p = jnp.arange(sp, dtype=jnp.int32)
    (qkv, z, qt, kt, vt, ga, gb, lg, conv_p) = _in_proj(
        x_prompt.reshape(bp * sp, d_model), norm1_w[0], w_r, _rope_tables(pos_p),
        tm=tm, rows_per_seq=sp, wqkv_t=wqkv_t, tabs_t=_rope_tables_t(pos_p), conv_w=conv_w[0])
    o_a, s_p = _delta_prompt(qkv.reshape(bp, sp, C_CONV), z.reshape(bp, sp, W_V_A),
                             lg.reshape(bp, sp, LOGIT_PAD), a_log[0], dt_bias[0],
                             delta_norm_w[0])
    o_b = _moba_prompt(qt, kt, vt)
    cache_kt = cache_k.reshape(n_pool, page_size, H_B, HD_B).transpose(0, 2, 3, 1)
    cache_vt = cache_v.reshape(n_pool, page_size, H_B, HD_B).transpose(0, 2, 3, 1)
    past_pages = page_table[:, :n_full * ppb].reshape(-1)
    y_p, km = _merge_mlp(x_prompt.reshape(bp * sp, d_model), o_a.reshape(bp * sp, W_V_A),
                         o_b.reshape(bp * sp, W_B), ga, gb, wpa, wpb, wo, norm2_w[0], wup, wdn,
                         norm_f_w, tm=tm, key_stream=(past_pages, cache_kt, ppb, n_full * ppb))

    pos_s = jnp.full((bs,), past_len, dtype=jnp.int32)
    (qkv_s, z_s, qr_s, kr_s, vr_s, ga_s, gb_s, lg_s) = _in_proj(
        x_sample.reshape(bs, d_model), norm1_w[0], w_r, _rope_tables(pos_s),
        tm=bs, rows_per_seq=bs)
    o_a_s, s_s, conv_s = _delta_step(qkv_s, z_s, lg_s, state_conv[0], conv_w[0], state_delta[0],
                                     a_log[0], dt_bias[0], delta_norm_w[0])
    top = _topk(qr_s.T, km, n_full=n_full, kk=kk)
    top_flat = top[:, :, :kk].reshape(-1)
    o_b_s = _moba_sample(top_flat, page_table.reshape(-1), qr_s.reshape(bs, 1, W_B),
                         kr_s.reshape(bs, 1, W_B), vr_s.reshape(bs, 1, W_B), cache_kt, cache_vt,
                         kk=kk, ppb=ppb)
    y_s = _merge_mlp(x_sample.reshape(bs, d_model), o_a_s.reshape(bs, W_V_A),
                     o_b_s.reshape(bs, W_B), ga_s, gb_s, wpa, wpb, wo, norm2_w[0], wup, wdn,
                     norm_f_w, tm=bs)

    def kv_out(t):
        return t.reshape(1, bp, H_B, HD_B, sp).transpose(0, 1, 4, 2, 3)

    return (y_p.reshape(bp, sp, d_model), y_s.reshape(bs, ss, d_model),
            kv_out(kt), kv_out(vt),
            s_p.reshape(1, bp, H_A, DK_A, DV_A), conv_p.reshape(1, bp, CONV_W - 1, C_CONV),
            kr_s.reshape(1, bs, ss, H_B, HD_B), vr_s.reshape(1, bs, ss, H_B, HD_B),
            s_s.reshape(1, bs, H_A, DK_A, DV_A), conv_s.reshape(1, bs, CONV_W - 1, C_CONV))
```

```python
import functools
import math

import jax
import jax.numpy as jnp
from jax import lax
from jax.experimental import pallas as pl
from jax.experimental.pallas import tpu as pltpu

F32 = jnp.float32
BF16 = jnp.bfloat16
HIGHEST = lax.Precision.HIGHEST

H_A = 4
DK_A = 128
DV_A = 128
W_QK_A = H_A * DK_A
W_V_A = H_A * DV_A
C_CONV = 2 * W_QK_A + W_V_A
CONV_W = 4
H_B = 8
HD_B = 64
W_B = H_B * HD_B
MOBA_BLOCK = 256
MOBA_TOPK = 3
ROPE_THETA = 500000.0
ROT_DIM = HD_B // 4
EPS = 1e-6

LANES = 128
SUBLANES = 8
BF16_SUBLANES = 16
MOBA_HEADS_PER_STEP = 8
KMEAN_SLOTS = 64
KMEAN_GROUP = 16
CONV_ROWS = 64
HEADS_PER_LANE_TILE = LANES // HD_B
LOGIT_PAD = LANES
DELTA_CHUNK = LANES
DELTA_TILE = 4 * DELTA_CHUNK
PREP_CHUNKS = 4
DELTA_STEP_SEQS = 2
VMEM_LIMIT = 56 * 1024 * 1024
NEG_INF = float("-inf")
UNCHOSEN_OFFSET = 1e30


def _dot(a, b, dims, precision=None):
    return lax.dot_general(a, b, (dims, ((), ())), precision=precision,
                           preferred_element_type=F32)


def _mm(a, b, precision=None):
    return _dot(a, b, ((1,), (0,)), precision)


def _mm_nt(a, b, precision=None):
    return _dot(a, b, ((1,), (1,)), precision)


def _bf(x):
    return x.astype(BF16)


def _sigmoid(x):
    return 1.0 / (1.0 + jnp.exp(-x))


def _silu(x):
    return x * _sigmoid(x)


def _softplus(x):
    return jnp.maximum(x, 0.0) + jnp.log(1.0 + jnp.exp(-jnp.abs(x)))


def _rms(x, w):
    return x * lax.rsqrt(jnp.mean(x * x, axis=-1, keepdims=True) + EPS) * w


def _iota(shape, dim):
    return lax.broadcasted_iota(jnp.int32, shape, dim)


def _const_spec(shape):
    nd = len(shape)
    return pl.BlockSpec(shape, lambda *_: (0,) * nd, pipeline_mode=pl.Buffered(1))


def _block_rank(sc, n_blocks, axis):
    blk = _iota(sc.shape, axis)
    rank = jnp.zeros(sc.shape, F32)
    for j in range(n_blocks):
        cj = sc[j:j + 1, :] if axis == 0 else sc[:, j:j + 1]
        first_on_tie = jnp.where(blk > j, 1.0, 0.0)
        rank = rank + jnp.where(cj > sc, 1.0, jnp.where(cj == sc, first_on_tie, 0.0))
    return rank


def _split_bf16(x):
    hi = _bf(x)
    return hi, _bf(x - hi.astype(F32))


def _mm_3pass(a, b):
    a_hi, a_lo = _split_bf16(a)
    b_hi, b_lo = _split_bf16(b)
    return _mm(jnp.concatenate([a_hi, a_lo, a_hi], axis=1),
               jnp.concatenate([b_hi, b_hi, b_lo], axis=0))


def _in_proj_kernel(x_ref, nw_ref, wa_ref, wb_ref, wlg_ref, cos_ref, slo_ref, shi_ref, *rest,
                    d_model, tiles_per_seq, kv_transposed):
    if kv_transposed:
        (wqkv_t_ref, cos_t_ref, sin_t_ref, cw_ref,
         qkv_ref, z_ref, qb_ref, kb_ref, vb_ref, ga_ref, gb_ref, lg_ref, conv_ref, hist_s) = rest
    else:
        qkv_ref, z_ref, qb_ref, kb_ref, vb_ref, ga_ref, gb_ref, lg_ref = rest
    step_i = pl.program_id(0)
    xb = _bf(_rms(x_ref[...], nw_ref[...]))
    tm = xb.shape[0]
    half = ROT_DIM // 2

    def conv_act_in_place():
        last = qkv_ref[tm - SUBLANES:tm, :]
        conv_ref[0] = last[SUBLANES - (CONV_W - 1):, :]
        for j in range(C_CONV // LANES):
            lanes = slice(j * LANES, (j + 1) * LANES)
            w = cw_ref[:, lanes]
            for rb in reversed(range(tm // CONV_ROWS)):
                r0 = rb * CONV_ROWS
                y = qkv_ref[r0:r0 + CONV_ROWS, lanes]
                prev = hist_s[:, lanes] if rb == 0 else qkv_ref[r0 - SUBLANES:r0, lanes]
                ext = jnp.concatenate([prev, y], axis=0)
                acc = y * w[CONV_W - 1:CONV_W]
                for i in range(CONV_W - 1):
                    lo = SUBLANES - (CONV_W - 1) + i
                    acc = acc + ext[lo:lo + CONV_ROWS] * w[i:i + 1]
                act = _silu(acc)
                if lanes.start < 2 * W_QK_A:
                    act = act * lax.rsqrt(jnp.sum(act * act, axis=-1, keepdims=True) + EPS)
                if lanes.start < W_QK_A:
                    act = act * (DK_A ** -0.5)
                qkv_ref[r0:r0 + CONV_ROWS, lanes] = act
        hist_s[...] = last

    def cols(c0, n):
        n_a, n_b = wa_ref.shape[1], wb_ref.shape[1]
        if c0 < n_a:
            return _mm(xb, wa_ref[:, c0:c0 + n])
        if c0 < n_a + n_b:
            return _mm(xb, wb_ref[:, c0 - n_a:c0 - n_a + n])
        return _mm(xb, wlg_ref[...])

    def rope_store(dst_ref, c0):
        cos, slo, shi = cos_ref[...], slo_ref[...], shi_ref[...]
        for c in range(W_B // LANES):
            y = cols(c0 + c * LANES, LANES)
            up = pltpu.roll(y, LANES - half, 1)
            dn = pltpu.roll(y, half, 1)
            dst_ref[:, c * LANES:(c + 1) * LANES] = y * cos + up * slo + dn * shi

    def rope_store_t(dst_ref, r0):
        cos_t, sin_t = cos_t_ref[...], sin_t_ref[...]
        y_t = _mm_nt(wqkv_t_ref[r0:r0 + W_B, :], xb)
        for h in range(H_B):
            r = h * HD_B
            x1, x2 = y_t[r:r + half], y_t[r + half:r + ROT_DIM]
            dst_ref[0, r:r + half, :] = x1 * cos_t - x2 * sin_t
            dst_ref[0, r + half:r + ROT_DIM, :] = x2 * cos_t + x1 * sin_t
            dst_ref[0, r + ROT_DIM:r + HD_B, :] = y_t[r + ROT_DIM:r + HD_B]

    if kv_transposed:
        @pl.when(step_i % tiles_per_seq == 0)
        def _():
            hist_s[...] = jnp.zeros_like(hist_s)

    step = 512
    c0 = 0
    for c in range(C_CONV // step):
        qkv_ref[:, c0:c0 + step] = cols(c0, step)
        c0 += step
    if kv_transposed:
        conv_act_in_place()
    z_ref[...] = cols(c0, W_V_A)
    c0 += W_V_A
    if kv_transposed:
        rope_store_t(qb_ref, 0)
        rope_store_t(kb_ref, W_B)
        vb_ref[0] = _mm_nt(wqkv_t_ref[2 * W_B:3 * W_B, :], xb)
    else:
        rope_store(qb_ref, c0)
        rope_store(kb_ref, c0 + W_B)
        vb_ref[...] = cols(c0 + 2 * W_B, W_B)
    c0 += 3 * W_B
    for g_ref in (ga_ref, gb_ref):
        for c in range(d_model // step):
            g_ref[:, c * step:(c + 1) * step] = cols(c0, step)
            c0 += step
    lg_ref[...] = cols(c0, LOGIT_PAD)


def _in_proj(x2d, norm_w, w_r, tabs, *, tm, rows_per_seq, wqkv_t=None, tabs_t=None, conv_w=None):
    rows, d_model = x2d.shape
    n_tiles = rows // tm
    tiles_per_seq = rows_per_seq // tm
    n_seq = rows // rows_per_seq
    kv_transposed = wqkv_t is not None

    def row_spec(n):
        return pl.BlockSpec((tm, n), lambda i: (i, 0))

    def sds(*shape):
        return jax.ShapeDtypeStruct(shape, F32)

    kv_t_spec = pl.BlockSpec((1, W_B, tm), lambda i: (i // tiles_per_seq, 0, i % tiles_per_seq))
    kv_shape = sds(n_seq, W_B, rows_per_seq) if kv_transposed else sds(rows, W_B)
    kv_spec = kv_t_spec if kv_transposed else row_spec(W_B)
    tab_spec = pl.BlockSpec((tm, LANES), lambda i: (i % tiles_per_seq, 0))
    in_specs = [row_spec(d_model), _const_spec((1, d_model)), *[_const_spec(w.shape) for w in w_r],
                tab_spec, tab_spec, tab_spec]
    args = [x2d, norm_w.reshape(1, d_model), *w_r, *tabs]
    out_shape = [sds(rows, C_CONV), sds(rows, W_V_A), kv_shape, kv_shape, kv_shape,
                 sds(rows, d_model), sds(rows, d_model), sds(rows, LOGIT_PAD)]
    out_specs = [row_spec(C_CONV), row_spec(W_V_A), kv_spec, kv_spec, kv_spec,
                 row_spec(d_model), row_spec(d_model), row_spec(LOGIT_PAD)]
    if kv_transposed:
        tab_t_spec = pl.BlockSpec((ROT_DIM // 2, tm), lambda i: (0, i % tiles_per_seq))
        in_specs += [_const_spec(wqkv_t.shape), tab_t_spec, tab_t_spec, _const_spec(conv_w.shape)]
        args += [wqkv_t, *tabs_t, conv_w]
        out_shape.append(sds(n_seq, CONV_W - 1, C_CONV))
        out_specs.append(pl.BlockSpec((1, CONV_W - 1, C_CONV),
                                      lambda i: (i // tiles_per_seq, 0, 0)))
        assert tm >= SUBLANES
        scratch = [pltpu.VMEM((SUBLANES, C_CONV), F32)]
    else:
        scratch = []
    return pl.pallas_call(
        functools.partial(_in_proj_kernel, d_model=d_model, tiles_per_seq=tiles_per_seq,
                          kv_transposed=kv_transposed),
        grid=(n_tiles,),
        in_specs=in_specs,
        out_specs=out_specs,
        out_shape=out_shape,
        scratch_shapes=scratch,
        compiler_params=pltpu.CompilerParams(dimension_semantics=("arbitrary",),
                                             vmem_limit_bytes=VMEM_LIMIT),
    )(*args)


def _rope_angles(pos):
    half = ROT_DIM // 2
    inv_freq = jnp.exp(jnp.arange(half, dtype=F32) * (-2.0 * math.log(ROPE_THETA) / ROT_DIM))
    ang = pos.astype(F32)[:, None] * inv_freq[None, :]
    return jnp.cos(ang), jnp.sin(ang)


def _rope_tables_t(pos):
    cos, sin = _rope_angles(pos)
    return cos.T, sin.T


def _rope_tables(pos):
    half = ROT_DIM // 2
    cos, sin = _rope_angles(pos)
    n = pos.shape[0]
    ones = jnp.ones((n, HD_B - ROT_DIM), F32)
    zeros_h = jnp.zeros((n, half), F32)
    zeros_r = jnp.zeros((n, HD_B - ROT_DIM), F32)
    cos_h = jnp.concatenate([cos, cos, ones], axis=1)
    slo_h = jnp.concatenate([-sin, zeros_h, zeros_r], axis=1)
    shi_h = jnp.concatenate([zeros_h, sin, zeros_r], axis=1)
    tile = lambda t: jnp.tile(t, (1, HEADS_PER_LANE_TILE))
    return tile(cos_h), tile(slo_h), tile(shi_h)


def _delta_prompt_kernel(alog_ref, dtb_ref, x_ref, z_ref, lg_ref, nw_ref,
                         o_ref, s_ref,
                         state_s, u_s, w_s, qg_s, qk_s, kdt_s, gl_s):
    t = pl.program_id(1)
    cs = DELTA_CHUNK
    n_chunks = x_ref.shape[1] // cs
    heads = range(H_A)

    @pl.when(t == 0)
    def _():
        state_s[...] = jnp.zeros_like(state_s)

    sq = (cs, cs)
    row = _iota(sq, 0)
    col = _iota(sq, 1)
    causal = col <= row
    strict = col < row
    tril_b = _bf(jnp.where(causal, 1.0, 0.0))
    eye_f = jnp.where(col == row, 1.0, 0.0)
    lane_row = _iota((1, LANES), 1)
    alog_row = jnp.zeros((1, LANES), F32)
    dtb_row = jnp.zeros((1, LANES), F32)
    for h in heads:
        alog_row = jnp.where(lane_row == H_A + h, alog_ref[h], alog_row)
        dtb_row = jnp.where(lane_row == H_A + h, dtb_ref[h], dtb_row)
    neg_a_row = -jnp.exp(alog_row)

    def prep(ci, carry):
        cidx = [ci * PREP_CHUNKS + d for d in range(PREP_CHUNKS)]
        rows = [pl.ds(pl.multiple_of(c * cs, cs), cs) for c in cidx]
        units = [(d, h) for d in range(PREP_CHUNKS) for h in heads]
        un = range(len(units))
        q = [x_ref[0, rows[d], h * DK_A:(h + 1) * DK_A] for d, h in units]
        k = [x_ref[0, rows[d], W_QK_A + h * DK_A:W_QK_A + (h + 1) * DK_A] for d, h in units]
        v = [x_ref[0, rows[d], 2 * W_QK_A + h * DV_A:2 * W_QK_A + (h + 1) * DV_A] for d, h in units]
        sig, gc_all = [], []
        for d in range(PREP_CHUNKS):
            lg = lg_ref[0, rows[d], :]
            sig.append(_sigmoid(lg))
            rem = neg_a_row * _softplus(lg + dtb_row)
            pieces = []
            for _ in range(3):
                piece = _bf(rem)
                pieces.append(piece)
                rem = rem - piece.astype(F32)
            csum = _mm(tril_b, jnp.concatenate(pieces, axis=1))
            gc_all.append(csum[:, :LANES] + csum[:, LANES:2 * LANES] + csum[:, 2 * LANES:])
        beta = [sig[d][:, h:h + 1] for d, h in units]
        gc = [jnp.broadcast_to(gc_all[d][:, H_A + h:H_A + h + 1], (cs, LANES)) for d, h in units]
        decay = [jnp.exp(jnp.where(causal, gc[u] - gc[u].T, NEG_INF)) for u in un]
        kb = [k[u] * beta[u] for u in un]
        a = [jnp.where(strict, _mm_nt(_bf(kb[u]), _bf(k[u])) * decay[u], 0.0) for u in un]
        t_inv = [eye_f - a[u] for u in un]
        a_pow = [_mm_3pass(a[u], a[u]) for u in un]
        n_levels = int(math.log2(cs)) - 1
        for lvl in range(n_levels):
            if lvl < n_levels - 1:
                prod = [_mm_3pass(a_pow[u], jnp.concatenate([a_pow[u], t_inv[u]], axis=1))
                        for u in un]
                a_pow = [prod[u][:, :cs] for u in un]
                t_inv = [t_inv[u] + prod[u][:, cs:] for u in un]
            else:
                t_inv = [t_inv[u] + _mm_3pass(a_pow[u], t_inv[u]) for u in un]
        e_gc = [jnp.exp(gc[u]) for u in un]
        uw = [_mm(_bf(t_inv[u]),
                  jnp.concatenate([_bf(v[u] * beta[u]), _bf(kb[u] * e_gc[u])], axis=1))
              for u in un]
        qk = [jnp.where(causal, _mm_nt(_bf(q[u]), _bf(k[u])) * decay[u], 0.0) for u in un]
        for u, (d, h) in enumerate(units):
            g_last = gc[u][cs - 1:cs, :]
            u_s[h, rows[d], :] = uw[u][:, :DV_A]
            w_s[h, rows[d], :] = _bf(uw[u][:, DV_A:])
            qk_s[h, rows[d], :] = _bf(qk[u])
            qg_s[h, rows[d], :] = _bf(q[u] * e_gc[u])
            kdt_s[h, cidx[d]] = _bf((k[u] * jnp.exp(g_last - gc[u])).T)
            gl_s[h, cidx[d]] = jnp.broadcast_to(jnp.exp(g_last), (SUBLANES, LANES))
        return carry

    lax.fori_loop(0, n_chunks // PREP_CHUNKS, prep, 0)

    nw = nw_ref[...]

    def scan(c, carry):
        rows = pl.ds(pl.multiple_of(c * cs, cs), cs)
        state = [state_s[h] for h in heads]
        sb = [_bf(state[h]) for h in heads]
        ws = [_mm(jnp.concatenate([w_s[h, rows, :], qg_s[h, rows, :]], axis=0), sb[h])
              for h in heads]
        vb = [_bf(u_s[h, rows, :] - ws[h][:cs]) for h in heads]
        o = [ws[h][cs:] + _mm(qk_s[h, rows, :], vb[h]) for h in heads]
        new = [state[h] * gl_s[h, c][0:1] + _mm(kdt_s[h, c], vb[h]) for h in heads]
        for h in heads:
            state_s[h] = new[h]
            lanes = slice(h * DV_A, (h + 1) * DV_A)
            o_ref[0, rows, lanes] = _bf(_rms(o[h], nw) * _silu(z_ref[0, rows, lanes]))
        return carry

    lax.fori_loop(0, n_chunks, scan, 0)

    @pl.when(t == pl.num_programs(1) - 1)
    def _():
        s_ref[0] = state_s[...]


def _delta_prompt(qkv, z, lg, a_log, dt_bias, norm_w):
    bsz, seq, _ = qkv.shape
    ts = min(DELTA_TILE, seq)
    n_chunks = ts // DELTA_CHUNK
    assert seq % ts == 0 and ts % (DELTA_CHUNK * PREP_CHUNKS) == 0
    assert DK_A == LANES and DV_A == LANES

    def tile_spec(n):
        return pl.BlockSpec((1, ts, n), lambda b, t: (b, t, 0))

    def per_head(shape, dtype):
        return pltpu.VMEM((H_A,) + shape, dtype)

    smem = pl.BlockSpec(memory_space=pltpu.SMEM)
    return pl.pallas_call(
        _delta_prompt_kernel,
        grid=(bsz, seq // ts),
        in_specs=[smem, smem, tile_spec(C_CONV), tile_spec(W_V_A), tile_spec(LOGIT_PAD),
                  _const_spec((1, DV_A))],
        out_specs=[tile_spec(W_V_A),
                   pl.BlockSpec((1, H_A, DK_A, DV_A), lambda b, t: (b, 0, 0, 0))],
        out_shape=[jax.ShapeDtypeStruct((bsz, seq, W_V_A), BF16),
                   jax.ShapeDtypeStruct((bsz, H_A, DK_A, DV_A), F32)],
        scratch_shapes=[per_head((DK_A, DV_A), F32),
                        per_head((ts, DV_A), F32),
                        per_head((ts, DK_A), BF16),
                        per_head((ts, DK_A), BF16),
                        per_head((ts, DELTA_CHUNK), BF16),
                        per_head((n_chunks, DK_A, DELTA_CHUNK), BF16),
                        per_head((n_chunks, SUBLANES, LANES), F32)],
        compiler_params=pltpu.CompilerParams(dimension_semantics=("arbitrary", "arbitrary"),
                                             vmem_limit_bytes=VMEM_LIMIT),
    )(a_log, dt_bias, qkv, z, lg, norm_w.reshape(1, DV_A))


def _moba_prompt_kernel(qt_ref, kt_ref, vt_ref, o_ref, kn_s, vt_s, km_s, ch_s, *, n_blocks):
    qi = pl.program_id(2)
    bs = MOBA_BLOCK
    n_pairs = qt_ref.shape[1] // LANES
    heads = range(n_pairs * HEADS_PER_LANE_TILE)
    pair_of = [h // HEADS_PER_LANE_TILE for h in heads]

    @pl.when(qi == 0)
    def _():
        km_s[...] = jnp.zeros_like(km_s)
        ones = jnp.ones((BF16_SUBLANES, bs), BF16)
        for j in range(n_blocks):
            keys = slice(j * bs, (j + 1) * bs)
            for pr in range(n_pairs):
                kj = kt_ref[0, pr * LANES:(pr + 1) * LANES, keys].T
                kn_s[j, pr] = _bf(kj)
                km_s[pr, j:j + 1, :] = jnp.mean(kj, axis=0, keepdims=True)
            for h in heads:
                vt_s[j, h, 0:HD_B, :] = _bf(vt_ref[0, h * HD_B:(h + 1) * HD_B, keys])
                vt_s[j, h, HD_B:, :] = ones

    d_row = _iota((LANES, bs), 0)
    blk = _iota((km_s.shape[1], bs), 0)
    key_le_query = _iota((bs, bs), 0) <= _iota((bs, bs), 1)
    qh = []
    for h in heads:
        hh = h % HEADS_PER_LANE_TILE
        qt = qt_ref[0, pair_of[h] * LANES:(pair_of[h] + 1) * LANES, :]
        qh.append(jnp.where((d_row >= hh * HD_B) & (d_row < (hh + 1) * HD_B), qt, 0.0))
    sc = [jnp.where(blk < qi, _mm_3pass(km_s[pair_of[h]], qh[h]), NEG_INF) for h in heads]
    rank = [_block_rank(sc[h], n_blocks, 0) for h in heads]
    for h in heads:
        ch_s[h] = jnp.where(blk < qi, jnp.where(rank[h] < MOBA_TOPK, 1.0, 0.0), 0.0)

    qs = [_bf(qh[h] * (HD_B ** -0.5 * math.log2(math.e))) for h in heads]
    s = [jnp.where(key_le_query, _mm(kn_s[qi, pair_of[h]], qs[h]), NEG_INF)
         for h in heads]
    m = [jnp.max(s[h], axis=0, keepdims=True) for h in heads]
    acc = [_mm(vt_s[qi, h], _bf(jnp.exp2(s[h] - m[h]))) for h in heads]

    def two_past_blocks(jj, carry):
        m, acc = carry
        js = (2 * jj, 2 * jj + 1)
        pick = [[ch_s[h, pl.ds(j, 1), :] > 0.0 for j in js] for h in heads]
        s = [[_mm(kn_s[j, pair_of[h]], qs[h]) for j in js] for h in heads]
        col_max = [[jnp.where(pick[h][i], jnp.max(s[h][i], axis=0, keepdims=True), NEG_INF)
                    for i in range(2)] for h in heads]
        m_new = [jnp.maximum(m[h], jnp.maximum(col_max[h][0], col_max[h][1])) for h in heads]
        off = [[jnp.where(pick[h][i], m_new[h], UNCHOSEN_OFFSET) for i in range(2)] for h in heads]
        acc = [jnp.exp2(m[h] - m_new[h]) * acc[h]
               + _mm(vt_s[js[0], h], _bf(jnp.exp2(s[h][0] - off[h][0])))
               + _mm(vt_s[js[1], h], _bf(jnp.exp2(s[h][1] - off[h][1]))) for h in heads]
        return m_new, acc

    m, acc = lax.fori_loop(0, (qi + 1) // 2, two_past_blocks, (m, acc))
    o_t = jnp.concatenate([acc[h][:HD_B] / acc[h][HD_B:HD_B + 1] for h in heads], axis=0)
    o_ref[0] = _bf(o_t.T)


def _moba_prompt(qt, kt, vt):
    bsz, _, seq = qt.shape
    n_blocks = seq // MOBA_BLOCK
    assert seq % MOBA_BLOCK == 0
    blocks_pad = -(-n_blocks // SUBLANES) * SUBLANES
    n_pairs = MOBA_HEADS_PER_STEP // HEADS_PER_LANE_TILE
    rows = n_pairs * LANES
    q_spec = pl.BlockSpec((1, rows, MOBA_BLOCK), lambda b, p, i: (b, p, i))
    kv_spec = pl.BlockSpec((1, rows, seq), lambda b, p, i: (b, p, 0))
    return pl.pallas_call(
        functools.partial(_moba_prompt_kernel, n_blocks=n_blocks),
        grid=(bsz, H_B // MOBA_HEADS_PER_STEP, n_blocks),
        in_specs=[q_spec, kv_spec, kv_spec],
        out_specs=pl.BlockSpec((1, MOBA_BLOCK, rows), lambda b, p, i: (b, i, p)),
        out_shape=jax.ShapeDtypeStruct((bsz, seq, W_B), BF16),
        scratch_shapes=[pltpu.VMEM((n_blocks, n_pairs, MOBA_BLOCK, LANES), BF16),
                        pltpu.VMEM((n_blocks, MOBA_HEADS_PER_STEP, HD_B + BF16_SUBLANES,
                                    MOBA_BLOCK), BF16),
                        pltpu.VMEM((n_pairs, blocks_pad, LANES), F32),
                        pltpu.VMEM((MOBA_HEADS_PER_STEP, blocks_pad, MOBA_BLOCK), F32)],
        compiler_params=pltpu.CompilerParams(
            dimension_semantics=("arbitrary", "arbitrary", "arbitrary"),
            vmem_limit_bytes=VMEM_LIMIT),
    )(qt, kt, vt)


def _key_mean_groups(pt_ref, cache_ref, km_ref, page_buf, page_sems, stream):
    pages_per_step, pages_per_seq, total, ppb, page_size = stream
    step_i = pl.program_id(0)
    base = step_i * pages_per_step
    blocks_per_group = KMEAN_GROUP // ppb
    lane = _iota((HD_B, LANES), 1)

    def page_copy(p, slot):
        return pltpu.make_async_copy(cache_ref.at[pt_ref[p]], page_buf.at[slot], page_sems.at[slot])

    @pl.when(step_i == 0)
    def _():
        for s in range(KMEAN_SLOTS):
            page_copy(s, s).start()

    @pl.when(base % pages_per_seq == 0)
    def _():
        km_ref[...] = jnp.zeros_like(km_ref)

    def consume_group(g):
        p0 = base + g * KMEAN_GROUP
        slot0 = (g * KMEAN_GROUP) % KMEAN_SLOTS
        for j in range(KMEAN_GROUP):
            page_copy(p0 + j, slot0 + j).wait()
        blk0 = (base % pages_per_seq) // ppb + g * blocks_per_group
        for h in range(H_B):
            acc = jnp.zeros((HD_B, LANES), F32)
            for k in range(blocks_per_group):
                x = page_buf[slot0 + k * ppb, h]
                for r in range(1, ppb):
                    x = x + page_buf[slot0 + k * ppb + r, h]
                mean = jnp.sum(x, axis=-1, keepdims=True) * (1.0 / (ppb * page_size))
                acc = jnp.where(lane == blk0 + k, mean, acc)
            rows = slice(h * HD_B, (h + 1) * HD_B)
            km_ref[0, rows, :] = km_ref[0, rows, :] + acc
        for j in range(KMEAN_GROUP):
            @pl.when(p0 + j + KMEAN_SLOTS < total)
            def _():
                page_copy(p0 + j + KMEAN_SLOTS, slot0 + j).start()

    return [functools.partial(consume_group, g) for g in range(pages_per_step // KMEAN_GROUP)]


def _merge_mlp_kernel(*refs, ff_step, stream):
    if stream is None:
        (x_ref, oa_ref, ob_ref, ga_ref, gb_ref, wpa_ref, wpb_ref, wo_ref, n2_ref, wup_ref, wdn_ref,
         nf_ref, y_ref) = refs
        groups = []
    else:
        (pt_ref, x_ref, oa_ref, ob_ref, ga_ref, gb_ref, wpa_ref, wpb_ref, wo_ref, n2_ref, wup_ref,
         wdn_ref, nf_ref, cache_ref, y_ref, km_ref, page_buf, page_sems) = refs
        groups = _key_mean_groups(pt_ref, cache_ref, km_ref, page_buf, page_sems, stream)
    d_ff = wup_ref.shape[1]
    n_ff = d_ff // ff_step

    y_a = _mm(oa_ref[...], wpa_ref[...])
    y_b = _mm(ob_ref[...], wpb_ref[...])
    mixed = _sigmoid(ga_ref[...]) * y_a + _sigmoid(gb_ref[...]) * y_b
    h = x_ref[...] + _mm(_bf(mixed), wo_ref[...])
    hn = _bf(_rms(h, n2_ref[...]))
    out = h
    for c in range(n_ff):
        for g in groups[c * len(groups) // n_ff:(c + 1) * len(groups) // n_ff]:
            g()
        u = jnp.maximum(_mm(hn, wup_ref[:, c * ff_step:(c + 1) * ff_step]), 0.0)
        out = out + _mm(_bf(u * u), wdn_ref[c * ff_step:(c + 1) * ff_step, :])
    y_ref[...] = _rms(out, nf_ref[...])


def _merge_mlp(x2d, oa, ob, ga, gb, wpa, wpb, wo, n2, wup, wdn, nf, *, tm, key_stream=None):
    rows, d_model = x2d.shape
    d_ff = wup.shape[1]
    n_tiles = rows // tm

    def row_spec(n):
        return pl.BlockSpec((tm, n), lambda i, *_: (i, 0))

    in_specs = [row_spec(d_model), row_spec(W_V_A), row_spec(W_B), row_spec(d_model),
                row_spec(d_model),
                _const_spec(wpa.shape), _const_spec(wpb.shape), _const_spec(wo.shape),
                _const_spec((1, d_model)), _const_spec(wup.shape), _const_spec(wdn.shape),
                _const_spec((1, d_model))]
    args = [x2d, oa, ob, ga, gb, wpa, wpb, wo, n2.reshape(1, d_model), wup, wdn,
            nf.reshape(1, d_model)]
    y_shape = jax.ShapeDtypeStruct((rows, d_model), F32)
    params = dict(dimension_semantics=("arbitrary",), vmem_limit_bytes=VMEM_LIMIT)
    ff_step = min(d_ff, 1024)
    if key_stream is None:
        return pl.pallas_call(
            functools.partial(_merge_mlp_kernel, ff_step=ff_step, stream=None),
            grid=(n_tiles,), in_specs=in_specs, out_specs=row_spec(d_model), out_shape=y_shape,
            compiler_params=pltpu.CompilerParams(**params),
        )(*args)

    page_ids, cache_t, ppb, pages_per_seq = key_stream
    total = page_ids.shape[0]
    _, _, _, page_size = cache_t.shape
    pages_per_step = total // n_tiles
    assert (pages_per_step * n_tiles == total and pages_per_step % KMEAN_SLOTS == 0
            and pages_per_seq % pages_per_step == 0 and KMEAN_GROUP % ppb == 0
            and KMEAN_SLOTS % KMEAN_GROUP == 0 and pages_per_seq // ppb <= LANES)
    km_shape = jax.ShapeDtypeStruct((total // pages_per_seq, W_B, LANES), F32)
    km_spec = pl.BlockSpec((1, W_B, LANES),
                           lambda i, *_: (i * pages_per_step // pages_per_seq, 0, 0))
    return pl.pallas_call(
        functools.partial(_merge_mlp_kernel, ff_step=ff_step,
                          stream=(pages_per_step, pages_per_seq, total, ppb, page_size)),
        grid_spec=pltpu.PrefetchScalarGridSpec(
            num_scalar_prefetch=1, grid=(n_tiles,),
            in_specs=in_specs + [pl.BlockSpec(memory_space=pl.ANY)],
            out_specs=[row_spec(d_model), km_spec],
            scratch_shapes=[pltpu.VMEM((KMEAN_SLOTS, H_B, HD_B, page_size), F32),
                            pltpu.SemaphoreType.DMA((KMEAN_SLOTS,))]),
        out_shape=[y_shape, km_shape],
        compiler_params=pltpu.CompilerParams(disable_bounds_checks=True, **params),
    )(page_ids, *args, cache_t)


def _delta_step_kernel(alog_ref, dtb_ref, x_ref, z_ref, lg_ref, sc_ref, cw_ref, sd_ref, nw_ref,
                       o_ref, snew_ref, cnew_ref):
    n_seq = x_ref.shape[0]
    w = cw_ref[...]
    y = []
    for s in range(n_seq):
        x = x_ref[s]
        hist = sc_ref[s]
        acc = x * w[CONV_W - 1:CONV_W]
        for i in range(CONV_W - 1):
            acc = acc + hist[i:i + 1] * w[i:i + 1]
        y.append(_silu(acc))
        cnew_ref[s, 0:CONV_W - 2, :] = hist[1:CONV_W - 1]
        cnew_ref[s, CONV_W - 2:CONV_W - 1, :] = x
    nw = nw_ref[...]
    sq = (DK_A, DK_A)
    eye = _iota(sq, 0) == _iota(sq, 1)

    def l2n(t):
        return t * lax.rsqrt(jnp.sum(t * t, axis=-1, keepdims=True) + EPS)

    units = [(s, h) for s in range(n_seq) for h in range(H_A)]
    un = range(len(units))
    q = [l2n(y[s][:, h * DK_A:(h + 1) * DK_A]) * (DK_A ** -0.5) for s, h in units]
    k = [l2n(y[s][:, W_QK_A + h * DK_A:W_QK_A + (h + 1) * DK_A]) for s, h in units]
    v = [y[s][:, 2 * W_QK_A + h * DV_A:2 * W_QK_A + (h + 1) * DV_A] for s, h in units]
    beta = [_sigmoid(lg_ref[s][:, h:h + 1]) for s, h in units]
    g = [-jnp.exp(jnp.full((1, 1), alog_ref[h], F32))
         * _softplus(lg_ref[s][:, H_A + h:H_A + h + 1] + dtb_ref[h]) for s, h in units]
    state = [sd_ref[s, h] * jnp.exp(g[u]) for u, (s, h) in enumerate(units)]
    kv = [_mm(jnp.broadcast_to(k[u], (SUBLANES, DK_A)), state[u], HIGHEST)[0:1] for u in un]
    dv = [(v[u] - kv[u]) * beta[u] for u in un]
    k_diag = [jnp.where(eye, jnp.broadcast_to(k[u], sq), 0.0) for u in un]
    state = [state[u] + _mm(k_diag[u], jnp.broadcast_to(dv[u], (DK_A, DV_A)), HIGHEST)
             for u in un]
    o = [_mm(jnp.broadcast_to(q[u], (SUBLANES, DK_A)), state[u], HIGHEST)[0:1] for u in un]
    for u, (s, h) in enumerate(units):
        snew_ref[s, h] = state[u]
        gate = _silu(z_ref[s][:, h * DV_A:(h + 1) * DV_A])
        o_ref[s, :, h * DV_A:(h + 1) * DV_A] = _bf(_rms(o[u], nw) * gate)


def _delta_step(qkv, z, lg, state_conv, conv_w, state_delta, a_log, dt_bias, norm_w):
    nb = qkv.shape[0]
    smem = pl.BlockSpec(memory_space=pltpu.SMEM)

    per_step = math.gcd(nb, DELTA_STEP_SEQS)

    def per_b(shape):
        nd = len(shape)
        return pl.BlockSpec((per_step,) + shape, lambda b: (b,) + (0,) * nd)

    return pl.pallas_call(
        _delta_step_kernel,
        grid=(nb // per_step,),
        in_specs=[smem, smem, per_b((1, C_CONV)), per_b((1, W_V_A)), per_b((1, LOGIT_PAD)),
                  per_b((CONV_W - 1, C_CONV)), _const_spec((CONV_W, C_CONV)),
                  per_b((H_A, DK_A, DV_A)), _const_spec((1, DV_A))],
        out_specs=[per_b((1, W_V_A)), per_b((H_A, DK_A, DV_A)), per_b((CONV_W - 1, C_CONV))],
        out_shape=[jax.ShapeDtypeStruct((nb, 1, W_V_A), BF16),
                   jax.ShapeDtypeStruct((nb, H_A, DK_A, DV_A), F32),
                   jax.ShapeDtypeStruct((nb, CONV_W - 1, C_CONV), F32)],
        compiler_params=pltpu.CompilerParams(dimension_semantics=("arbitrary",),
                                             vmem_limit_bytes=VMEM_LIMIT),
    )(a_log, dt_bias, qkv.reshape(nb, 1, C_CONV), z.reshape(nb, 1, W_V_A),
      lg.reshape(nb, 1, LOGIT_PAD), state_conv, conv_w, state_delta, norm_w.reshape(1, DV_A))


def _topk_kernel(qt_ref, km_ref, idx_ref, *, n_full, kk):
    n_seq = km_ref.shape[0]
    qt = qt_ref[...]
    per_seq = []
    for b in range(n_seq):
        prod = km_ref[b] * qt[:, b:b + 1]
        per_seq.append(jnp.concatenate(
            [jnp.sum(prod[h * HD_B:(h + 1) * HD_B], axis=0, keepdims=True) for h in range(H_B)],
            axis=0))
    sc = jnp.concatenate(per_seq, axis=0)
    blk = _iota(sc.shape, 1)
    sc = jnp.where(blk < n_full, sc, NEG_INF)
    rank = _block_rank(sc, n_full, 1)
    blk_f = blk.astype(F32)
    out = jnp.zeros(sc.shape, F32)
    for r in range(kk):
        pick = jnp.sum(jnp.where(rank == float(r), blk_f, 0.0), axis=-1, keepdims=True)
        out = jnp.where(blk == r, pick, out)
    out = out.astype(jnp.int32)
    for b in range(n_seq):
        idx_ref[b] = out[b * H_B:(b + 1) * H_B]


def _topk(q_t, km, *, n_full, kk):
    n_seq = km.shape[0]
    return pl.pallas_call(
        functools.partial(_topk_kernel, n_full=n_full, kk=kk),
        grid=(1,),
        in_specs=[_const_spec(q_t.shape), _const_spec(km.shape)],
        out_specs=pl.BlockSpec((n_seq, H_B, LANES), lambda i: (0, 0, 0)),
        out_shape=jax.ShapeDtypeStruct((n_seq, H_B, LANES), jnp.int32),
        compiler_params=pltpu.CompilerParams(dimension_semantics=("arbitrary",),
                                             vmem_limit_bytes=VMEM_LIMIT),
    )(q_t, km)


_PAGE_DMA_PARAMS = pltpu.CompilerParams(dimension_semantics=("arbitrary",),
                                        vmem_limit_bytes=VMEM_LIMIT, disable_bounds_checks=True)


def _moba_sample_kernel(top_ref, pt_ref, q_ref, kn_ref, vn_ref, ck_ref, cv_ref, o_ref,
                        kbuf, vbuf, sems, *, kk, ppb, n_pages, page_size, n_seq):
    b = pl.program_id(0)
    pages_per_head = kk * ppb

    def copies(seq, h, i, page=None):
        slot = seq % 2
        if page is None:
            blk = top_ref[(seq * H_B + h) * kk + i // ppb]
            page = pt_ref[seq * n_pages + blk * ppb + i % ppb]
        keys = pl.ds(i * page_size, page_size)
        return (pltpu.make_async_copy(ck_ref.at[page, h], kbuf.at[slot, h, :, keys],
                                      sems.at[slot, 0, h]),
                pltpu.make_async_copy(cv_ref.at[page, h], vbuf.at[slot, h, :, keys],
                                      sems.at[slot, 1, h]))

    def start_fetch(seq):
        for h in range(H_B):
            for i in range(pages_per_head):
                for c in copies(seq, h, i):
                    c.start()

    @pl.when(b == 0)
    def _():
        start_fetch(b)

    @pl.when(b + 1 < n_seq)
    def _():
        start_fetch(b + 1)

    slot = b % 2
    q = q_ref[0]
    kn = kn_ref[0]
    vn = vn_ref[0]
    heads = range(H_B)
    for h in heads:
        for i in range(pages_per_head):
            for c in copies(b, h, i, page=0):
                c.wait()
    sl = [slice(h * HD_B, (h + 1) * HD_B) for h in heads]
    qh = [q[:, sl[h]] * (HD_B ** -0.5) for h in heads]
    s = [_mm(_bf(jnp.broadcast_to(qh[h], (SUBLANES, HD_B))), _bf(kbuf[slot, h]))[0:1]
         for h in heads]
    s_own = [jnp.sum(qh[h] * kn[:, sl[h]], axis=-1, keepdims=True) for h in heads]
    m = [jnp.maximum(jnp.max(s[h], axis=-1, keepdims=True), s_own[h]) for h in heads]
    p = [jnp.exp(s[h] - m[h]) for h in heads]
    p_own = [jnp.exp(s_own[h] - m[h]) for h in heads]
    l = [jnp.sum(p[h], axis=-1, keepdims=True) + p_own[h] for h in heads]
    pv = [_mm_nt(_bf(jnp.broadcast_to(p[h], (SUBLANES, p[h].shape[1]))), _bf(vbuf[slot, h]))[0:1]
          for h in heads]
    for h in heads:
        o_ref[0, :, sl[h]] = _bf((pv[h] + p_own[h] * vn[:, sl[h]]) / l[h])


def _moba_sample(top_flat, pt_flat, q3, kn3, vn3, cache_kt, cache_vt, *, kk, ppb):
    nb = q3.shape[0]
    _, _, _, page_size = cache_kt.shape
    keys = kk * ppb * page_size
    row = pl.BlockSpec((1, 1, W_B), lambda b, *_: (b, 0, 0))
    any_spec = pl.BlockSpec(memory_space=pl.ANY)
    return pl.pallas_call(
        functools.partial(_moba_sample_kernel, kk=kk, ppb=ppb,
                          n_pages=pt_flat.shape[0] // nb, page_size=page_size, n_seq=nb),
        grid_spec=pltpu.PrefetchScalarGridSpec(
            num_scalar_prefetch=2,
            grid=(nb,),
            in_specs=[row, row, row, any_spec, any_spec],
            out_specs=row,
            scratch_shapes=[pltpu.VMEM((2, H_B, HD_B, keys), F32),
                            pltpu.VMEM((2, H_B, HD_B, keys), F32),
                            pltpu.SemaphoreType.DMA((2, 2, H_B))]),
        out_shape=jax.ShapeDtypeStruct((nb, 1, W_B), BF16),
        compiler_params=_PAGE_DMA_PARAMS,
    )(top_flat, pt_flat, q3, kn3, vn3, cache_kt, cache_vt)


def _rearranged_w_in(w_in, d_model):
    o_lg = C_CONV + W_V_A
    o_b = o_lg + 2 * H_A
    w = _bf(w_in)
    pad = jnp.zeros((d_model, LOGIT_PAD - 2 * H_A), BF16)
    return w[:, :o_lg], w[:, o_b:], jnp.concatenate([w[:, o_lg:o_b], pad], axis=1)


def kernel(x_prompt, x_sample, cache_k, cache_v, page_table, state_delta, state_conv, norm1_w,
           w_in, conv_w, a_log, dt_bias, delta_norm_w, w_proj_a, w_proj_b, w_out, norm2_w, w_up,
           w_down, norm_f_w):
    depth = w_in.shape[0]
    assert depth == 1, "single-layer trunk"
    bp, sp, d_model = x_prompt.shape
    bs, ss, _ = x_sample.shape
    assert ss == 1, "one new token per sample sequence"
    _, n_pool, page_size, _, _ = cache_k.shape
    n_pages = page_table.shape[1]
    past_len = n_pages * page_size
    ppb = MOBA_BLOCK // page_size
    n_full = past_len // MOBA_BLOCK
    assert n_full * MOBA_BLOCK == past_len, "the sample token starts a fresh MoBA block"
    kk = min(MOBA_TOPK, n_full)
    assert kk >= 1

    w_r = _rearranged_w_in(w_in[0], d_model)
    o_qb = C_CONV + W_V_A + 2 * H_A
    wqkv_t = _bf(w_in[0][:, o_qb:o_qb + 3 * W_B].T)
    wpa, wpb, wo = _bf(w_proj_a[0]), _bf(w_proj_b[0]), _bf(w_out[0])
    wup, wdn = _bf(w_up[0]), _bf(w_down[0])

    tm = 256
    pos_p = jnp.arange(sp, dtype=jnp.int32)
    (qkv, z, qt, kt, vt, ga, gb, lg, conv_p) = _in_proj(
        x_prompt.reshape(bp * sp, d_model), norm1_w[0], w_r, _rope_tables(pos_p),
        tm=2 * tm, rows_per_seq=sp, wqkv_t=wqkv_t, tabs_t=_rope_tables_t(pos_p), conv_w=conv_w[0])
    o_a, s_p = _delta_prompt(qkv.reshape(bp, sp, C_CONV), z.reshape(bp, sp, W_V_A),
                             lg.reshape(bp, sp, LOGIT_PAD), a_log[0], dt_bias[0],
                             delta_norm_w[0])
    o_b = _moba_prompt(qt, kt, vt)
    cache_kt = cache_k.reshape(n_pool, page_size, H_B, HD_B).transpose(0, 2, 3, 1)
    cache_vt = cache_v.reshape(n_pool, page_size, H_B, HD_B).transpose(0, 2, 3, 1)
    past_pages = page_table[:, :n_full * ppb].reshape(-1)
    y_p, km = _merge_mlp(x_prompt.reshape(bp * sp, d_model), o_a.reshape(bp * sp, W_V_A),
                         o_b.reshape(bp * sp, W_B), ga, gb, wpa, wpb, wo, norm2_w[0], wup, wdn,
                         norm_f_w, tm=tm, key_stream=(past_pages, cache_kt, ppb, n_full * ppb))

    pos_s = jnp.full((bs,), past_len, dtype=jnp.int32)
    (qkv_s, z_s, qr_s, kr_s, vr_s, ga_s, gb_s, lg_s) = _in_proj(
        x_sample.reshape(bs, d_model), norm1_w[0], w_r, _rope_tables(pos_s),
        tm=bs, rows_per_seq=bs)
    o_a_s, s_s, conv_s = _delta_step(qkv_s, z_s, lg_s, state_conv[0], conv_w[0], state_delta[0],
                                     a_log[0], dt_bias[0], delta_norm_w[0])
    top = _topk(qr_s.T, km, n_full=n_full, kk=kk)
    top_flat = top[:, :, :kk].reshape(-1)
    o_b_s = _moba_sample(top_flat, page_table.reshape(-1), qr_s.reshape(bs, 1, W_B),
                         kr_s.reshape(bs, 1, W_B), vr_s.reshape(bs, 1, W_B), cache_kt, cache_vt,
                         kk=kk, ppb=ppb)
    y_s = _merge_mlp(x_sample.reshape(bs, d_model), o_a_s.reshape(bs, W_V_A),
                     o_b_s.reshape(bs, W_B), ga_s, gb_s, wpa, wpb, wo, norm2_w[0], wup, wdn,
                     norm_f_w, tm=bs)

    def kv_out(t):
        return t.reshape(1, bp, H_B, HD_B, sp).transpose(0, 1, 4, 2, 3)

    return (y_p.reshape(bp, sp, d_model), y_s.reshape(bs, ss, d_model),
            kv_out(kt), kv_out(vt),
            s_p.reshape(1, bp, H_A, DK_A, DV_A), conv_p.reshape(1, bp, CONV_W - 1, C_CONV),
            kr_s.reshape(1, bs, ss, H_B, HD_B), vr_s.reshape(1, bs, ss, H_B, HD_B),
            s_s.reshape(1, bs, H_A, DK_A, DV_A), conv_s.reshape(1, bs, CONV_W - 1, C_CONV))
```

```python
import functools
import math

import jax
import jax.numpy as jnp
from jax import lax
from jax.experimental import pallas as pl
from jax.experimental.pallas import tpu as pltpu

F32 = jnp.float32
BF16 = jnp.bfloat16
HIGHEST = lax.Precision.HIGHEST

H_A = 4
DK_A = 128
DV_A = 128
W_QK_A = H_A * DK_A
W_V_A = H_A * DV_A
C_CONV = 2 * W_QK_A + W_V_A
CONV_W = 4
H_B = 8
HD_B = 64
W_B = H_B * HD_B
MOBA_BLOCK = 256
MOBA_TOPK = 3
ROPE_THETA = 500000.0
ROT_DIM = HD_B // 4
EPS = 1e-6

LANES = 128
SUBLANES = 8
BF16_SUBLANES = 16
MOBA_HEADS_PER_STEP = 8
KMEAN_SLOTS = 64
KMEAN_GROUP = 16
CONV_ROWS = 64
HEADS_PER_LANE_TILE = LANES // HD_B
LOGIT_PAD = LANES
DELTA_CHUNK = LANES
DELTA_TILE = 4 * DELTA_CHUNK
PREP_CHUNKS = 4
DELTA_STEP_SEQS = 2
VMEM_LIMIT = 56 * 1024 * 1024
NEG_INF = float("-inf")
UNCHOSEN_OFFSET = 1e30


def _dot(a, b, dims, precision=None):
    return lax.dot_general(a, b, (dims, ((), ())), precision=precision,
                           preferred_element_type=F32)


def _mm(a, b, precision=None):
    return _dot(a, b, ((1,), (0,)), precision)


def _mm_nt(a, b, precision=None):
    return _dot(a, b, ((1,), (1,)), precision)


def _bf(x):
    return x.astype(BF16)


def _sigmoid(x):
    return 1.0 / (1.0 + jnp.exp(-x))


def _silu(x):
    return x * _sigmoid(x)


def _softplus(x):
    return jnp.maximum(x, 0.0) + jnp.log(1.0 + jnp.exp(-jnp.abs(x)))


def _rms(x, w):
    return x * lax.rsqrt(jnp.mean(x * x, axis=-1, keepdims=True) + EPS) * w


def _iota(shape, dim):
    return lax.broadcasted_iota(jnp.int32, shape, dim)


def _const_spec(shape):
    nd = len(shape)
    return pl.BlockSpec(shape, lambda *_: (0,) * nd, pipeline_mode=pl.Buffered(1))


def _block_rank(sc, n_blocks, axis):
    blk = _iota(sc.shape, axis)
    rank = jnp.zeros(sc.shape, F32)
    for j in range(n_blocks):
        cj = sc[j:j + 1, :] if axis == 0 else sc[:, j:j + 1]
        first_on_tie = jnp.where(blk > j, 1.0, 0.0)
        rank = rank + jnp.where(cj > sc, 1.0, jnp.where(cj == sc, first_on_tie, 0.0))
    return rank


def _split_bf16(x):
    hi = _bf(x)
    return hi, _bf(x - hi.astype(F32))


def _mm_3pass(a, b):
    a_hi, a_lo = _split_bf16(a)
    b_hi, b_lo = _split_bf16(b)
    return _mm(jnp.concatenate([a_hi, a_lo, a_hi], axis=1),
               jnp.concatenate([b_hi, b_hi, b_lo], axis=0))


def _in_proj_kernel(x_ref, nw_ref, wa_ref, wb_ref, wlg_ref, cos_ref, slo_ref, shi_ref, *rest,
                    d_model, tiles_per_seq, kv_transposed):
    if kv_transposed:
        (wqkv_t_ref, cos_t_ref, sin_t_ref, cw_ref,
         qkv_ref, z_ref, qb_ref, kb_ref, vb_ref, ga_ref, gb_ref, lg_ref, conv_ref, hist_s) = rest
    else:
        qkv_ref, z_ref, qb_ref, kb_ref, vb_ref, ga_ref, gb_ref, lg_ref = rest
    step_i = pl.program_id(0)
    xb = _bf(_rms(x_ref[...], nw_ref[...]))
    tm = xb.shape[0]
    half = ROT_DIM // 2

    def conv_act_in_place():
        last = qkv_ref[tm - SUBLANES:tm, :]
        conv_ref[0] = last[SUBLANES - (CONV_W - 1):, :]
        for j in range(C_CONV // LANES):
            lanes = slice(j * LANES, (j + 1) * LANES)
            w = cw_ref[:, lanes]
            for rb in reversed(range(tm // CONV_ROWS)):
                r0 = rb * CONV_ROWS
                y = qkv_ref[r0:r0 + CONV_ROWS, lanes]
                prev = hist_s[:, lanes] if rb == 0 else qkv_ref[r0 - SUBLANES:r0, lanes]
                ext = jnp.concatenate([prev, y], axis=0)
                acc = y * w[CONV_W - 1:CONV_W]
                for i in range(CONV_W - 1):
                    lo = SUBLANES - (CONV_W - 1) + i
                    acc = acc + ext[lo:lo + CONV_ROWS] * w[i:i + 1]
                act = _silu(acc)
                if lanes.start < 2 * W_QK_A:
                    act = act * lax.rsqrt(jnp.sum(act * act, axis=-1, keepdims=True) + EPS)
                if lanes.start < W_QK_A:
                    act = act * (DK_A ** -0.5)
                qkv_ref[r0:r0 + CONV_ROWS, lanes] = act
        hist_s[...] = last

    def cols(c0, n):
        n_a, n_b = wa_ref.shape[1], wb_ref.shape[1]
        if c0 < n_a:
            return _mm(xb, wa_ref[:, c0:c0 + n])
        if c0 < n_a + n_b:
            return _mm(xb, wb_ref[:, c0 - n_a:c0 - n_a + n])
        return _mm(xb, wlg_ref[...])

    def rope_store(dst_ref, c0):
        cos, slo, shi = cos_ref[...], slo_ref[...], shi_ref[...]
        for c in range(W_B // LANES):
            y = cols(c0 + c * LANES, LANES)
            up = pltpu.roll(y, LANES - half, 1)
            dn = pltpu.roll(y, half, 1)
            dst_ref[:, c * LANES:(c + 1) * LANES] = y * cos + up * slo + dn * shi

    def rope_store_t(dst_ref, r0):
        cos_t, sin_t = cos_t_ref[...], sin_t_ref[...]
        y_t = _mm_nt(wqkv_t_ref[r0:r0 + W_B, :], xb)
        for h in range(H_B):
            r = h * HD_B
            x1, x2 = y_t[r:r + half], y_t[r + half:r + ROT_DIM]
            dst_ref[0, r:r + half, :] = x1 * cos_t - x2 * sin_t
            dst_ref[0, r + half:r + ROT_DIM, :] = x2 * cos_t + x1 * sin_t
            dst_ref[0, r + ROT_DIM:r + HD_B, :] = y_t[r + ROT_DIM:r + HD_B]

    if kv_transposed:
        @pl.when(step_i % tiles_per_seq == 0)
        def _():
            hist_s[...] = jnp.zeros_like(hist_s)

    step = 512
    c0 = 0
    for c in range(C_CONV // step):
        qkv_ref[:, c0:c0 + step] = cols(c0, step)
        c0 += step
    if kv_transposed:
        conv_act_in_place()
    z_ref[...] = cols(c0, W_V_A)
    c0 += W_V_A
    if kv_transposed:
        rope_store_t(qb_ref, 0)
        rope_store_t(kb_ref, W_B)
        vb_ref[0] = _mm_nt(wqkv_t_ref[2 * W_B:3 * W_B, :], xb)
    else:
        rope_store(qb_ref, c0)
        rope_store(kb_ref, c0 + W_B)
        vb_ref[...] = cols(c0 + 2 * W_B, W_B)
    c0 += 3 * W_B
    for g_ref in (ga_ref, gb_ref):
        for c in range(d_model // step):
            g_ref[:, c * step:(c + 1) * step] = cols(c0, step)
            c0 += step
    lg_ref[...] = cols(c0, LOGIT_PAD)


def _in_proj(x2d, norm_w, w_r, tabs, *, tm, rows_per_seq, wqkv_t=None, tabs_t=None, conv_w=None):
    rows, d_model = x2d.shape
    n_tiles = rows // tm
    tiles_per_seq = rows_per_seq // tm
    n_seq = rows // rows_per_seq
    kv_transposed = wqkv_t is not None

    def row_spec(n):
        return pl.BlockSpec((tm, n), lambda i: (i, 0))

    def sds(*shape):
        return jax.ShapeDtypeStruct(shape, F32)

    kv_t_spec = pl.BlockSpec((1, W_B, tm), lambda i: (i // tiles_per_seq, 0, i % tiles_per_seq))
    kv_shape = sds(n_seq, W_B, rows_per_seq) if kv_transposed else sds(rows, W_B)
    kv_spec = kv_t_spec if kv_transposed else row_spec(W_B)
    tab_spec = pl.BlockSpec((tm, LANES), lambda i: (i % tiles_per_seq, 0))
    in_specs = [row_spec(d_model), _const_spec((1, d_model)), *[_const_spec(w.shape) for w in w_r],
                tab_spec, tab_spec, tab_spec]
    args = [x2d, norm_w.reshape(1, d_model), *w_r, *tabs]
    out_shape = [sds(rows, C_CONV), sds(rows, W_V_A), kv_shape, kv_shape, kv_shape,
                 sds(rows, d_model), sds(rows, d_model), sds(rows, LOGIT_PAD)]
    out_specs = [row_spec(C_CONV), row_spec(W_V_A), kv_spec, kv_spec, kv_spec,
                 row_spec(d_model), row_spec(d_model), row_spec(LOGIT_PAD)]
    if kv_transposed:
        tab_t_spec = pl.BlockSpec((ROT_DIM // 2, tm), lambda i: (0, i % tiles_per_seq))
        in_specs += [_const_spec(wqkv_t.shape), tab_t_spec, tab_t_spec, _const_spec(conv_w.shape)]
        args += [wqkv_t, *tabs_t, conv_w]
        out_shape.append(sds(n_seq, CONV_W - 1, C_CONV))
        out_specs.append(pl.BlockSpec((1, CONV_W - 1, C_CONV),
                                      lambda i: (i // tiles_per_seq, 0, 0)))
        assert tm >= SUBLANES
        scratch = [pltpu.VMEM((SUBLANES, C_CONV), F32)]
    else:
        scratch = []
    return pl.pallas_call(
        functools.partial(_in_proj_kernel, d_model=d_model, tiles_per_seq=tiles_per_seq,
                          kv_transposed=kv_transposed),
        grid=(n_tiles,),
        in_specs=in_specs,
        out_specs=out_specs,
        out_shape=out_shape,
        scratch_shapes=scratch,
        compiler_params=pltpu.CompilerParams(dimension_semantics=("arbitrary",),
                                             vmem_limit_bytes=VMEM_LIMIT),
    )(*args)


def _rope_angles(pos):
    half = ROT_DIM // 2
    inv_freq = jnp.exp(jnp.arange(half, dtype=F32) * (-2.0 * math.log(ROPE_THETA) / ROT_DIM))
    ang = pos.astype(F32)[:, None] * inv_freq[None, :]
    return jnp.cos(ang), jnp.sin(ang)


def _rope_tables_t(pos):
    cos, sin = _rope_angles(pos)
    return cos.T, sin.T


def _rope_tables(pos):
    half = ROT_DIM // 2
    cos, sin = _rope_angles(pos)
    n = pos.shape[0]
    ones = jnp.ones((n, HD_B - ROT_DIM), F32)
    zeros_h = jnp.zeros((n, half), F32)
    zeros_r = jnp.zeros((n, HD_B - ROT_DIM), F32)
    cos_h = jnp.concatenate([cos, cos, ones], axis=1)
    slo_h = jnp.concatenate([-sin, zeros_h, zeros_r], axis=1)
    shi_h = jnp.concatenate([zeros_h, sin, zeros_r], axis=1)
    tile = lambda t: jnp.tile(t, (1, HEADS_PER_LANE_TILE))
    return tile(cos_h), tile(slo_h), tile(shi_h)


def _delta_prompt_kernel(alog_ref, dtb_ref, x_ref, z_ref, lg_ref, nw_ref,
                         o_ref, s_ref,
                         state_s, u_s, w_s, qg_s, qk_s, kdt_s, gl_s):
    t = pl.program_id(1)
    cs = DELTA_CHUNK
    n_chunks = x_ref.shape[1] // cs
    heads = range(H_A)

    @pl.when(t == 0)
    def _():
        state_s[...] = jnp.zeros_like(state_s)

    sq = (cs, cs)
    row = _iota(sq, 0)
    col = _iota(sq, 1)
    causal = col <= row
    strict = col < row
    tril_b = _bf(jnp.where(causal, 1.0, 0.0))
    eye_f = jnp.where(col == row, 1.0, 0.0)
    lane_row = _iota((1, LANES), 1)
    alog_row = jnp.zeros((1, LANES), F32)
    dtb_row = jnp.zeros((1, LANES), F32)
    for h in heads:
        alog_row = jnp.where(lane_row == H_A + h, alog_ref[h], alog_row)
        dtb_row = jnp.where(lane_row == H_A + h, dtb_ref[h], dtb_row)
    neg_a_row = -jnp.exp(alog_row)

    def prep(ci, carry):
        cidx = [ci * PREP_CHUNKS + d for d in range(PREP_CHUNKS)]
        rows = [pl.ds(pl.multiple_of(c * cs, cs), cs) for c in cidx]
        units = [(d, h) for d in range(PREP_CHUNKS) for h in heads]
        un = range(len(units))
        q = [x_ref[0, rows[d], h * DK_A:(h + 1) * DK_A] for d, h in units]
        k = [x_ref[0, rows[d], W_QK_A + h * DK_A:W_QK_A + (h + 1) * DK_A] for d, h in units]
        v = [x_ref[0, rows[d], 2 * W_QK_A + h * DV_A:2 * W_QK_A + (h + 1) * DV_A] for d, h in units]
        sig, gc_all = [], []
        for d in range(PREP_CHUNKS):
            lg = lg_ref[0, rows[d], :]
            sig.append(_sigmoid(lg))
            rem = neg_a_row * _softplus(lg + dtb_row)
            pieces = []
            for _ in range(3):
                piece = _bf(rem)
                pieces.append(piece)
                rem = rem - piece.astype(F32)
            csum = _mm(tril_b, jnp.concatenate(pieces, axis=1))
            gc_all.append(csum[:, :LANES] + csum[:, LANES:2 * LANES] + csum[:, 2 * LANES:])
        beta = [sig[d][:, h:h + 1] for d, h in units]
        gc = [jnp.broadcast_to(gc_all[d][:, H_A + h:H_A + h + 1], (cs, LANES)) for d, h in units]
        decay = [jnp.exp(jnp.where(causal, gc[u] - gc[u].T, NEG_INF)) for u in un]
        kb = [k[u] * beta[u] for u in un]
        a = [jnp.where(strict, _mm_nt(_bf(kb[u]), _bf(k[u])) * decay[u], 0.0) for u in un]
        t_inv = [eye_f - a[u] for u in un]
        a_pow = [_mm_3pass(a[u], a[u]) for u in un]
        n_levels = int(math.log2(cs)) - 1
        for lvl in range(n_levels):
            if lvl < n_levels - 1:
                prod = [_mm_3pass(a_pow[u], jnp.concatenate([a_pow[u], t_inv[u]], axis=1))
                        for u in un]
                a_pow = [prod[u][:, :cs] for u in un]
                t_inv = [t_inv[u] + prod[u][:, cs:] for u in un]
            else:
                t_inv = [t_inv[u] + _mm_3pass(a_pow[u], t_inv[u]) for u in un]
        e_gc = [jnp.exp(gc[u]) for u in un]
        uw = [_mm(_bf(t_inv[u]),
                  jnp.concatenate([_bf(v[u] * beta[u]), _bf(kb[u] * e_gc[u])], axis=1))
              for u in un]
        qk = [jnp.where(causal, _mm_nt(_bf(q[u]), _bf(k[u])) * decay[u], 0.0) for u in un]
        for u, (d, h) in enumerate(units):
            g_last = gc[u][cs - 1:cs, :]
            u_s[h, rows[d], :] = uw[u][:, :DV_A]
            w_s[h, rows[d], :] = _bf(uw[u][:, DV_A:])
            qk_s[h, rows[d], :] = _bf(qk[u])
            qg_s[h, rows[d], :] = _bf(q[u] * e_gc[u])
            kdt_s[h, cidx[d]] = _bf((k[u] * jnp.exp(g_last - gc[u])).T)
            gl_s[h, cidx[d]] = jnp.broadcast_to(jnp.exp(g_last), (SUBLANES, LANES))
        return carry

    lax.fori_loop(0, n_chunks // PREP_CHUNKS, prep, 0)

    nw = nw_ref[...]

    def scan(c, carry):
        rows = pl.ds(pl.multiple_of(c * cs, cs), cs)
        state = [state_s[h] for h in heads]
        sb = [_bf(state[h]) for h in heads]
        ws = [_mm(jnp.concatenate([w_s[h, rows, :], qg_s[h, rows, :]], axis=0), sb[h])
              for h in heads]
        vb = [_bf(u_s[h, rows, :] - ws[h][:cs]) for h in heads]
        o = [ws[h][cs:] + _mm(qk_s[h, rows, :], vb[h]) for h in heads]
        new = [state[h] * gl_s[h, c][0:1] + _mm(kdt_s[h, c], vb[h]) for h in heads]
        for h in heads:
            state_s[h] = new[h]
            lanes = slice(h * DV_A, (h + 1) * DV_A)
            o_ref[0, rows, lanes] = _bf(_rms(o[h], nw) * _silu(z_ref[0, rows, lanes]))
        return carry

    lax.fori_loop(0, n_chunks, scan, 0)

    @pl.when(t == pl.num_programs(1) - 1)
    def _():
        s_ref[0] = state_s[...]


def _delta_prompt(qkv, z, lg, a_log, dt_bias, norm_w):
    bsz, seq, _ = qkv.shape
    ts = min(DELTA_TILE, seq)
    n_chunks = ts // DELTA_CHUNK
    assert seq % ts == 0 and ts % (DELTA_CHUNK * PREP_CHUNKS) == 0
    assert DK_A == LANES and DV_A == LANES

    def tile_spec(n):
        return pl.BlockSpec((1, ts, n), lambda b, t: (b, t, 0))

    def per_head(shape, dtype):
        return pltpu.VMEM((H_A,) + shape, dtype)

    smem = pl.BlockSpec(memory_space=pltpu.SMEM)
    return pl.pallas_call(
        _delta_prompt_kernel,
        grid=(bsz, seq // ts),
        in_specs=[smem, smem, tile_spec(C_CONV), tile_spec(W_V_A), tile_spec(LOGIT_PAD),
                  _const_spec((1, DV_A))],
        out_specs=[tile_spec(W_V_A),
                   pl.BlockSpec((1, H_A, DK_A, DV_A), lambda b, t: (b, 0, 0, 0))],
        out_shape=[jax.ShapeDtypeStruct((bsz, seq, W_V_A), BF16),
                   jax.ShapeDtypeStruct((bsz, H_A, DK_A, DV_A), F32)],
        scratch_shapes=[per_head((DK_A, DV_A), F32),
                        per_head((ts, DV_A), F32),
                        per_head((ts, DK_A), BF16),
                        per_head((ts, DK_A), BF16),
                        per_head((ts, DELTA_CHUNK), BF16),
                        per_head((n_chunks, DK_A, DELTA_CHUNK), BF16),
                        per_head((n_chunks, SUBLANES, LANES), F32)],
        compiler_params=pltpu.CompilerParams(dimension_semantics=("arbitrary", "arbitrary"),
                                             vmem_limit_bytes=VMEM_LIMIT),
    )(a_log, dt_bias, qkv, z, lg, norm_w.reshape(1, DV_A))


def _moba_prompt_kernel(qt_ref, kt_ref, vt_ref, o_ref, kn_s, vt_s, km_s, ch_s, *, n_blocks):
    qi = pl.program_id(2)
    bs = MOBA_BLOCK
    n_pairs = qt_ref.shape[1] // LANES
    heads = range(n_pairs * HEADS_PER_LANE_TILE)
    pair_of = [h // HEADS_PER_LANE_TILE for h in heads]

    @pl.when(qi == 0)
    def _():
        km_s[...] = jnp.zeros_like(km_s)
        ones = jnp.ones((BF16_SUBLANES, bs), BF16)
        for j in range(n_blocks):
            keys = slice(j * bs, (j + 1) * bs)
            for pr in range(n_pairs):
                kj = kt_ref[0, pr * LANES:(pr + 1) * LANES, keys].T
                kn_s[j, pr] = _bf(kj)
                km_s[pr, j:j + 1, :] = jnp.mean(kj, axis=0, keepdims=True)
            for h in heads:
                vt_s[j, h, 0:HD_B, :] = _bf(vt_ref[0, h * HD_B:(h + 1) * HD_B, keys])
                vt_s[j, h, HD_B:, :] = ones

    d_row = _iota((LANES, bs), 0)
    blk = _iota((km_s.shape[1], bs), 0)
    key_le_query = _iota((bs, bs), 0) <= _iota((bs, bs), 1)
    qh = []
    for h in heads:
        hh = h % HEADS_PER_LANE_TILE
        qt = qt_ref[0, pair_of[h] * LANES:(pair_of[h] + 1) * LANES, :]
        qh.append(jnp.where((d_row >= hh * HD_B) & (d_row < (hh + 1) * HD_B), qt, 0.0))
    sc = [jnp.where(blk < qi, _mm_3pass(km_s[pair_of[h]], qh[h]), NEG_INF) for h in heads]
    rank = [_block_rank(sc[h], n_blocks, 0) for h in heads]
    for h in heads:
        ch_s[h] = jnp.where(blk < qi, jnp.where(rank[h] < MOBA_TOPK, 1.0, 0.0), 0.0)

    qs = [_bf(qh[h] * (HD_B ** -0.5 * math.log2(math.e))) for h in heads]
    s = [jnp.where(key_le_query, _mm(kn_s[qi, pair_of[h]], qs[h]), NEG_INF)
         for h in heads]
    m = [jnp.max(s[h], axis=0, keepdims=True) for h in heads]
    acc = [_mm(vt_s[qi, h], _bf(jnp.exp2(s[h] - m[h]))) for h in heads]

    def two_past_blocks(jj, carry):
        m, acc = carry
        js = (2 * jj, 2 * jj + 1)
        pick = [[ch_s[h, pl.ds(j, 1), :] > 0.0 for j in js] for h in heads]
        s = [[_mm(kn_s[j, pair_of[h]], qs[h]) for j in js] for h in heads]
        col_max = [[jnp.where(pick[h][i], jnp.max(s[h][i], axis=0, keepdims=True), NEG_INF)
                    for i in range(2)] for h in heads]
        m_new = [jnp.maximum(m[h], jnp.maximum(col_max[h][0], col_max[h][1])) for h in heads]
        off = [[jnp.where(pick[h][i], m_new[h], UNCHOSEN_OFFSET) for i in range(2)] for h in heads]
        acc = [jnp.exp2(m[h] - m_new[h]) * acc[h]
               + _mm(vt_s[js[0], h], _bf(jnp.exp2(s[h][0] - off[h][0])))
               + _mm(vt_s[js[1], h], _bf(jnp.exp2(s[h][1] - off[h][1]))) for h in heads]
        return m_new, acc

    m, acc = lax.fori_loop(0, (qi + 1) // 2, two_past_blocks, (m, acc))
    o_t = jnp.concatenate([acc[h][:HD_B] / acc[h][HD_B:HD_B + 1] for h in heads], axis=0)
    o_ref[0] = _bf(o_t.T)


def _moba_prompt(qt, kt, vt):
    bsz, _, seq = qt.shape
    n_blocks = seq // MOBA_BLOCK
    assert seq % MOBA_BLOCK == 0
    blocks_pad = -(-n_blocks // SUBLANES) * SUBLANES
    n_pairs = MOBA_HEADS_PER_STEP // HEADS_PER_LANE_TILE
    rows = n_pairs * LANES
    q_spec = pl.BlockSpec((1, rows, MOBA_BLOCK), lambda b, p, i: (b, p, i))
    kv_spec = pl.BlockSpec((1, rows, seq), lambda b, p, i: (b, p, 0))
    return pl.pallas_call(
        functools.partial(_moba_prompt_kernel, n_blocks=n_blocks),
        grid=(bsz, H_B // MOBA_HEADS_PER_STEP, n_blocks),
        in_specs=[q_spec, kv_spec, kv_spec],
        out_specs=pl.BlockSpec((1, MOBA_BLOCK, rows), lambda b, p, i: (b, i, p)),
        out_shape=jax.ShapeDtypeStruct((bsz, seq, W_B), BF16),
        scratch_shapes=[pltpu.VMEM((n_blocks, n_pairs, MOBA_BLOCK, LANES), BF16),
                        pltpu.VMEM((n_blocks, MOBA_HEADS_PER_STEP, HD_B + BF16_SUBLANES,
                                    MOBA_BLOCK), BF16),
                        pltpu.VMEM((n_pairs, blocks_pad, LANES), F32),
                        pltpu.VMEM((MOBA_HEADS_PER_STEP, blocks_pad, MOBA_BLOCK), F32)],
        compiler_params=pltpu.CompilerParams(
            dimension_semantics=("arbitrary", "arbitrary", "arbitrary"),
            vmem_limit_bytes=VMEM_LIMIT),
    )(qt, kt, vt)


def _key_mean_groups(pt_ref, cache_ref, km_ref, page_buf, page_sems, stream):
    pages_per_step, pages_per_seq, total, ppb, page_size = stream
    step_i = pl.program_id(0)
    base = step_i * pages_per_step
    blocks_per_group = KMEAN_GROUP // ppb
    lane = _iota((HD_B, LANES), 1)

    def page_copy(p, slot):
        return pltpu.make_async_copy(cache_ref.at[pt_ref[p]], page_buf.at[slot], page_sems.at[slot])

    @pl.when(step_i == 0)
    def _():
        for s in range(KMEAN_SLOTS):
            page_copy(s, s).start()

    @pl.when(base % pages_per_seq == 0)
    def _():
        km_ref[...] = jnp.zeros_like(km_ref)

    def consume_group(g):
        p0 = base + g * KMEAN_GROUP
        slot0 = (g * KMEAN_GROUP) % KMEAN_SLOTS
        for j in range(KMEAN_GROUP):
            page_copy(p0 + j, slot0 + j).wait()
        blk0 = (base % pages_per_seq) // ppb + g * blocks_per_group
        for h in range(H_B):
            acc = jnp.zeros((HD_B, LANES), F32)
            for k in range(blocks_per_group):
                x = page_buf[slot0 + k * ppb, h]
                for r in range(1, ppb):
                    x = x + page_buf[slot0 + k * ppb + r, h]
                mean = jnp.sum(x, axis=-1, keepdims=True) * (1.0 / (ppb * page_size))
                acc = jnp.where(lane == blk0 + k, mean, acc)
            rows = slice(h * HD_B, (h + 1) * HD_B)
            km_ref[0, rows, :] = km_ref[0, rows, :] + acc
        for j in range(KMEAN_GROUP):
            @pl.when(p0 + j + KMEAN_SLOTS < total)
            def _():
                page_copy(p0 + j + KMEAN_SLOTS, slot0 + j).start()

    return [functools.partial(consume_group, g) for g in range(pages_per_step // KMEAN_GROUP)]


def _merge_mlp_kernel(*refs, ff_step, stream):
    if stream is None:
        (x_ref, oa_ref, ob_ref, ga_ref, gb_ref, wpa_ref, wpb_ref, wo_ref, n2_ref, wup_ref, wdn_ref,
         nf_ref, y_ref) = refs
        groups = []
    else:
        (pt_ref, x_ref, oa_ref, ob_ref, ga_ref, gb_ref, wpa_ref, wpb_ref, wo_ref, n2_ref, wup_ref,
         wdn_ref, nf_ref, cache_ref, y_ref, km_ref, page_buf, page_sems) = refs
        groups = _key_mean_groups(pt_ref, cache_ref, km_ref, page_buf, page_sems, stream)
    d_ff = wup_ref.shape[1]
    n_ff = d_ff // ff_step

    y_a = _mm(oa_ref[...], wpa_ref[...])
    y_b = _mm(ob_ref[...], wpb_ref[...])
    mixed = _sigmoid(ga_ref[...]) * y_a + _sigmoid(gb_ref[...]) * y_b
    h = x_ref[...] + _mm(_bf(mixed), wo_ref[...])
    hn = _bf(_rms(h, n2_ref[...]))
    out = h
    for c in range(n_ff):
        for g in groups[c * len(groups) // n_ff:(c + 1) * len(groups) // n_ff]:
            g()
        u = jnp.maximum(_mm(hn, wup_ref[:, c * ff_step:(c + 1) * ff_step]), 0.0)
        out = out + _mm(_bf(u * u), wdn_ref[c * ff_step:(c + 1) * ff_step, :])
    y_ref[...] = _rms(out, nf_ref[...])


def _merge_mlp(x2d, oa, ob, ga, gb, wpa, wpb, wo, n2, wup, wdn, nf, *, tm, key_stream=None):
    rows, d_model = x2d.shape
    d_ff = wup.shape[1]
    n_tiles = rows // tm

    def row_spec(n):
        return pl.BlockSpec((tm, n), lambda i, *_: (i, 0))

    in_specs = [row_spec(d_model), row_spec(W_V_A), row_spec(W_B), row_spec(d_model),
                row_spec(d_model),
                _const_spec(wpa.shape), _const_spec(wpb.shape), _const_spec(wo.shape),
                _const_spec((1, d_model)), _const_spec(wup.shape), _const_spec(wdn.shape),
                _const_spec((1, d_model))]
    args = [x2d, oa, ob, ga, gb, wpa, wpb, wo, n2.reshape(1, d_model), wup, wdn,
            nf.reshape(1, d_model)]
    y_shape = jax.ShapeDtypeStruct((rows, d_model), F32)
    params = dict(dimension_semantics=("arbitrary",), vmem_limit_bytes=VMEM_LIMIT)
    ff_step = min(d_ff, 1024)
    if key_stream is None:
        return pl.pallas_call(
            functools.partial(_merge_mlp_kernel, ff_step=ff_step, stream=None),
            grid=(n_tiles,), in_specs=in_specs, out_specs=row_spec(d_model), out_shape=y_shape,
            compiler_params=pltpu.CompilerParams(**params),
        )(*args)

    page_ids, cache_t, ppb, pages_per_seq = key_stream
    total = page_ids.shape[0]
    _, _, _, page_size = cache_t.shape
    pages_per_step = total // n_tiles
    assert (pages_per_step * n_tiles == total and pages_per_step % KMEAN_SLOTS == 0
            and pages_per_seq % pages_per_step == 0 and KMEAN_GROUP % ppb == 0
            and KMEAN_SLOTS % KMEAN_GROUP == 0 and pages_per_seq // ppb <= LANES)
    km_shape = jax.ShapeDtypeStruct((total // pages_per_seq, W_B, LANES), F32)
    km_spec = pl.BlockSpec((1, W_B, LANES),
                           lambda i, *_: (i * pages_per_step // pages_per_seq, 0, 0))
    return pl.pallas_call(
        functools.partial(_merge_mlp_kernel, ff_step=ff_step,
                          stream=(pages_per_step, pages_per_seq, total, ppb, page_size)),
        grid_spec=pltpu.PrefetchScalarGridSpec(
            num_scalar_prefetch=1, grid=(n_tiles,),
            in_specs=in_specs + [pl.BlockSpec(memory_space=pl.ANY)],
            out_specs=[row_spec(d_model), km_spec],
            scratch_shapes=[pltpu.VMEM((KMEAN_SLOTS, H_B, HD_B, page_size), F32),
                            pltpu.SemaphoreType.DMA((KMEAN_SLOTS,))]),
        out_shape=[y_shape, km_shape],
        compiler_params=pltpu.CompilerParams(disable_bounds_checks=True, **params),
    )(page_ids, *args, cache_t)


def _delta_step_kernel(alog_ref, dtb_ref, x_ref, z_ref, lg_ref, sc_ref, cw_ref, sd_ref, nw_ref,
                       o_ref, snew_ref, cnew_ref):
    n_seq = x_ref.shape[0]
    w = cw_ref[...]
    y = []
    for s in range(n_seq):
        x = x_ref[s]
        hist = sc_ref[s]
        acc = x * w[CONV_W - 1:CONV_W]
        for i in range(CONV_W - 1):
            acc = acc + hist[i:i + 1] * w[i:i + 1]
        y.append(_silu(acc))
        cnew_ref[s, 0:CONV_W - 2, :] = hist[1:CONV_W - 1]
        cnew_ref[s, CONV_W - 2:CONV_W - 1, :] = x
    nw = nw_ref[...]
    sq = (DK_A, DK_A)
    eye = _iota(sq, 0) == _iota(sq, 1)

    def l2n(t):
        return t * lax.rsqrt(jnp.sum(t * t, axis=-1, keepdims=True) + EPS)

    units = [(s, h) for s in range(n_seq) for h in range(H_A)]
    un = range(len(units))
    q = [l2n(y[s][:, h * DK_A:(h + 1) * DK_A]) * (DK_A ** -0.5) for s, h in units]
    k = [l2n(y[s][:, W_QK_A + h * DK_A:W_QK_A + (h + 1) * DK_A]) for s, h in units]
    v = [y[s][:, 2 * W_QK_A + h * DV_A:2 * W_QK_A + (h + 1) * DV_A] for s, h in units]
    beta = [_sigmoid(lg_ref[s][:, h:h + 1]) for s, h in units]
    g = [-jnp.exp(jnp.full((1, 1), alog_ref[h], F32))
         * _softplus(lg_ref[s][:, H_A + h:H_A + h + 1] + dtb_ref[h]) for s, h in units]
    state = [sd_ref[s, h] * jnp.exp(g[u]) for u, (s, h) in enumerate(units)]
    kv = [_mm(jnp.broadcast_to(k[u], (SUBLANES, DK_A)), state[u], HIGHEST)[0:1] for u in un]
    dv = [(v[u] - kv[u]) * beta[u] for u in un]
    k_diag = [jnp.where(eye, jnp.broadcast_to(k[u], sq), 0.0) for u in un]
    state = [state[u] + _mm(k_diag[u], jnp.broadcast_to(dv[u], (DK_A, DV_A)), HIGHEST)
             for u in un]
    o = [_mm(jnp.broadcast_to(q[u], (SUBLANES, DK_A)), state[u], HIGHEST)[0:1] for u in un]
    for u, (s, h) in enumerate(units):
        snew_ref[s, h] = state[u]
        gate = _silu(z_ref[s][:, h * DV_A:(h + 1) * DV_A])
        o_ref[s, :, h * DV_A:(h + 1) * DV_A] = _bf(_rms(o[u], nw) * gate)


def _delta_step(qkv, z, lg, state_conv, conv_w, state_delta, a_log, dt_bias, norm_w):
    nb = qkv.shape[0]
    smem = pl.BlockSpec(memory_space=pltpu.SMEM)

    per_step = math.gcd(nb, DELTA_STEP_SEQS)

    def per_b(shape):
        nd = len(shape)
        return pl.BlockSpec((per_step,) + shape, lambda b: (b,) + (0,) * nd)

    return pl.pallas_call(
        _delta_step_kernel,
        grid=(nb // per_step,),
        in_specs=[smem, smem, per_b((1, C_CONV)), per_b((1, W_V_A)), per_b((1, LOGIT_PAD)),
                  per_b((CONV_W - 1, C_CONV)), _const_spec((CONV_W, C_CONV)),
                  per_b((H_A, DK_A, DV_A)), _const_spec((1, DV_A))],
        out_specs=[per_b((1, W_V_A)), per_b((H_A, DK_A, DV_A)), per_b((CONV_W - 1, C_CONV))],
        out_shape=[jax.ShapeDtypeStruct((nb, 1, W_V_A), BF16),
                   jax.ShapeDtypeStruct((nb, H_A, DK_A, DV_A), F32),
                   jax.ShapeDtypeStruct((nb, CONV_W - 1, C_CONV), F32)],
        compiler_params=pltpu.CompilerParams(dimension_semantics=("arbitrary",),
                                             vmem_limit_bytes=VMEM_LIMIT),
    )(a_log, dt_bias, qkv.reshape(nb, 1, C_CONV), z.reshape(nb, 1, W_V_A),
      lg.reshape(nb, 1, LOGIT_PAD), state_conv, conv_w, state_delta, norm_w.reshape(1, DV_A))


def _topk_kernel(qt_ref, km_ref, idx_ref, *, n_full, kk):
    n_seq = km_ref.shape[0]
    qt = qt_ref[...]
    per_seq = []
    for b in range(n_seq):
        prod = km_ref[b] * qt[:, b:b + 1]
        per_seq.append(jnp.concatenate(
            [jnp.sum(prod[h * HD_B:(h + 1) * HD_B], axis=0, keepdims=True) for h in range(H_B)],
            axis=0))
    sc = jnp.concatenate(per_seq, axis=0)
    blk = _iota(sc.shape, 1)
    sc = jnp.where(blk < n_full, sc, NEG_INF)
    rank = _block_rank(sc, n_full, 1)
    blk_f = blk.astype(F32)
    out = jnp.zeros(sc.shape, F32)
    for r in range(kk):
        pick = jnp.sum(jnp.where(rank == float(r), blk_f, 0.0), axis=-1, keepdims=True)
        out = jnp.where(blk == r, pick, out)
    out = out.astype(jnp.int32)
    for b in range(n_seq):
        idx_ref[b] = out[b * H_B:(b + 1) * H_B]


def _topk(q_t, km, *, n_full, kk):
    n_seq = km.shape[0]
    return pl.pallas_call(
        functools.partial(_topk_kernel, n_full=n_full, kk=kk),
        grid=(1,),
        in_specs=[_const_spec(q_t.shape), _const_spec(km.shape)],
        out_specs=pl.BlockSpec((n_seq, H_B, LANES), lambda i: (0, 0, 0)),
        out_shape=jax.ShapeDtypeStruct((n_seq, H_B, LANES), jnp.int32),
        compiler_params=pltpu.CompilerParams(dimension_semantics=("arbitrary",),
                                             vmem_limit_bytes=VMEM_LIMIT),
    )(q_t, km)


_PAGE_DMA_PARAMS = pltpu.CompilerParams(dimension_semantics=("arbitrary",),
                                        vmem_limit_bytes=VMEM_LIMIT, disable_bounds_checks=True)


def _moba_sample_kernel(top_ref, pt_ref, q_ref, kn_ref, vn_ref, ck_ref, cv_ref, o_ref,
                        kbuf, vbuf, sems, *, kk, ppb, n_pages, page_size, n_seq):
    b = pl.program_id(0)
    pages_per_head = kk * ppb

    def copies(seq, h, i, page=None):
        slot = seq % 2
        if page is None:
            blk = top_ref[(seq * H_B + h) * kk + i // ppb]
            page = pt_ref[seq * n_pages + blk * ppb + i % ppb]
        keys = pl.ds(i * page_size, page_size)
        return (pltpu.make_async_copy(ck_ref.at[page, h], kbuf.at[slot, h, :, keys],
                                      sems.at[slot, 0, h]),
                pltpu.make_async_copy(cv_ref.at[page, h], vbuf.at[slot, h, :, keys],
                                      sems.at[slot, 1, h]))

    def start_fetch(seq):
        for h in range(H_B):
            for i in range(pages_per_head):
                for priority, c in enumerate(copies(seq, h, i)):
                    c.start(priority=priority)

    @pl.when(b == 0)
    def _():
        start_fetch(b)

    @pl.when(b + 1 < n_seq)
    def _():
        start_fetch(b + 1)

    slot = b % 2
    q = q_ref[0]
    kn = kn_ref[0]
    vn = vn_ref[0]
    heads = range(H_B)
    for h in heads:
        for i in range(pages_per_head):
            for c in copies(b, h, i, page=0):
                c.wait()
    sl = [slice(h * HD_B, (h + 1) * HD_B) for h in heads]
    qh = [q[:, sl[h]] * (HD_B ** -0.5) for h in heads]
    s = [_mm(_bf(jnp.broadcast_to(qh[h], (SUBLANES, HD_B))), _bf(kbuf[slot, h]))[0:1]
         for h in heads]
    s_own = [jnp.sum(qh[h] * kn[:, sl[h]], axis=-1, keepdims=True) for h in heads]
    m = [jnp.maximum(jnp.max(s[h], axis=-1, keepdims=True), s_own[h]) for h in heads]
    p = [jnp.exp(s[h] - m[h]) for h in heads]
    p_own = [jnp.exp(s_own[h] - m[h]) for h in heads]
    l = [jnp.sum(p[h], axis=-1, keepdims=True) + p_own[h] for h in heads]
    pv = [_mm_nt(_bf(jnp.broadcast_to(p[h], (SUBLANES, p[h].shape[1]))), _bf(vbuf[slot, h]))[0:1]
          for h in heads]
    for h in heads:
        o_ref[0, :, sl[h]] = _bf((pv[h] + p_own[h] * vn[:, sl[h]]) / l[h])


def _moba_sample(top_flat, pt_flat, q3, kn3, vn3, cache_kt, cache_vt, *, kk, ppb):
    nb = q3.shape[0]
    _, _, _, page_size = cache_kt.shape
    keys = kk * ppb * page_size
    row = pl.BlockSpec((1, 1, W_B), lambda b, *_: (b, 0, 0))
    any_spec = pl.BlockSpec(memory_space=pl.ANY)
    return pl.pallas_call(
        functools.partial(_moba_sample_kernel, kk=kk, ppb=ppb,
                          n_pages=pt_flat.shape[0] // nb, page_size=page_size, n_seq=nb),
        grid_spec=pltpu.PrefetchScalarGridSpec(
            num_scalar_prefetch=2,
            grid=(nb,),
            in_specs=[row, row, row, any_spec, any_spec],
            out_specs=row,
            scratch_shapes=[pltpu.VMEM((2, H_B, HD_B, keys), F32),
                            pltpu.VMEM((2, H_B, HD_B, keys), F32),
                            pltpu.SemaphoreType.DMA((2, 2, H_B))]),
        out_shape=jax.ShapeDtypeStruct((nb, 1, W_B), BF16),
        compiler_params=_PAGE_DMA_PARAMS,
    )(top_flat, pt_flat, q3, kn3, vn3, cache_kt, cache_vt)


def _rearranged_w_in(w_in, d_model):
    o_lg = C_CONV + W_V_A
    o_b = o_lg + 2 * H_A
    w = _bf(w_in)
    pad = jnp.zeros((d_model, LOGIT_PAD - 2 * H_A), BF16)
    return w[:, :o_lg], w[:, o_b:], jnp.concatenate([w[:, o_lg:o_b], pad], axis=1)


def kernel(x_prompt, x_sample, cache_k, cache_v, page_table, state_delta, state_conv, norm1_w,
           w_in, conv_w, a_log, dt_bias, delta_norm_w, w_proj_a, w_proj_b, w_out, norm2_w, w_up,
           w_down, norm_f_w):
    depth = w_in.shape[0]
    assert depth == 1, "single-layer trunk"
    bp, sp, d_model = x_prompt.shape
    bs, ss, _ = x_sample.shape
    assert ss == 1, "one new token per sample sequence"
    _, n_pool, page_size, _, _ = cache_k.shape
    n_pages = page_table.shape[1]
    past_len = n_pages * page_size
    ppb = MOBA_BLOCK // page_size
    n_full = past_len // MOBA_BLOCK
    assert n_full * MOBA_BLOCK == past_len, "the sample token starts a fresh MoBA block"
    kk = min(MOBA_TOPK, n_full)
    assert kk >= 1

    w_r = _rearranged_w_in(w_in[0], d_model)
    o_qb = C_CONV + W_V_A + 2 * H_A
    wqkv_t = _bf(w_in[0][:, o_qb:o_qb + 3 * W_B].T)
    wpa, wpb, wo = _bf(w_proj_a[0]), _bf(w_proj_b[0]), _bf(w_out[0])
    wup, wdn = _bf(w_up[0]), _bf(w_down[0])

    tm = 256
    pos_p = jnp.arange(sp, dtype=jnp.int32)
    (qkv, z, qt, kt, vt, ga, gb, lg, conv_p) = _in_proj(
        x_prompt.reshape(bp * sp, d_model), norm1_w[0], w_r, _rope_tables(pos_p),
        tm=2 * tm, rows_per_seq=sp, wqkv_t=wqkv_t, tabs_t=_rope_tables_t(pos_p), conv_w=conv_w[0])
    o_a, s_p = _delta_prompt(qkv.reshape(bp, sp, C_CONV), z.reshape(bp, sp, W_V_A),
                             lg.reshape(bp, sp, LOGIT_PAD), a_log[0], dt_bias[0],
                             delta_norm_w[0])
    o_b = _moba_prompt(qt, kt, vt)
    cache_kt = cache_k.reshape(n_pool, page_size, H_B, HD_B).transpose(0, 2, 3, 1)
    cache_vt = cache_v.reshape(n_pool, page_size, H_B, HD_B).transpose(0, 2, 3, 1)
    past_pages = page_table[:, :n_full * ppb].reshape(-1)
    y_p, km = _merge_mlp(x_prompt.reshape(bp * sp, d_model), o_a.reshape(bp * sp, W_V_A),
                         o_b.reshape(bp * sp, W_B), ga, gb, wpa, wpb, wo, norm2_w[0], wup, wdn,
                         norm_f_w, tm=tm, key_stream=(past_pages, cache_kt, ppb, n_full * ppb))

    pos_s = jnp.full((bs,), past_len, dtype=jnp.int32)
    (qkv_s, z_s, qr_s, kr_s, vr_s, ga_s, gb_s, lg_s) = _in_proj(
        x_sample.reshape(bs, d_model), norm1_w[0], w_r, _rope_tables(pos_s),
        tm=bs, rows_per_seq=bs)
    o_a_s, s_s, conv_s = _delta_step(qkv_s, z_s, lg_s, state_conv[0], conv_w[0], state_delta[0],
                                     a_log[0], dt_bias[0], delta_norm_w[0])
    top = _topk(qr_s.T, km, n_full=n_full, kk=kk)
    top_flat = top[:, :, :kk].reshape(-1)
    o_b_s = _moba_sample(top_flat, page_table.reshape(-1), qr_s.reshape(bs, 1, W_B),
                         kr_s.reshape(bs, 1, W_B), vr_s.reshape(bs, 1, W_B), cache_kt, cache_vt,
                         kk=kk, ppb=ppb)
    y_s = _merge_mlp(x_sample.reshape(bs, d_model), o_a_s.reshape(bs, W_V_A),
                     o_b_s.reshape(bs, W_B), ga_s, gb_s, wpa, wpb, wo, norm2_w[0], wup, wdn,
                     norm_f_w, tm=bs)

    def kv_out(t):
        return t.reshape(1, bp, H_B, HD_B, sp).transpose(0, 1, 4, 2, 3)

    return (y_p.reshape(bp, sp, d_model), y_s.reshape(bs, ss, d_model),
            kv_out(kt), kv_out(vt),
            s_p.reshape(1, bp, H_A, DK_A, DV_A), conv_p.reshape(1, bp, CONV_W - 1, C_CONV),
            kr_s.reshape(1, bs, ss, H_B, HD_B), vr_s.reshape(1, bs, ss, H_B, HD_B),
            s_s.reshape(1, bs, H_A, DK_A, DV_A), conv_s.reshape(1, bs, CONV_W - 1, C_CONV))
```
